```python
import math
import jax
import jax.numpy as jnp
from jax import lax
import numpy as np

D_MODEL = 1024
BATCH = 8
SEQ = 8192
DEPTH = 4

DN_HEADS = 8
DN_DK = 64
DN_DV = 64
DN_CONV = 4
DN_CHUNK = 64
DF_HEADS = 4
DF_D = 64
DF_QBLK = 128
REL_BUCKETS = 32
REL_MAX_DIST = 128
D_FF = 2816
N_EXPERTS = 8
TOP_K = 2
D_FF_EXPERT = 3584
MOE_BLK = 512
LN_EPS = 1e-5
RMS_EPS = 1e-6
ALPHA = (2 * DEPTH) ** 0.25
BETA_INIT = (8 * DEPTH) ** -0.25

DN_QK_W = DN_HEADS * DN_DK
DN_V_W = DN_HEADS * DN_DV
DN_CONV_CH = 2 * DN_QK_W + DN_V_W
DF_QK_W = DF_HEADS * 2 * DF_D
DF_V_W = DF_HEADS * 2 * DF_D
MIX_W = DN_V_W + DF_V_W
_IN_SIZES = (DN_CONV_CH, DN_V_W, DN_HEADS, DN_HEADS, DF_QK_W, DF_QK_W, DF_V_W)
IN_W = sum(_IN_SIZES)
IN_OFFSETS = tuple(int(o) for o in np.cumsum(_IN_SIZES)[:-1])
N_DENSE = (DEPTH + 1) // 2
N_MOE = DEPTH // 2

kernel_name = 'hybrid_deltanet_diffattn_moe_deepnorm'

F32 = jnp.float32


def layer_norm(x, g, b):
    xf = x.astype(F32)
    mu = jnp.mean(xf, axis=-1, keepdims=True)
    var = jnp.mean(jnp.square(xf - mu), axis=-1, keepdims=True)
    return ((xf - mu) * lax.rsqrt(var + LN_EPS) * g + b).astype(x.dtype)


def rms_norm(x, w):
    xf = x.astype(F32)
    return xf * lax.rsqrt(jnp.mean(jnp.square(xf), axis=-1, keepdims=True) + RMS_EPS) * w.astype(F32)


def l2norm(x):
    return x * lax.rsqrt(jnp.sum(jnp.square(x), axis=-1, keepdims=True) + 1e-6)


def t5_causal_bucket(dist):
    max_exact = REL_BUCKETS // 2
    d = jnp.maximum(dist, 1).astype(F32)
    large = max_exact + (jnp.log(d / max_exact) / math.log(REL_MAX_DIST / max_exact)
                         * (REL_BUCKETS - max_exact)).astype(jnp.int32)
    large = jnp.minimum(large, REL_BUCKETS - 1)
    return jnp.where(dist < max_exact, dist, large)


def causal_depthwise_conv(x, w):
    return lax.conv_general_dilated(
        x, w[:, None, :].astype(x.dtype), window_strides=(1,), padding=[(DN_CONV - 1, 0)],
        dimension_numbers=('NWC', 'WIO', 'NWC'), feature_group_count=x.shape[-1])


def gated_delta_rule(q, k, v, g, beta):
    bsz, nh, seq, dk = q.shape
    dv = v.shape[-1]
    c = DN_CHUNK
    n = seq // c
    q = q.reshape(bsz, nh, n, c, dk)
    k = k.reshape(bsz, nh, n, c, dk)
    v = v.reshape(bsz, nh, n, c, dv)
    g = g.reshape(bsz, nh, n, c)
    beta = beta.reshape(bsz, nh, n, c)
    gc = jnp.cumsum(g, axis=-1)
    causal = jnp.tril(jnp.ones((c, c), bool))
    strict = jnp.tril(jnp.ones((c, c), bool), -1)
    decay = jnp.exp(jnp.where(causal, gc[..., :, None] - gc[..., None, :], -jnp.inf))
    kb = k * beta[..., None]
    lower = jnp.where(strict, jnp.einsum('bhnid,bhnjd->bhnij', kb, k) * decay, 0.0)
    tri = lower + jnp.eye(c, dtype=q.dtype)
    rhs = jnp.concatenate([v * beta[..., None], kb * jnp.exp(gc)[..., None]], axis=-1)
    sol = lax.linalg.triangular_solve(tri, rhs, left_side=True, lower=True, unit_diagonal=True)
    u, w = sol[..., :dv], sol[..., dv:]
    intra = jnp.where(causal, jnp.einsum('bhnid,bhnjd->bhnij', q, k) * decay, 0.0)
    q_dec = q * jnp.exp(gc)[..., None]
    k_dec = k * jnp.exp(gc[..., -1:] - gc)[..., None]
    g_last = jnp.exp(gc[..., -1])

    def step(state, inp):
        q_i, k_i, u_i, w_i, a_i, gl_i = inp
        v_new = u_i - jnp.einsum('bhcd,bhde->bhce', w_i, state)
        o = jnp.einsum('bhcd,bhde->bhce', q_i, state) + jnp.einsum('bhij,bhje->bhie', a_i, v_new)
        state = state * gl_i[..., None, None] + jnp.einsum('bhcd,bhce->bhde', k_i, v_new)
        return state, o

    xs = tuple(jnp.moveaxis(t, 2, 0) for t in (q_dec, k_dec, u, w, intra, g_last))
    state0 = jnp.zeros((bsz, nh, dk, dv), q.dtype)
    _, o = lax.scan(step, state0, xs)
    return jnp.moveaxis(o, 0, 2).reshape(bsz, nh, seq, dv)


def diff_attention(q, k, v, lam, bias_dist):
    seq = q.shape[2]
    scale = DF_D ** -0.5
    outs = []
    for s0 in range(0, seq, DF_QBLK):
        e = s0 + DF_QBLK
        qb = q[:, :, s0:e]
        kb = k[:, :, :e]
        vb = v[:, :, :e]
        logits = jnp.einsum('bhqmd,bhkmd->bhmqk', qb, kb, preferred_element_type=F32) * scale
        dist = (s0 + jnp.arange(DF_QBLK, dtype=jnp.int32))[:, None] - jnp.arange(e, dtype=jnp.int32)[None, :]
        bias = bias_dist[jnp.maximum(dist, 0)].transpose(2, 0, 1)
        logits = jnp.where(dist >= 0, logits + bias[None, :, None], -jnp.inf)
        p = jax.nn.softmax(logits, axis=-1)
        attn = p[:, :, 0] - lam * p[:, :, 1]
        outs.append(jnp.einsum('bhqk,bhkd->bhqd', attn, vb.astype(F32)))
    return jnp.concatenate(outs, axis=2)


def hybrid_mixer(x, w_in, w_out, conv_w, dn_a_log, dn_dt_bias, dn_norm_w, df_lambda, df_subln_w,
                 bias_dist, lambda_init):
    bsz, seq, _ = x.shape
    proj = x @ w_in
    qkv_dn, z, a, b, q_df, k_df, v_df = jnp.split(proj, IN_OFFSETS, axis=-1)

    qkv = jax.nn.silu(causal_depthwise_conv(qkv_dn, conv_w))
    q, k, v = jnp.split(qkv, [DN_QK_W, 2 * DN_QK_W], axis=-1)

    def heads(t, d):
        return t.reshape(bsz, seq, -1, d).transpose(0, 2, 1, 3).astype(F32)

    q = l2norm(heads(q, DN_DK)) * DN_DK ** -0.5
    k = l2norm(heads(k, DN_DK))
    v = heads(v, DN_DV)
    beta = jax.nn.sigmoid(b.astype(F32)).transpose(0, 2, 1)
    g = (-jnp.exp(dn_a_log.astype(F32))
         * jax.nn.softplus(a.astype(F32) + dn_dt_bias.astype(F32))).transpose(0, 2, 1)
    o_dn = gated_delta_rule(q, k, v, g, beta).transpose(0, 2, 1, 3)
    o_dn = rms_norm(o_dn, dn_norm_w) * jax.nn.silu(z.reshape(bsz, seq, DN_HEADS, DN_DV).astype(F32))
    o_dn = o_dn.reshape(bsz, seq, DN_V_W).astype(x.dtype)

    qd = q_df.reshape(bsz, seq, DF_HEADS, 2, DF_D).transpose(0, 2, 1, 3, 4)
    kd = k_df.reshape(bsz, seq, DF_HEADS, 2, DF_D).transpose(0, 2, 1, 3, 4)
    vd = v_df.reshape(bsz, seq, DF_HEADS, 2 * DF_D).transpose(0, 2, 1, 3)
    lf = df_lambda.astype(F32)
    lam = jnp.exp(jnp.sum(lf[0] * lf[1])) - jnp.exp(jnp.sum(lf[2] * lf[3])) + lambda_init
    o_df = diff_attention(qd, kd, vd, lam, bias_dist)
    o_df = rms_norm(o_df, df_subln_w) * (1.0 - lambda_init)
    o_df = o_df.transpose(0, 2, 1, 3).reshape(bsz, seq, DF_V_W).astype(x.dtype)

    return jnp.concatenate([o_dn, o_df], axis=-1) @ w_out


def swiglu(x, w_gate, w_up, w_down):
    return (jax.nn.silu(x @ w_gate) * (x @ w_up)) @ w_down


def moe_swiglu(x2, router, w_gate, w_up, w_down):
    n_tok, d = x2.shape
    logits = jnp.dot(x2, router, preferred_element_type=F32)
    top_logit, top_idx = lax.top_k(logits, TOP_K)
    gate = jax.nn.softmax(top_logit, axis=-1).astype(x2.dtype)
    n_asg = n_tok * TOP_K
    flat_e = top_idx.reshape(n_asg)
    order = jnp.argsort(flat_e)
    sorted_e = flat_e[order]
    sorted_tok = (order // TOP_K).astype(jnp.int32)
    sorted_gate = gate.reshape(n_asg)[order]
    counts = jnp.zeros((N_EXPERTS,), jnp.int32).at[flat_e].add(1)
    padded = (counts + MOE_BLK - 1) // MOE_BLK * MOE_BLK
    pad_end = jnp.cumsum(padded)
    pad_start = pad_end - padded
    cnt_start = jnp.cumsum(counts) - counts
    slot = pad_start[sorted_e] + jnp.arange(n_asg, dtype=jnp.int32) - cnt_start[sorted_e]
    n_slot = -(-n_asg // MOE_BLK) * MOE_BLK + N_EXPERTS * MOE_BLK
    n_blk = n_slot // MOE_BLK
    slot_tok = jnp.full((n_slot,), n_tok, jnp.int32).at[slot].set(sorted_tok)
    slot_gate = jnp.zeros((n_slot,), x2.dtype).at[slot].set(sorted_gate)
    blk_e = jnp.minimum(jnp.searchsorted(pad_end, jnp.arange(n_blk, dtype=jnp.int32) * MOE_BLK,
                                         side='right'), N_EXPERTS - 1)
    x_pad = jnp.concatenate([x2, jnp.zeros((1, d), x2.dtype)], axis=0)
    xb = x_pad[slot_tok].reshape(n_blk, MOE_BLK, d)

    def expert_block(args):
        xe, e = args
        return swiglu(xe, w_gate[e], w_up[e], w_down[e])

    yb = lax.map(expert_block, (xb, blk_e)).reshape(n_slot, d)
    out = jnp.zeros((n_tok + 1, d), x2.dtype).at[slot_tok].add(yb * slot_gate[:, None])
    return out[:n_tok]


def setup_inputs(seed: int = 0) -> dict:
    key = jax.random.key(seed)
    ks = jax.random.split(key, 24)

    def nrm(k, shape, s):
        return jax.random.normal(k, shape, F32) * s

    x = nrm(ks[0], (BATCH, SEQ, D_MODEL), 1.0)
    w_in = nrm(ks[1], (DEPTH, D_MODEL, IN_W), D_MODEL ** -0.5)
    w_out = nrm(ks[2], (DEPTH, MIX_W, D_MODEL), MIX_W ** -0.5 * BETA_INIT)
    conv_w = nrm(ks[3], (DEPTH, DN_CONV, DN_CONV_CH), DN_CONV ** -0.5)
    dn_a_log = jnp.log(jax.random.uniform(ks[4], (DEPTH, DN_HEADS), F32, 1.0, 16.0))
    dt = jnp.exp(jax.random.uniform(ks[5], (DEPTH, DN_HEADS), F32, math.log(1e-3), math.log(1e-1)))
    dn_dt_bias = dt + jnp.log(-jnp.expm1(-dt))
    dn_norm_w = 1.0 + nrm(ks[6], (DEPTH, DN_DV), 0.02)
    df_lambda = nrm(ks[7], (DEPTH, 4, DF_D), 0.1)
    df_subln_w = 1.0 + nrm(ks[8], (DEPTH, 2 * DF_D), 0.02)
    rel_bias = nrm(ks[9], (REL_BUCKETS, DF_HEADS), 0.5)
    ln1_g = 1.0 + nrm(ks[10], (DEPTH, D_MODEL), 0.02)
    ln1_b = nrm(ks[11], (DEPTH, D_MODEL), 0.02)
    ln2_g = 1.0 + nrm(ks[12], (DEPTH, D_MODEL), 0.02)
    ln2_b = nrm(ks[13], (DEPTH, D_MODEL), 0.02)
    ffn_w_gate = nrm(ks[14], (N_DENSE, D_MODEL, D_FF), D_MODEL ** -0.5)
    ffn_w_up = nrm(ks[15], (N_DENSE, D_MODEL, D_FF), D_MODEL ** -0.5)
    ffn_w_down = nrm(ks[16], (N_DENSE, D_FF, D_MODEL), D_FF ** -0.5 * BETA_INIT)
    moe_router = nrm(ks[17], (N_MOE, D_MODEL, N_EXPERTS), D_MODEL ** -0.5)
    moe_w_gate = nrm(ks[18], (N_MOE, N_EXPERTS, D_MODEL, D_FF_EXPERT), D_MODEL ** -0.5)
    moe_w_up = nrm(ks[19], (N_MOE, N_EXPERTS, D_MODEL, D_FF_EXPERT), D_MODEL ** -0.5)
    moe_w_down = nrm(ks[20], (N_MOE, N_EXPERTS, D_FF_EXPERT, D_MODEL), D_FF_EXPERT ** -0.5 * BETA_INIT)
    return {'x': x, 'w_in': w_in, 'w_out': w_out, 'conv_w': conv_w, 'dn_a_log': dn_a_log,
            'dn_dt_bias': dn_dt_bias, 'dn_norm_w': dn_norm_w, 'df_lambda': df_lambda,
            'df_subln_w': df_subln_w, 'rel_bias': rel_bias, 'ln1_g': ln1_g, 'ln1_b': ln1_b,
            'ln2_g': ln2_g, 'ln2_b': ln2_b, 'ffn_w_gate': ffn_w_gate, 'ffn_w_up': ffn_w_up,
            'ffn_w_down': ffn_w_down, 'moe_router': moe_router, 'moe_w_gate': moe_w_gate,
            'moe_w_up': moe_w_up, 'moe_w_down': moe_w_down}


def reference(x, w_in, w_out, conv_w, dn_a_log, dn_dt_bias, dn_norm_w, df_lambda, df_subln_w, rel_bias,
              ln1_g, ln1_b, ln2_g, ln2_b, ffn_w_gate, ffn_w_up, ffn_w_down, moe_router, moe_w_gate,
              moe_w_up, moe_w_down):
    bsz, seq, d = x.shape
    dist = jnp.arange(seq, dtype=jnp.int32)
    bias_dist = rel_bias[t5_causal_bucket(dist)].astype(F32)
    for layer in range(DEPTH):
        lambda_init = 0.8 - 0.6 * math.exp(-0.3 * layer)
        mix = hybrid_mixer(x, w_in[layer], w_out[layer], conv_w[layer], dn_a_log[layer], dn_dt_bias[layer],
                           dn_norm_w[layer], df_lambda[layer], df_subln_w[layer], bias_dist, lambda_init)
        x = layer_norm(ALPHA * x + mix, ln1_g[layer], ln1_b[layer])
        if layer % 2 == 0:
            i = layer // 2
            f = swiglu(x, ffn_w_gate[i], ffn_w_up[i], ffn_w_down[i])
        else:
            i = layer // 2
            f = moe_swiglu(x.reshape(bsz * seq, d), moe_router[i], moe_w_gate[i], moe_w_up[i],
                           moe_w_down[i]).reshape(bsz, seq, d)
        x = layer_norm(ALPHA * x + f, ln2_g[layer], ln2_b[layer])
    return x
```

```python
import functools
import math

import jax
import jax.numpy as jnp
import numpy as np
from jax import lax
from jax.experimental import pallas as pl
from jax.experimental.pallas import tpu as pltpu

F32 = jnp.float32
BF16 = jnp.bfloat16

DN_HEADS = 8
DN_D = 64
DN_CONV = 4
DN_CHUNK = 64
DN_W = DN_HEADS * DN_D
DF_HEADS = 4
DF_D = 64
DF_W = DF_HEADS * 2 * DF_D
REL_BUCKETS = 32
REL_MAX_DIST = 128
N_EXPERTS = 8
TOP_K = 2
MOE_BLK = 512
LN_EPS = 1e-5
RMS_EPS = 1e-6

LANES = 128
HALO = 8
PAIR = 2 * DN_D
N_PAIRS = DN_HEADS // 2
VMEM_LIMIT = 56 * 1024 * 1024

TM_PROJ = 512
TS_DN = 256
TQ_DF = 256
TM_FFN = 512
FF_CHUNK = 256


def _cparams(sem):
    return pltpu.CompilerParams(dimension_semantics=sem, vmem_limit_bytes=VMEM_LIMIT)


def _dot(a, b):
    return jnp.dot(a, b, preferred_element_type=F32)


def _dot_nt(a, b):
    return lax.dot_general(a, b, (((1,), (1,)), ((), ())), preferred_element_type=F32)


def _dot_tn(a, b):
    return lax.dot_general(a, b, (((0,), (0,)), ((), ())), preferred_element_type=F32)


def _split(x):
    hi = x.astype(BF16)
    lo = (x - hi.astype(F32)).astype(BF16)
    return hi, lo


def _mm_xc(x, c):
    hi, lo = _split(x)
    return _dot(hi, c) + _dot(lo, c)


def _mm_cx(c, x):
    hi, lo = _split(x)
    return _dot(c, hi) + _dot(c, lo)


def _mm3(a, b):
    ah, al = _split(a)
    bh, bl = _split(b)
    return _dot(ah, bh) + _dot(ah, bl) + _dot(al, bh)


def _layer_norm(v, g, b):
    mu = jnp.mean(v, axis=-1, keepdims=True)
    d = v - mu
    var = jnp.mean(d * d, axis=-1, keepdims=True)
    return d * lax.rsqrt(var + LN_EPS) * g + b


def _inproj_kernel(x_ref, w1_ref, w2_ref, qkv_ref, z_ref, ab_ref, qd_ref, kd_ref, vd_ref):
    xb = x_ref[...].astype(BF16)
    c = 3 * DN_W
    qkv_ref[...] = _dot(xb, w1_ref[:, 0:c])
    z_ref[...] = _dot(xb, w1_ref[:, c:c + DN_W])
    ab_ref[...] = _dot(xb, w1_ref[:, c + DN_W:c + DN_W + LANES])
    qd_ref[...] = (_dot(xb, w2_ref[:, 0:DF_W]) * (DF_D ** -0.5)).astype(BF16)
    kd_ref[...] = _dot(xb, w2_ref[:, DF_W:2 * DF_W]).astype(BF16)
    vd_ref[...] = _dot(xb, w2_ref[:, 2 * DF_W:3 * DF_W]).astype(BF16)


def _inproj(x2, w1, w2):
    t, d = x2.shape
    tm = min(TM_PROJ, t)
    n1, n2 = w1.shape[1], w2.shape[1]
    row = lambda i: (i, 0)
    const = lambda i: (0, 0)
    return pl.pallas_call(
        _inproj_kernel,
        grid=(t // tm,),
        in_specs=[pl.BlockSpec((tm, d), row),
                  pl.BlockSpec((d, n1), const, pipeline_mode=pl.Buffered(1)),
                  pl.BlockSpec((d, n2), const, pipeline_mode=pl.Buffered(1))],
        out_specs=[pl.BlockSpec((tm, 3 * DN_W), row), pl.BlockSpec((tm, DN_W), row),
                   pl.BlockSpec((tm, LANES), row), pl.BlockSpec((tm, DF_W), row),
                   pl.BlockSpec((tm, DF_W), row), pl.BlockSpec((tm, DF_W), row)],
        out_shape=[jax.ShapeDtypeStruct((t, 3 * DN_W), F32), jax.ShapeDtypeStruct((t, DN_W), F32),
                   jax.ShapeDtypeStruct((t, LANES), F32), jax.ShapeDtypeStruct((t, DF_W), BF16),
                   jax.ShapeDtypeStruct((t, DF_W), BF16), jax.ShapeDtypeStruct((t, DF_W), BF16)],
        compiler_params=_cparams(("parallel",)),
        name="inproj",
    )(x2, w1, w2)


def _dn_kernel(qkv_ref, halo_ref, ab_ref, z_ref, convw_ref, par_ref, ea_ref, eb_ref, seg_ref,
               tri_ref, ones_ref, o_ref,
               xe_s, q_s, k_s, kb_s, vb_s, gc_s, eg_s, kdec_s, gl_s, od_s, state_s):
    i = pl.program_id(1)
    ts = qkv_ref.shape[0]
    n_chunks = ts // DN_CHUNK

    @pl.when(i == 0)
    def _():
        state_s[...] = jnp.zeros_like(state_s)

    halo = halo_ref[...]
    xe_s[0:HALO, :] = jnp.where(i > 0, halo, jnp.zeros_like(halo))
    xe_s[HALO:, :] = qkv_ref[...]
    y = convw_ref[0:1, :] * xe_s[pl.ds(HALO - 3, ts), :]
    y += convw_ref[1:2, :] * xe_s[pl.ds(HALO - 2, ts), :]
    y += convw_ref[2:3, :] * xe_s[pl.ds(HALO - 1, ts), :]
    y += convw_ref[3:4, :] * xe_s[pl.ds(HALO, ts), :]
    y = y * jax.nn.sigmoid(y)
    q = y[:, 0:DN_W]
    k = y[:, DN_W:2 * DN_W]
    v = y[:, 2 * DN_W:3 * DN_W]

    seg = seg_ref[...]
    q = q * lax.rsqrt(_mm_xc(q * q, seg) + 1e-6) * (DN_D ** -0.5)
    k = k * lax.rsqrt(_mm_xc(k * k, seg) + 1e-6)

    ab = ab_ref[...]
    a_l = _mm_xc(ab, ea_ref[...])
    b_l = _mm_xc(ab, eb_ref[...])
    beta = jax.nn.sigmoid(b_l)
    xa = a_l + par_ref[1:2, :]
    softplus = jnp.maximum(xa, 0.0) + jnp.log(1.0 + jnp.exp(-jnp.abs(xa)))
    g = -jnp.exp(par_ref[0:1, :]) * softplus
    gc = _mm_cx(tri_ref[...], g)
    gl = _mm_cx(ones_ref[...], g)
    eg = jnp.exp(gc)
    kb = k * beta
    q_s[...] = q
    k_s[...] = k
    kb_s[...] = kb
    vb_s[...] = v * beta
    gc_s[...] = gc
    eg_s[...] = eg
    kdec_s[...] = k * jnp.exp(gl - gc)
    gl_s[...] = jnp.exp(gl)

    lane = lax.broadcasted_iota(jnp.int32, (DN_CHUNK, PAIR), 1)
    rowi = lax.broadcasted_iota(jnp.int32, (DN_CHUNK, PAIR), 0)
    colj = jnp.where(lane >= DN_D, lane - DN_D, lane)
    even = lane < DN_D
    eye2 = rowi == colj
    lower = rowi >= colj
    lane_b = lax.broadcasted_iota(jnp.int32, (PAIR, PAIR), 1)
    row_b = lax.broadcasted_iota(jnp.int32, (PAIR, PAIR), 0)
    bdmask = (lane_b < DN_D) == (row_b < DN_D)

    def bd(xm):
        z0 = jnp.zeros_like(xm)
        return jnp.concatenate([jnp.where(even, xm, z0), jnp.where(even, z0, xm)], axis=0)

    def chunk_body(c, carry):
        r0 = pl.multiple_of(c * DN_CHUNK, DN_CHUNK)
        rows = pl.ds(r0, DN_CHUNK)
        for p in range(N_PAIRS):
            cols = slice(p * PAIR, (p + 1) * PAIR)
            qc = q_s[rows, cols]
            kc = k_s[rows, cols]
            kbc = kb_s[rows, cols]
            vbc = vb_s[rows, cols]
            gcc = gc_s[rows, cols]
            egc = eg_s[rows, cols]
            kdc = kdec_s[rows, cols]
            glc = gl_s[pl.ds(r0, 1), cols]

            k2 = bd(kc).astype(BF16)
            aq = _dot_nt(jnp.concatenate([kbc, qc], axis=0).astype(BF16), k2)
            gcj = jnp.sum(jnp.where(eye2, gcc, 0.0), axis=0, keepdims=True)
            dec = jnp.where(lower, jnp.exp(jnp.minimum(gcc - gcj, 0.0)), 0.0)
            a_qk = aq[DN_CHUNK:, :] * dec
            l_m = jnp.where(eye2, 0.0, aq[:DN_CHUNK, :] * dec)

            p_m = jnp.where(eye2, 1.0, 0.0) - l_m
            y_m = l_m
            for _ in range(5):
                y_m = _mm3(y_m, bd(y_m))
                p_m = p_m + _mm3(p_m, bd(y_m))

            rhs = jnp.concatenate([bd(vbc), bd(kbc * egc)], axis=1)
            uw = _mm3(p_m, rhs)
            u = uw[:, :PAIR]
            w = uw[:, PAIR:]
            qo = _dot(a_qk.astype(BF16), jnp.concatenate([bd(w), bd(u)], axis=1).astype(BF16))
            q_eff = qc * egc - qo[:, :PAIR]
            o_intra = qo[:, PAIR:]
            mn = _dot_tn(kdc.astype(BF16), jnp.concatenate([w, u], axis=1).astype(BF16))
            m_m = jnp.where(bdmask, mn[:, :PAIR], 0.0)
            n_m = jnp.where(bdmask, mn[:, PAIR:], 0.0)

            st = state_s[p]
            r = _dot(jnp.concatenate([q_eff, m_m], axis=0).astype(BF16), st.astype(BF16))
            od_s[rows, cols] = r[:DN_CHUNK, :] + o_intra
            state_s[p] = glc * st - r[DN_CHUNK:, :] + n_m
        return carry

    lax.fori_loop(0, n_chunks, chunk_body, 0)

    od = od_s[...]
    ms = _mm_xc(od * od, seg) * (1.0 / DN_D)
    zz = z_ref[...]
    o_ref[...] = (od * lax.rsqrt(ms + RMS_EPS) * par_ref[2:3, :] * (zz * jax.nn.sigmoid(zz))).astype(BF16)


def _deltanet(qkv, ab, z, convw, par, consts, bsz, seq):
    t = qkv.shape[0]
    ts = min(TS_DN, seq)
    nt = seq // ts
    hb = ts // HALO
    ea, eb, seg, tri, ones = consts
    row = lambda b, i: (b * nt + i, 0)
    const = lambda b, i: (0, 0)
    halo_map = lambda b, i: (jnp.maximum((b * nt + i) * hb - 1, 0), 0)
    cspec = lambda a: pl.BlockSpec(a.shape, const, pipeline_mode=pl.Buffered(1))
    big = lambda: pltpu.VMEM((ts, DN_W), F32)
    return pl.pallas_call(
        _dn_kernel,
        grid=(bsz, nt),
        in_specs=[pl.BlockSpec((ts, 3 * DN_W), row), pl.BlockSpec((HALO, 3 * DN_W), halo_map),
                  pl.BlockSpec((ts, LANES), row), pl.BlockSpec((ts, DN_W), row),
                  cspec(convw), cspec(par), cspec(ea), cspec(eb), cspec(seg), cspec(tri), cspec(ones)],
        out_specs=pl.BlockSpec((ts, DN_W), row),
        out_shape=jax.ShapeDtypeStruct((t, DN_W), BF16),
        scratch_shapes=[pltpu.VMEM((ts + HALO, 3 * DN_W), F32)] + [big() for _ in range(9)]
                       + [pltpu.VMEM((N_PAIRS, PAIR, PAIR), F32)],
        compiler_params=_cparams(("parallel", "arbitrary")),
        name="deltanet",
    )(qkv, qkv, ab, z, convw, par, ea, eb, seg, tri, ones)


def _df_kernel(sc_ref, q_ref, k_ref, v_ref, bt_ref, w_ref, o_ref):
    h = pl.program_id(1)
    qi = pl.program_id(2)
    tq = q_ref.shape[0]
    lane = lax.broadcasted_iota(jnp.int32, (tq, 2 * DF_D), 1)
    q = q_ref[...]
    zq = jnp.zeros_like(q)
    qs = (jnp.where(lane < DF_D, q, zq), jnp.where(lane < DF_D, zq, q))
    far = sc_ref[2 + h]

    def step(ki, carry, bias, mask):
        r0 = pl.multiple_of(ki * tq, tq)
        kk = k_ref[pl.ds(r0, tq), :]
        vv = v_ref[pl.ds(r0, tq), :]
        out = []
        for mp in range(2):
            m_old, l_old, acc = carry[mp]
            s = _dot_nt(qs[mp], kk) + bias
            if mask is not None:
                s = jnp.where(mask, s, -jnp.inf)
            m_new = jnp.maximum(m_old, jnp.max(s, axis=-1, keepdims=True))
            alpha = jnp.exp(m_old - m_new)
            pr = jnp.exp(s - m_new)
            l_new = alpha * l_old + jnp.sum(pr, axis=-1, keepdims=True)
            acc = alpha * acc + _dot(pr.astype(BF16), vv)
            out.append((m_new, l_new, acc))
        return tuple(out)

    init1 = (jnp.full((tq, 1), -jnp.inf, F32), jnp.zeros((tq, 1), F32), jnp.zeros((tq, 2 * DF_D), F32))
    carry = (init1, init1)
    carry = lax.fori_loop(0, jnp.maximum(qi - 1, 0), lambda ki, c: step(ki, c, far, None), carry)
    carry = lax.fori_loop(jnp.maximum(qi - 1, 0), qi, lambda ki, c: step(ki, c, bt_ref[1], None), carry)
    ri = lax.broadcasted_iota(jnp.int32, (tq, tq), 0)
    ci = lax.broadcasted_iota(jnp.int32, (tq, tq), 1)
    carry = step(qi, carry, bt_ref[0], ri >= ci)

    (_, l0, a0), (_, l1, a1) = carry
    o = a0 / l0 - sc_ref[0] * (a1 / l1)
    ms = jnp.mean(o * o, axis=-1, keepdims=True)
    o_ref[...] = (o * lax.rsqrt(ms + RMS_EPS) * w_ref[...] * sc_ref[1]).astype(BF16)


def _diff_attention(qd, kd, vd, btiles, subln_w, scalars, bsz, seq):
    t = qd.shape[0]
    tq = min(TQ_DF, seq)
    nq = seq // tq
    return pl.pallas_call(
        _df_kernel,
        grid_spec=pltpu.PrefetchScalarGridSpec(
            num_scalar_prefetch=1,
            grid=(bsz, DF_HEADS, nq),
            in_specs=[pl.BlockSpec((tq, 2 * DF_D), lambda b, h, i, sc: (b * nq + i, h)),
                      pl.BlockSpec((seq, 2 * DF_D), lambda b, h, i, sc: (b, h)),
                      pl.BlockSpec((seq, 2 * DF_D), lambda b, h, i, sc: (b, h)),
                      pl.BlockSpec((None, 2, tq, tq), lambda b, h, i, sc: (h, 0, 0, 0)),
                      pl.BlockSpec((1, 2 * DF_D), lambda b, h, i, sc: (0, 0))],
            out_specs=pl.BlockSpec((tq, 2 * DF_D), lambda b, h, i, sc: (b * nq + i, h)),
        ),
        out_shape=jax.ShapeDtypeStruct((t, DF_W), BF16),
        compiler_params=_cparams(("parallel", "parallel", "arbitrary")),
        name="diffattn",
    )(scalars, qd, kd, vd, btiles, subln_w)


def _outproj_kernel(alpha, x_ref, a_ref, b_ref, w_ref, g_ref, bb_ref, o_ref):
    mix = _dot(a_ref[...], w_ref[0:DN_W, :]) + _dot(b_ref[...], w_ref[DN_W:, :])
    o_ref[...] = _layer_norm(alpha * x_ref[...] + mix, g_ref[...], bb_ref[...])


def _outproj_ln(x2, o_dn, o_df, w_out, g, b, alpha):
    t, d = x2.shape
    tm = min(TM_PROJ, t)
    row = lambda i: (i, 0)
    const = lambda i: (0, 0)
    return pl.pallas_call(
        functools.partial(_outproj_kernel, alpha),
        grid=(t // tm,),
        in_specs=[pl.BlockSpec((tm, d), row), pl.BlockSpec((tm, DN_W), row), pl.BlockSpec((tm, DF_W), row),
                  pl.BlockSpec(w_out.shape, const, pipeline_mode=pl.Buffered(1)),
                  pl.BlockSpec((1, d), const), pl.BlockSpec((1, d), const)],
        out_specs=pl.BlockSpec((tm, d), row),
        out_shape=jax.ShapeDtypeStruct((t, d), F32),
        compiler_params=_cparams(("parallel",)),
        name="outproj_ln",
    )(x2, o_dn, o_df, w_out, g, b)


def _swiglu_acc(xb, wg_ref, wu_ref, wd_ref, acc_ref):
    d_ff = wg_ref.shape[-1]
    for c0 in range(0, d_ff, FF_CHUNK):
        cs = slice(c0, c0 + FF_CHUNK)
        hg = _dot(xb, wg_ref[:, cs])
        hu = _dot(xb, wu_ref[:, cs])
        hh = (hg * jax.nn.sigmoid(hg) * hu).astype(BF16)
        contrib = _dot(hh, wd_ref[cs, :])
        if c0 == 0:
            acc_ref[...] = contrib
        else:
            acc_ref[...] += contrib


def _ffn_kernel(alpha, x_ref, wg_ref, wu_ref, wd_ref, g_ref, b_ref, o_ref, acc_ref):
    x = x_ref[...]
    _swiglu_acc(x.astype(BF16), wg_ref, wu_ref, wd_ref, acc_ref)
    o_ref[...] = _layer_norm(alpha * x + acc_ref[...], g_ref[...], b_ref[...])


def _ffn_ln(x2, wg, wu, wd, g, b, alpha):
    t, d = x2.shape
    tm = min(TM_FFN, t)
    row = lambda i: (i, 0)
    const = lambda i: (0, 0)
    wspec = lambda a: pl.BlockSpec(a.shape, const, pipeline_mode=pl.Buffered(1))
    return pl.pallas_call(
        functools.partial(_ffn_kernel, alpha),
        grid=(t // tm,),
        in_specs=[pl.BlockSpec((tm, d), row), wspec(wg), wspec(wu), wspec(wd),
                  pl.BlockSpec((1, d), const), pl.BlockSpec((1, d), const)],
        out_specs=pl.BlockSpec((tm, d), row),
        out_shape=jax.ShapeDtypeStruct((t, d), F32),
        scratch_shapes=[pltpu.VMEM((tm, d), F32)],
        compiler_params=_cparams(("parallel",)),
        name="ffn_ln",
    )(x2, wg, wu, wd, g, b)


def _router_kernel(x_ref, wr_ref, o_ref, xb_ref):
    x = x_ref[...]
    xb_ref[...] = x.astype(BF16)
    logits = jnp.dot(x, wr_ref[...], preferred_element_type=F32, precision=lax.Precision.HIGHEST)
    lane = lax.broadcasted_iota(jnp.int32, logits.shape, 1)
    lg = jnp.where(lane < N_EXPERTS, logits, -jnp.inf)
    m1 = jnp.max(lg, axis=-1, keepdims=True)
    i1 = jnp.min(jnp.where(lg == m1, lane, LANES), axis=-1, keepdims=True)
    lg2 = jnp.where(lane == i1, -jnp.inf, lg)
    m2 = jnp.max(lg2, axis=-1, keepdims=True)
    i2 = jnp.min(jnp.where(lg2 == m2, lane, LANES), axis=-1, keepdims=True)
    e = jnp.exp(m2 - m1)
    g1 = 1.0 / (1.0 + e)
    g2 = e / (1.0 + e)
    out = jnp.where(lane == 0, i1.astype(F32), 0.0)
    out = jnp.where(lane == 1, i2.astype(F32), out)
    out = jnp.where(lane == 2, g1, out)
    out = jnp.where(lane == 3, g2, out)
    o_ref[...] = out


def _router(x2, wr):
    t, d = x2.shape
    tm = min(TM_PROJ, t)
    row = lambda i: (i, 0)
    return pl.pallas_call(
        _router_kernel,
        grid=(t // tm,),
        in_specs=[pl.BlockSpec((tm, d), row), pl.BlockSpec((d, LANES), lambda i: (0, 0))],
        out_specs=[pl.BlockSpec((tm, LANES), row), pl.BlockSpec((tm, d), row)],
        out_shape=[jax.ShapeDtypeStruct((t, LANES), F32), jax.ShapeDtypeStruct((t, d), BF16)],
        compiler_params=_cparams(("parallel",)),
        name="router",
    )(x2, wr)


def _expert_kernel(be_ref, nu_ref, x_ref, wg_ref, wu_ref, wd_ref, o_ref, acc_ref):
    i = pl.program_id(0)

    @pl.when(i < nu_ref[0])
    def _():
        _swiglu_acc(x_ref[...], wg_ref, wu_ref, wd_ref, acc_ref)
        o_ref[...] = acc_ref[...].astype(BF16)

    @pl.when(i >= nu_ref[0])
    def _():
        o_ref[...] = jnp.zeros_like(o_ref)


def _experts(xb, blk_e, n_used, wg, wu, wd):
    n_slot, d = xb.shape
    n_blk = n_slot // MOE_BLK
    dff = wg.shape[-1]
    row = lambda i, be, nu: (i, 0)
    wmap = lambda i, be, nu: (be[i], 0, 0)
    return pl.pallas_call(
        _expert_kernel,
        grid_spec=pltpu.PrefetchScalarGridSpec(
            num_scalar_prefetch=2,
            grid=(n_blk,),
            in_specs=[pl.BlockSpec((MOE_BLK, d), row),
                      pl.BlockSpec((None, d, dff), wmap, pipeline_mode=pl.Buffered(1)),
                      pl.BlockSpec((None, d, dff), wmap, pipeline_mode=pl.Buffered(1)),
                      pl.BlockSpec((None, dff, d), wmap, pipeline_mode=pl.Buffered(1))],
            out_specs=pl.BlockSpec((MOE_BLK, d), row),
            scratch_shapes=[pltpu.VMEM((MOE_BLK, d), F32)],
        ),
        out_shape=jax.ShapeDtypeStruct((n_slot, d), BF16),
        compiler_params=_cparams(("arbitrary",)),
        name="experts",
    )(blk_e, n_used, xb, wg, wu, wd)


def _combine_kernel(alpha, x_ref, y_ref, r_ref, g_ref, b_ref, o_ref):
    d = x_ref.shape[1]
    r = r_ref[...]
    f = r[:, 2:3] * y_ref[:, 0:d].astype(F32) + r[:, 3:4] * y_ref[:, d:2 * d].astype(F32)
    o_ref[...] = _layer_norm(alpha * x_ref[...] + f, g_ref[...], b_ref[...])


def _combine_ln(x2, y2, route, g, b, alpha):
    t, d = x2.shape
    tm = min(TM_PROJ, t)
    row = lambda i: (i, 0)
    const = lambda i: (0, 0)
    return pl.pallas_call(
        functools.partial(_combine_kernel, alpha),
        grid=(t // tm,),
        in_specs=[pl.BlockSpec((tm, d), row), pl.BlockSpec((tm, 2 * d), row), pl.BlockSpec((tm, LANES), row),
                  pl.BlockSpec((1, d), const), pl.BlockSpec((1, d), const)],
        out_specs=pl.BlockSpec((tm, d), row),
        out_shape=jax.ShapeDtypeStruct((t, d), F32),
        compiler_params=_cparams(("parallel",)),
        name="combine_ln",
    )(x2, y2, route, g, b)


def _moe(x2, wr, wg, wu, wd, g, b, alpha):
    t, d = x2.shape
    route, xb16 = _router(x2, wr)
    top_idx = route[:, 0:2].astype(jnp.int32)
    n_asg = t * TOP_K
    flat_e = top_idx.reshape(n_asg)
    onehot = (flat_e[:, None] == jnp.arange(N_EXPERTS, dtype=jnp.int32)[None, :]).astype(jnp.int32)
    csum = jnp.cumsum(onehot, axis=0)
    rank = jnp.sum((csum - onehot) * onehot, axis=1)
    counts = csum[-1]
    padded = (counts + MOE_BLK - 1) // MOE_BLK * MOE_BLK
    pad_end = jnp.cumsum(padded)
    pad_start = pad_end - padded
    slot = pad_start[flat_e] + rank
    n_slot = -(-n_asg // MOE_BLK) * MOE_BLK + N_EXPERTS * MOE_BLK
    n_blk = n_slot // MOE_BLK
    blk_e = jnp.minimum(jnp.searchsorted(pad_end, jnp.arange(n_blk, dtype=jnp.int32) * MOE_BLK, side='right'),
                        N_EXPERTS - 1).astype(jnp.int32)
    n_used = (pad_end[-1:] // MOE_BLK).astype(jnp.int32)
    tok = jnp.arange(n_asg, dtype=jnp.int32) // TOP_K
    slot_tok = jnp.full((n_slot,), t, jnp.int32).at[slot].set(tok)
    x_pad = jnp.concatenate([xb16, jnp.zeros((1, d), BF16)], axis=0)
    xb = x_pad[slot_tok]
    yb = _experts(xb, blk_e, n_used, wg, wu, wd)
    y2 = yb[slot].reshape(t, TOP_K * d)
    return _combine_ln(x2, y2, route, g, b, alpha)


def _t5_causal_bucket(dist):
    max_exact = REL_BUCKETS // 2
    d = jnp.maximum(dist, 1).astype(F32)
    large = max_exact + (jnp.log(d / max_exact) / math.log(REL_MAX_DIST / max_exact)
                         * (REL_BUCKETS - max_exact)).astype(jnp.int32)
    large = jnp.minimum(large, REL_BUCKETS - 1)
    return jnp.where(dist < max_exact, dist, large)


def _dn_constants(ts):
    lanes = np.arange(DN_W)
    ea = np.zeros((LANES, DN_W), np.float32)
    eb = np.zeros((LANES, DN_W), np.float32)
    ea[lanes // DN_D, lanes] = 1.0
    eb[DN_HEADS + lanes // DN_D, lanes] = 1.0
    seg = (lanes[:, None] // DN_D == lanes[None, :] // DN_D).astype(np.float32)
    r = np.arange(ts)
    same = r[:, None] // DN_CHUNK == r[None, :] // DN_CHUNK
    tri = (same & (r[:, None] >= r[None, :])).astype(np.float32)
    ones = same.astype(np.float32)
    return tuple(jnp.asarray(a, BF16) for a in (ea, eb, seg, tri, ones))


def kernel(x, w_in, w_out, conv_w, dn_a_log, dn_dt_bias, dn_norm_w, df_lambda, df_subln_w, rel_bias,
           ln1_g, ln1_b, ln2_g, ln2_b, ffn_w_gate, ffn_w_up, ffn_w_down, moe_router, moe_w_gate,
           moe_w_up, moe_w_down):
    bsz, seq, d = x.shape
    depth = w_in.shape[0]
    alpha = (2 * depth) ** 0.25
    t = bsz * seq
    tq = min(TQ_DF, seq)
    assert tq >= REL_MAX_DIST and seq % tq == 0 and seq % min(TS_DN, seq) == 0

    bias_dist = rel_bias[_t5_causal_bucket(jnp.arange(seq, dtype=jnp.int32))].astype(F32)
    ii = jnp.arange(tq, dtype=jnp.int32)
    rel = ii[:, None] - ii[None, :]
    idx = jnp.stack([jnp.maximum(rel, 0), jnp.minimum(rel + tq, seq - 1)])
    btiles = jnp.transpose(bias_dist[idx], (3, 0, 1, 2))
    far_bias = bias_dist[seq - 1]

    dn_consts = _dn_constants(min(TS_DN, seq))
    c_dn = 3 * DN_W
    x2 = x.reshape(t, d)
    for layer in range(depth):
        lambda_init = 0.8 - 0.6 * math.exp(-0.3 * layer)
        wl = w_in[layer]
        n1 = c_dn + DN_W + 2 * DN_HEADS
        w1 = jnp.concatenate([wl[:, :n1], jnp.zeros((d, LANES - 2 * DN_HEADS), F32)], axis=1).astype(BF16)
        w2 = wl[:, n1:].astype(BF16)
        qkv, z, ab, qd, kd, vd = _inproj(x2, w1, w2)

        convw = jnp.concatenate([conv_w[layer], jnp.zeros((HALO - DN_CONV, c_dn), F32)], axis=0)
        par = jnp.zeros((HALO, DN_W), F32)
        par = par.at[0].set(jnp.repeat(dn_a_log[layer], DN_D))
        par = par.at[1].set(jnp.repeat(dn_dt_bias[layer], DN_D))
        par = par.at[2].set(jnp.tile(dn_norm_w[layer], DN_HEADS))
        o_dn = _deltanet(qkv, ab, z, convw, par, dn_consts, bsz, seq)

        lf = df_lambda[layer].astype(F32)
        lam = jnp.exp(jnp.sum(lf[0] * lf[1])) - jnp.exp(jnp.sum(lf[2] * lf[3])) + lambda_init
        scalars = jnp.concatenate([jnp.stack([lam, jnp.asarray(1.0 - lambda_init, F32)]), far_bias]).astype(F32)
        o_df = _diff_attention(qd, kd, vd, btiles, df_subln_w[layer].reshape(1, 2 * DF_D), scalars, bsz, seq)

        x2 = _outproj_ln(x2, o_dn, o_df, w_out[layer].astype(BF16), ln1_g[layer].reshape(1, d),
                         ln1_b[layer].reshape(1, d), alpha)
        i = layer // 2
        g2 = ln2_g[layer].reshape(1, d)
        b2 = ln2_b[layer].reshape(1, d)
        if layer % 2 == 0:
            x2 = _ffn_ln(x2, ffn_w_gate[i].astype(BF16), ffn_w_up[i].astype(BF16), ffn_w_down[i].astype(BF16),
                         g2, b2, alpha)
        else:
            wr = jnp.concatenate([moe_router[i], jnp.zeros((d, LANES - N_EXPERTS), F32)], axis=1)
            x2 = _moe(x2, wr, moe_w_gate[i].astype(BF16), moe_w_up[i].astype(BF16), moe_w_down[i].astype(BF16),
                      g2, b2, alpha)
    return x2.reshape(bsz, seq, d)
```

```python
import functools
import math

import jax
import jax.numpy as jnp
import numpy as np
from jax import lax
from jax.experimental import pallas as pl
from jax.experimental.pallas import tpu as pltpu

F32 = jnp.float32
BF16 = jnp.bfloat16

DN_HEADS = 8
DN_D = 64
DN_CONV = 4
DN_CHUNK = 64
DN_W = DN_HEADS * DN_D
DF_HEADS = 4
DF_D = 64
DF_W = DF_HEADS * 2 * DF_D
REL_BUCKETS = 32
REL_MAX_DIST = 128
N_EXPERTS = 8
TOP_K = 2
MOE_BLK = 512
LN_EPS = 1e-5
RMS_EPS = 1e-6
LOG2E = math.log2(math.e)

LANES = 128
HALO = 8
PAIR = 2 * DN_D
N_PAIRS = DN_HEADS // 2
VMEM_LIMIT = 56 * 1024 * 1024

TM_PROJ = 512
TS_DN = 256
DN_UNROLL = 2
TQ_DF = 512
TM_FFN = 512
FF_CHUNK = 256


def _cparams(sem):
    return pltpu.CompilerParams(dimension_semantics=sem, vmem_limit_bytes=VMEM_LIMIT)


def _dot(a, b):
    return jnp.dot(a, b, preferred_element_type=F32)


def _dot_nt(a, b):
    return lax.dot_general(a, b, (((1,), (1,)), ((), ())), preferred_element_type=F32)


def _dot_tn(a, b):
    return lax.dot_general(a, b, (((0,), (0,)), ((), ())), preferred_element_type=F32)


def _split(x):
    hi = x.astype(BF16)
    lo = (x - hi.astype(F32)).astype(BF16)
    return hi, lo


def _mm_xc(x, c):
    hi, lo = _split(x)
    return _dot(hi, c) + _dot(lo, c)


def _mm_cx(c, x):
    hi, lo = _split(x)
    return _dot(c, hi) + _dot(c, lo)


def _mm3(a, b):
    ah, al = _split(a)
    bh, bl = _split(b)
    return _dot(ah, bh) + _dot(ah, bl) + _dot(al, bh)


def _layer_norm(v, g, b):
    mu = jnp.mean(v, axis=-1, keepdims=True)
    d = v - mu
    var = jnp.mean(d * d, axis=-1, keepdims=True)
    return d * lax.rsqrt(var + LN_EPS) * g + b


def _inproj_kernel(x_ref, w1_ref, w2_ref, wvt_ref, qkv_ref, z_ref, ab_ref, qd_ref, kd_ref, vd_ref):
    xb = x_ref[...].astype(BF16)
    c = 3 * DN_W
    qkv_ref[...] = _dot(xb, w1_ref[:, 0:c])
    z_ref[...] = _dot(xb, w1_ref[:, c:c + DN_W])
    ab_ref[...] = _dot(xb, w1_ref[:, c + DN_W:c + DN_W + LANES])
    qd_ref[...] = (_dot(xb, w2_ref[:, 0:DF_W]) * (DF_D ** -0.5 * LOG2E)).astype(BF16)
    kd_ref[...] = _dot(xb, w2_ref[:, DF_W:2 * DF_W]).astype(BF16)
    vd_ref[...] = _dot_nt(wvt_ref[...], xb).astype(BF16)


def _inproj(x2, w1, w2, wvt, bsz, seq):
    t, d = x2.shape
    tm = min(TM_PROJ, seq)
    nt = seq // tm
    n1, n2 = w1.shape[1], w2.shape[1]
    row = lambda i: (i, 0)
    const = lambda i: (0, 0)
    return pl.pallas_call(
        _inproj_kernel,
        grid=(t // tm,),
        in_specs=[pl.BlockSpec((tm, d), row),
                  pl.BlockSpec((d, n1), const, pipeline_mode=pl.Buffered(1)),
                  pl.BlockSpec((d, n2), const, pipeline_mode=pl.Buffered(1)),
                  pl.BlockSpec((DF_W, d), const, pipeline_mode=pl.Buffered(1))],
        out_specs=[pl.BlockSpec((tm, 3 * DN_W), row), pl.BlockSpec((tm, DN_W), row),
                   pl.BlockSpec((tm, LANES), row), pl.BlockSpec((tm, DF_W), row),
                   pl.BlockSpec((tm, DF_W), row),
                   pl.BlockSpec((None, DF_W, tm), lambda i: (i // nt, 0, i % nt))],
        out_shape=[jax.ShapeDtypeStruct((t, 3 * DN_W), F32), jax.ShapeDtypeStruct((t, DN_W), F32),
                   jax.ShapeDtypeStruct((t, LANES), F32), jax.ShapeDtypeStruct((t, DF_W), BF16),
                   jax.ShapeDtypeStruct((t, DF_W), BF16), jax.ShapeDtypeStruct((bsz, DF_W, seq), BF16)],
        compiler_params=_cparams(("parallel",)),
        name="inproj",
    )(x2, w1, w2, wvt)


def _dn_kernel(qkv_ref, halo_ref, ab_ref, z_ref, convw_ref, par_ref, ea_ref, eb_ref, seg_ref,
               tri_ref, ones_ref, o_ref,
               xe_s, q_s, k_s, kb_s, vb_s, gc_s, eg_s, kdec_s, gl_s, od_s, qe_s, m_s, n_s, state_s):
    i = pl.program_id(1)
    ts = qkv_ref.shape[0]
    n_chunks = ts // DN_CHUNK

    @pl.when(i == 0)
    def _():
        state_s[...] = jnp.zeros_like(state_s)

    halo = halo_ref[...]
    xe_s[0:HALO, :] = jnp.where(i > 0, halo, jnp.zeros_like(halo))
    xe_s[HALO:, :] = qkv_ref[...]
    xe = xe_s[...]
    y = convw_ref[3:4, :] * xe[HALO:, :]
    for tap in range(DN_CONV - 1):
        y += convw_ref[tap:tap + 1, :] * pltpu.roll(xe, DN_CONV - 1 - tap, axis=0)[HALO:, :]
    y = y * jax.nn.sigmoid(y)
    q = y[:, 0:DN_W]
    k = y[:, DN_W:2 * DN_W]
    v = y[:, 2 * DN_W:3 * DN_W]

    seg = seg_ref[...]
    q = q * lax.rsqrt(_dot((q * q).astype(BF16), seg) + 1e-6) * (DN_D ** -0.5)
    k = k * lax.rsqrt(_dot((k * k).astype(BF16), seg) + 1e-6)

    ab = ab_ref[...]
    a_l = _mm_xc(ab, ea_ref[...])
    b_l = _mm_xc(ab, eb_ref[...])
    beta = jax.nn.sigmoid(b_l)
    xa = a_l + par_ref[1:2, :]
    softplus = jnp.maximum(xa, 0.0) + jnp.log(1.0 + jnp.exp(-jnp.abs(xa)))
    g = -jnp.exp(par_ref[0:1, :]) * softplus
    gc = _mm_cx(tri_ref[...], g)
    gl = _mm_cx(ones_ref[...], g)
    eg = jnp.exp(gc)
    kb = k * beta
    q_s[...] = q
    k_s[...] = k
    kb_s[...] = kb
    vb_s[...] = v * beta
    gc_s[...] = gc
    eg_s[...] = eg
    kdec_s[...] = k * jnp.exp(gl - gc)
    gl_s[...] = jnp.exp(gl)

    lane = lax.broadcasted_iota(jnp.int32, (DN_CHUNK, PAIR), 1)
    rowi = lax.broadcasted_iota(jnp.int32, (DN_CHUNK, PAIR), 0)
    colj = jnp.where(lane >= DN_D, lane - DN_D, lane)
    even = lane < DN_D
    eye2 = rowi == colj
    lower = rowi >= colj
    lane_b = lax.broadcasted_iota(jnp.int32, (PAIR, PAIR), 1)
    row_b = lax.broadcasted_iota(jnp.int32, (PAIR, PAIR), 0)
    bdmask = (lane_b < DN_D) == (row_b < DN_D)

    def bd(xm):
        z0 = jnp.zeros_like(xm)
        return jnp.concatenate([jnp.where(even, xm, z0), jnp.where(even, z0, xm)], axis=0)

    eye_f = jnp.where(eye2, 1.0, 0.0)

    def local_body(cc, carry):
        chains = [(cc * DN_UNROLL + dc, p) for dc in range(DN_UNROLL) for p in range(N_PAIRS)]
        n = len(chains)
        rows = [pl.ds(pl.multiple_of(c * DN_CHUNK, DN_CHUNK), DN_CHUNK) for c, _ in chains]
        cols = [slice(p * PAIR, (p + 1) * PAIR) for _, p in chains]
        ld = lambda ref, i: ref[rows[i], cols[i]]

        aq = [_dot_nt(jnp.concatenate([ld(kb_s, i), ld(q_s, i)], axis=0).astype(BF16),
                      bd(ld(k_s, i).astype(BF16))) for i in range(n)]
        a_qk, l_m = [], []
        for i in range(n):
            gcc = ld(gc_s, i)
            gcj = jnp.sum(jnp.where(eye2, gcc, 0.0), axis=0, keepdims=True)
            dec = jnp.where(lower, jnp.exp(jnp.minimum(gcc - gcj, 0.0)), 0.0)
            a_qk.append((aq[i][DN_CHUNK:, :] * dec).astype(BF16))
            l_m.append(jnp.where(eye2, 0.0, aq[i][:DN_CHUNK, :] * dec))

        lhl = [_split(m) for m in l_m]
        x0 = [eye_f - m for m in l_m]
        yb = [h for h, _ in lhl]
        for _ in range(5):
            yb = [_dot(yb[i], bd(yb[i])).astype(BF16) for i in range(n)]
            x0 = [x0[i] + _dot(x0[i].astype(BF16), bd(yb[i])) for i in range(n)]
        xhl = [_split(m) for m in x0]
        res = []
        for i in range(n):
            bxh = bd(xhl[i][0])
            lx = _dot(lhl[i][0], bxh) + _dot(lhl[i][0], bd(xhl[i][1])) + _dot(lhl[i][1], bxh)
            res.append((eye_f - x0[i] - lx).astype(BF16))
        t_m = [x0[i] + _dot(xhl[i][0], bd(res[i])) for i in range(n)]

        uw = []
        for i in range(n):
            th, tl = _split(t_m[i])
            kbg = (ld(kb_s, i) * ld(eg_s, i)).astype(BF16)
            rhs = jnp.concatenate([bd(ld(vb_s, i).astype(BF16)), bd(kbg)], axis=1)
            uw.append((_dot(th, rhs) + _dot(tl, rhs)).astype(BF16))
        qo = [_dot(a_qk[i], jnp.concatenate([bd(uw[i][:, PAIR:]), bd(uw[i][:, :PAIR])], axis=1))
              for i in range(n)]
        mn = [_dot_tn(ld(kdec_s, i).astype(BF16), jnp.concatenate([uw[i][:, PAIR:], uw[i][:, :PAIR]], axis=1))
              for i in range(n)]
        for i, (c, p) in enumerate(chains):
            qe_s[rows[i], cols[i]] = (ld(q_s, i) * ld(eg_s, i) - qo[i][:, :PAIR]).astype(BF16)
            od_s[rows[i], cols[i]] = qo[i][:, PAIR:]
            m_s[c, p] = jnp.where(bdmask, mn[i][:, :PAIR], 0.0).astype(BF16)
            n_s[c, p] = jnp.where(bdmask, mn[i][:, PAIR:], 0.0)
        return carry

    lax.fori_loop(0, n_chunks // DN_UNROLL, local_body, 0)

    for c in range(n_chunks):
        rows = slice(c * DN_CHUNK, (c + 1) * DN_CHUNK)
        for p in range(N_PAIRS):
            cols = slice(p * PAIR, (p + 1) * PAIR)
            st = state_s[p]
            r = _dot(jnp.concatenate([qe_s[rows, cols], m_s[c, p]], axis=0), st.astype(BF16))
            od_s[rows, cols] += r[:DN_CHUNK, :]
            state_s[p] = gl_s[c * DN_CHUNK:c * DN_CHUNK + 1, cols] * st - r[DN_CHUNK:, :] + n_s[c, p]

    od = od_s[...]
    ms = _dot((od * od).astype(BF16), seg) * (1.0 / DN_D)
    zz = z_ref[...]
    o_ref[...] = (od * lax.rsqrt(ms + RMS_EPS) * par_ref[2:3, :] * (zz * jax.nn.sigmoid(zz))).astype(BF16)


def _deltanet(qkv, ab, z, convw, par, consts, bsz, seq):
    t = qkv.shape[0]
    ts = min(TS_DN, seq)
    nt = seq // ts
    hb = ts // HALO
    ea, eb, seg, tri, ones = consts
    row = lambda b, i: (b * nt + i, 0)
    const = lambda b, i: (0, 0)
    halo_map = lambda b, i: (jnp.maximum((b * nt + i) * hb - 1, 0), 0)
    cspec = lambda a: pl.BlockSpec(a.shape, const, pipeline_mode=pl.Buffered(1))
    big = lambda: pltpu.VMEM((ts, DN_W), F32)
    return pl.pallas_call(
        _dn_kernel,
        grid=(bsz, nt),
        in_specs=[pl.BlockSpec((ts, 3 * DN_W), row), pl.BlockSpec((HALO, 3 * DN_W), halo_map),
                  pl.BlockSpec((ts, LANES), row), pl.BlockSpec((ts, DN_W), row),
                  cspec(convw), cspec(par), cspec(ea), cspec(eb), cspec(seg), cspec(tri), cspec(ones)],
        out_specs=pl.BlockSpec((ts, DN_W), row),
        out_shape=jax.ShapeDtypeStruct((t, DN_W), BF16),
        scratch_shapes=[pltpu.VMEM((ts + HALO, 3 * DN_W), F32)] + [big() for _ in range(9)]
                       + [pltpu.VMEM((ts, DN_W), BF16),
                          pltpu.VMEM((ts // DN_CHUNK, N_PAIRS, PAIR, PAIR), BF16),
                          pltpu.VMEM((ts // DN_CHUNK, N_PAIRS, PAIR, PAIR), F32),
                          pltpu.VMEM((N_PAIRS, PAIR, PAIR), F32)],
        compiler_params=_cparams(("parallel", "arbitrary")),
        name="deltanet",
    )(qkv, qkv, ab, z, convw, par, ea, eb, seg, tri, ones)


def _df_kernel(sc_ref, q_ref, k_ref, vt_ref, bt_ref, w_ref, o_ref, sa_ref, sb_ref, acc_ref):
    qi = pl.program_id(2)
    tq = q_ref.shape[0]
    lane = lax.broadcasted_iota(jnp.int32, (tq, 2 * DF_D), 1)
    q = q_ref[...]
    zq = jnp.zeros_like(q)
    qs = (jnp.where(lane < DF_D, q, zq), jnp.where(lane < DF_D, zq, q))
    acc_ref[...] = jnp.zeros_like(acc_ref)

    def scores(j, s_ref):
        kk = k_ref[pl.ds(pl.multiple_of(j * tq, tq), tq), :]
        for mp in range(2):
            s_ref[mp] = _dot_nt(kk, qs[mp])

    def absorb(j, s_ref, bias, carry):
        vt = vt_ref[:, pl.ds(pl.multiple_of(j * tq, tq), tq)]
        out = []
        for mp in range(2):
            m_old, l_old = carry[mp]
            st = s_ref[mp]
            if bias is not None:
                st = st + bias
            m_new = jnp.maximum(m_old, jnp.max(st, axis=0, keepdims=True))
            alpha = jnp.exp2(m_old - m_new)
            pr = jnp.exp2(st - m_new)
            l_new = alpha * l_old + jnp.sum(pr, axis=0, keepdims=True)
            acc_ref[mp] = alpha * acc_ref[mp] + _dot(vt, pr.astype(BF16))
            out.append((m_new, l_new))
        return tuple(out)

    init1 = (jnp.full((1, tq), -jnp.inf, F32), jnp.zeros((1, tq), F32))
    carry = (init1, init1)
    n_far = jnp.maximum(qi - 1, 0)
    odd = n_far % 2

    @pl.when(qi == 0)
    def _():
        scores(0, sb_ref)

    @pl.when(jnp.logical_and(qi > 0, odd == 0))
    def _():
        scores(0, sa_ref)

    def odd_step(_, c):
        scores(0, sb_ref)
        scores(1, sa_ref)
        return absorb(0, sb_ref, None, c)

    carry = lax.fori_loop(0, odd, odd_step, carry)

    def pair_step(jp, c):
        j = odd + 2 * jp
        scores(j + 1, sb_ref)
        c = absorb(j, sa_ref, None, c)
        scores(j + 2, sa_ref)
        return absorb(j + 1, sb_ref, None, c)

    carry = lax.fori_loop(0, n_far // 2, pair_step, carry)

    def prev_step(j, c):
        scores(j + 1, sb_ref)
        return absorb(j, sa_ref, bt_ref[1], c)

    carry = lax.fori_loop(n_far, qi, prev_step, carry)
    carry = absorb(qi, sb_ref, bt_ref[0], carry)

    (_, l0), (_, l1) = carry
    ot = acc_ref[0] / l0 - sc_ref[0] * (acc_ref[1] / l1)
    ms = jnp.mean(ot * ot, axis=0, keepdims=True)
    ot = ot * lax.rsqrt(ms + RMS_EPS) * w_ref[...] * sc_ref[1]
    o_ref[...] = ot.T.astype(BF16)


def _attn_bias_tiles(rel_bias, seq, tq):
    bias_dist = rel_bias[_t5_causal_bucket(jnp.arange(seq, dtype=jnp.int32))].astype(F32)
    far = bias_dist[seq - 1]
    kj = jnp.arange(tq, dtype=jnp.int32)[:, None]
    qi = jnp.arange(tq, dtype=jnp.int32)[None, :]
    dist = jnp.stack([qi - kj, qi + tq - kj])
    vals = (bias_dist[jnp.clip(dist, 0, seq - 1)] - far) * LOG2E
    vals = jnp.where((dist >= 0)[..., None], vals, -jnp.inf)
    return jnp.transpose(vals, (3, 0, 1, 2))


def _diff_attention(qd, kd, vdt, btiles, subln_w, scalars, bsz, seq):
    t = qd.shape[0]
    tq = min(TQ_DF, seq)
    nq = seq // tq
    sbuf = pltpu.VMEM((2, tq, tq), F32)
    return pl.pallas_call(
        _df_kernel,
        grid_spec=pltpu.PrefetchScalarGridSpec(
            num_scalar_prefetch=1,
            grid=(bsz, DF_HEADS, nq),
            in_specs=[pl.BlockSpec((tq, 2 * DF_D), lambda b, h, i, sc: (b * nq + i, h)),
                      pl.BlockSpec((seq, 2 * DF_D), lambda b, h, i, sc: (b, h)),
                      pl.BlockSpec((None, 2 * DF_D, seq), lambda b, h, i, sc: (b, h, 0)),
                      pl.BlockSpec((None, 2, tq, tq), lambda b, h, i, sc: (h, 0, 0, 0)),
                      pl.BlockSpec((2 * DF_D, 1), lambda b, h, i, sc: (0, 0))],
            out_specs=pl.BlockSpec((tq, 2 * DF_D), lambda b, h, i, sc: (b * nq + i, h)),
            scratch_shapes=[sbuf, sbuf, pltpu.VMEM((2, 2 * DF_D, tq), F32)],
        ),
        out_shape=jax.ShapeDtypeStruct((t, DF_W), BF16),
        compiler_params=_cparams(("parallel", "parallel", "arbitrary")),
        name="diffattn",
    )(scalars, qd, kd, vdt, btiles, subln_w)


def _outproj_kernel(alpha, x_ref, a_ref, b_ref, w_ref, g_ref, bb_ref, o_ref):
    mix = _dot(a_ref[...], w_ref[0:DN_W, :]) + _dot(b_ref[...], w_ref[DN_W:, :])
    o_ref[...] = _layer_norm(alpha * x_ref[...] + mix, g_ref[...], bb_ref[...])


def _outproj_ln(x2, o_dn, o_df, w_out, g, b, alpha):
    t, d = x2.shape
    tm = min(TM_PROJ, t)
    row = lambda i: (i, 0)
    const = lambda i: (0, 0)
    return pl.pallas_call(
        functools.partial(_outproj_kernel, alpha),
        grid=(t // tm,),
        in_specs=[pl.BlockSpec((tm, d), row), pl.BlockSpec((tm, DN_W), row), pl.BlockSpec((tm, DF_W), row),
                  pl.BlockSpec(w_out.shape, const, pipeline_mode=pl.Buffered(1)),
                  pl.BlockSpec((1, d), const), pl.BlockSpec((1, d), const)],
        out_specs=pl.BlockSpec((tm, d), row),
        out_shape=jax.ShapeDtypeStruct((t, d), F32),
        compiler_params=_cparams(("parallel",)),
        name="outproj_ln",
    )(x2, o_dn, o_df, w_out, g, b)


def _swiglu_acc(xb, wg_ref, wu_ref, wd_ref, acc_ref):
    d_ff = wg_ref.shape[-1]
    for c0 in range(0, d_ff, FF_CHUNK):
        cs = slice(c0, c0 + FF_CHUNK)
        hg = _dot(xb, wg_ref[:, cs])
        hu = _dot(xb, wu_ref[:, cs])
        hh = (hg * jax.nn.sigmoid(hg) * hu).astype(BF16)
        contrib = _dot(hh, wd_ref[cs, :])
        if c0 == 0:
            acc_ref[...] = contrib
        else:
            acc_ref[...] += contrib


def _ffn_kernel(alpha, x_ref, wg_ref, wu_ref, wd_ref, g_ref, b_ref, o_ref, acc_ref):
    x = x_ref[...]
    _swiglu_acc(x.astype(BF16), wg_ref, wu_ref, wd_ref, acc_ref)
    o_ref[...] = _layer_norm(alpha * x + acc_ref[...], g_ref[...], b_ref[...])


def _ffn_ln(x2, wg, wu, wd, g, b, alpha):
    t, d = x2.shape
    tm = min(TM_FFN, t)
    row = lambda i: (i, 0)
    const = lambda i: (0, 0)
    wspec = lambda a: pl.BlockSpec(a.shape, const, pipeline_mode=pl.Buffered(1))
    return pl.pallas_call(
        functools.partial(_ffn_kernel, alpha),
        grid=(t // tm,),
        in_specs=[pl.BlockSpec((tm, d), row), wspec(wg), wspec(wu), wspec(wd),
                  pl.BlockSpec((1, d), const), pl.BlockSpec((1, d), const)],
        out_specs=pl.BlockSpec((tm, d), row),
        out_shape=jax.ShapeDtypeStruct((t, d), F32),
        scratch_shapes=[pltpu.VMEM((tm, d), F32)],
        compiler_params=_cparams(("parallel",)),
        name="ffn_ln",
    )(x2, wg, wu, wd, g, b)


def _router_kernel(x_ref, wr_ref, o_ref, xb_ref):
    x = x_ref[...]
    xb_ref[...] = x.astype(BF16)
    logits = jnp.dot(x, wr_ref[...], preferred_element_type=F32, precision=lax.Precision.HIGHEST)
    lane = lax.broadcasted_iota(jnp.int32, logits.shape, 1)
    lg = jnp.where(lane < N_EXPERTS, logits, -jnp.inf)
    m1 = jnp.max(lg, axis=-1, keepdims=True)
    i1 = jnp.min(jnp.where(lg == m1, lane, LANES), axis=-1, keepdims=True)
    lg2 = jnp.where(lane == i1, -jnp.inf, lg)
    m2 = jnp.max(lg2, axis=-1, keepdims=True)
    i2 = jnp.min(jnp.where(lg2 == m2, lane, LANES), axis=-1, keepdims=True)
    e = jnp.exp(m2 - m1)
    g1 = 1.0 / (1.0 + e)
    g2 = e / (1.0 + e)
    out = jnp.where(lane == 0, i1.astype(F32), 0.0)
    out = jnp.where(lane == 1, i2.astype(F32), out)
    out = jnp.where(lane == 2, g1, out)
    out = jnp.where(lane == 3, g2, out)
    o_ref[...] = out


def _router(x2, wr):
    t, d = x2.shape
    tm = min(TM_PROJ, t)
    row = lambda i: (i, 0)
    return pl.pallas_call(
        _router_kernel,
        grid=(t // tm,),
        in_specs=[pl.BlockSpec((tm, d), row), pl.BlockSpec((d, LANES), lambda i: (0, 0))],
        out_specs=[pl.BlockSpec((tm, LANES), row), pl.BlockSpec((tm, d), row)],
        out_shape=[jax.ShapeDtypeStruct((t, LANES), F32), jax.ShapeDtypeStruct((t, d), BF16)],
        compiler_params=_cparams(("parallel",)),
        name="router",
    )(x2, wr)


def _expert_kernel(be_ref, nu_ref, x_ref, wg_ref, wu_ref, wd_ref, o_ref, acc_ref):
    i = pl.program_id(0)

    @pl.when(i < nu_ref[0])
    def _():
        _swiglu_acc(x_ref[...], wg_ref, wu_ref, wd_ref, acc_ref)
        o_ref[...] = acc_ref[...].astype(BF16)

    @pl.when(i >= nu_ref[0])
    def _():
        o_ref[...] = jnp.zeros_like(o_ref)


def _experts(xb, blk_e, n_used, wg, wu, wd):
    n_slot, d = xb.shape
    n_blk = n_slot // MOE_BLK
    dff = wg.shape[-1]
    row = lambda i, be, nu: (i, 0)
    wmap = lambda i, be, nu: (be[i], 0, 0)
    return pl.pallas_call(
        _expert_kernel,
        grid_spec=pltpu.PrefetchScalarGridSpec(
            num_scalar_prefetch=2,
            grid=(n_blk,),
            in_specs=[pl.BlockSpec((MOE_BLK, d), row),
                      pl.BlockSpec((None, d, dff), wmap, pipeline_mode=pl.Buffered(1)),
                      pl.BlockSpec((None, d, dff), wmap, pipeline_mode=pl.Buffered(1)),
                      pl.BlockSpec((None, dff, d), wmap, pipeline_mode=pl.Buffered(1))],
            out_specs=pl.BlockSpec((MOE_BLK, d), row),
            scratch_shapes=[pltpu.VMEM((MOE_BLK, d), F32)],
        ),
        out_shape=jax.ShapeDtypeStruct((n_slot, d), BF16),
        compiler_params=_cparams(("arbitrary",)),
        name="experts",
    )(blk_e, n_used, xb, wg, wu, wd)


def _combine_kernel(alpha, x_ref, y_ref, r_ref, g_ref, b_ref, o_ref):
    d = x_ref.shape[1]
    r = r_ref[...]
    f = r[:, 2:3] * y_ref[:, 0:d].astype(F32) + r[:, 3:4] * y_ref[:, d:2 * d].astype(F32)
    o_ref[...] = _layer_norm(alpha * x_ref[...] + f, g_ref[...], b_ref[...])


def _combine_ln(x2, y2, route, g, b, alpha):
    t, d = x2.shape
    tm = min(TM_PROJ, t)
    row = lambda i: (i, 0)
    const = lambda i: (0, 0)
    return pl.pallas_call(
        functools.partial(_combine_kernel, alpha),
        grid=(t // tm,),
        in_specs=[pl.BlockSpec((tm, d), row), pl.BlockSpec((tm, 2 * d), row), pl.BlockSpec((tm, LANES), row),
                  pl.BlockSpec((1, d), const), pl.BlockSpec((1, d), const)],
        out_specs=pl.BlockSpec((tm, d), row),
        out_shape=jax.ShapeDtypeStruct((t, d), F32),
        compiler_params=_cparams(("parallel",)),
        name="combine_ln",
    )(x2, y2, route, g, b)


def _moe(x2, wr, wg, wu, wd, g, b, alpha):
    t, d = x2.shape
    route, xb16 = _router(x2, wr)
    top_idx = route[:, 0:2].astype(jnp.int32)
    n_asg = t * TOP_K
    flat_e = top_idx.reshape(n_asg)
    onehot = (flat_e[:, None] == jnp.arange(N_EXPERTS, dtype=jnp.int32)[None, :]).astype(jnp.int32)
    csum = jnp.cumsum(onehot, axis=0)
    rank = jnp.sum((csum - onehot) * onehot, axis=1)
    counts = csum[-1]
    padded = (counts + MOE_BLK - 1) // MOE_BLK * MOE_BLK
    pad_end = jnp.cumsum(padded)
    pad_start = pad_end - padded
    slot = pad_start[flat_e] + rank
    n_slot = -(-n_asg // MOE_BLK) * MOE_BLK + N_EXPERTS * MOE_BLK
    n_blk = n_slot // MOE_BLK
    blk_e = jnp.minimum(jnp.searchsorted(pad_end, jnp.arange(n_blk, dtype=jnp.int32) * MOE_BLK, side='right'),
                        N_EXPERTS - 1).astype(jnp.int32)
    n_used = (pad_end[-1:] // MOE_BLK).astype(jnp.int32)
    tok = jnp.arange(n_asg, dtype=jnp.int32) // TOP_K
    slot_tok = jnp.full((n_slot,), t, jnp.int32).at[slot].set(tok)
    x_pad = jnp.concatenate([xb16, jnp.zeros((1, d), BF16)], axis=0)
    xb = x_pad[slot_tok]
    yb = _experts(xb, blk_e, n_used, wg, wu, wd)
    y2 = yb[slot].reshape(t, TOP_K * d)
    return _combine_ln(x2, y2, route, g, b, alpha)


def _t5_causal_bucket(dist):
    max_exact = REL_BUCKETS // 2
    d = jnp.maximum(dist, 1).astype(F32)
    large = max_exact + (jnp.log(d / max_exact) / math.log(REL_MAX_DIST / max_exact)
                         * (REL_BUCKETS - max_exact)).astype(jnp.int32)
    large = jnp.minimum(large, REL_BUCKETS - 1)
    return jnp.where(dist < max_exact, dist, large)


def _dn_constants(ts):
    lanes = np.arange(DN_W)
    ea = np.zeros((LANES, DN_W), np.float32)
    eb = np.zeros((LANES, DN_W), np.float32)
    ea[lanes // DN_D, lanes] = 1.0
    eb[DN_HEADS + lanes // DN_D, lanes] = 1.0
    seg = (lanes[:, None] // DN_D == lanes[None, :] // DN_D).astype(np.float32)
    r = np.arange(ts)
    same = r[:, None] // DN_CHUNK == r[None, :] // DN_CHUNK
    tri = (same & (r[:, None] >= r[None, :])).astype(np.float32)
    ones = same.astype(np.float32)
    return tuple(jnp.asarray(a, BF16) for a in (ea, eb, seg, tri, ones))


def kernel(x, w_in, w_out, conv_w, dn_a_log, dn_dt_bias, dn_norm_w, df_lambda, df_subln_w, rel_bias,
           ln1_g, ln1_b, ln2_g, ln2_b, ffn_w_gate, ffn_w_up, ffn_w_down, moe_router, moe_w_gate,
           moe_w_up, moe_w_down):
    bsz, seq, d = x.shape
    depth = w_in.shape[0]
    alpha = (2 * depth) ** 0.25
    t = bsz * seq
    tq = min(TQ_DF, seq)
    assert tq >= REL_MAX_DIST and seq % tq == 0 and seq % min(TS_DN, seq) == 0
    btiles = _attn_bias_tiles(rel_bias, seq, tq)

    dn_consts = _dn_constants(min(TS_DN, seq))
    c_dn = 3 * DN_W
    x2 = x.reshape(t, d)
    for layer in range(depth):
        lambda_init = 0.8 - 0.6 * math.exp(-0.3 * layer)
        wl = w_in[layer]
        n1 = c_dn + DN_W + 2 * DN_HEADS
        w1 = jnp.concatenate([wl[:, :n1], jnp.zeros((d, LANES - 2 * DN_HEADS), F32)], axis=1).astype(BF16)
        w2 = wl[:, n1:n1 + 2 * DF_W].astype(BF16)
        wvt = wl[:, n1 + 2 * DF_W:].T.astype(BF16)
        qkv, z, ab, qd, kd, vdt = _inproj(x2, w1, w2, wvt, bsz, seq)

        convw = jnp.concatenate([conv_w[layer], jnp.zeros((HALO - DN_CONV, c_dn), F32)], axis=0)
        par = jnp.zeros((HALO, DN_W), F32)
        par = par.at[0].set(jnp.repeat(dn_a_log[layer], DN_D))
        par = par.at[1].set(jnp.repeat(dn_dt_bias[layer], DN_D))
        par = par.at[2].set(jnp.tile(dn_norm_w[layer], DN_HEADS))
        o_dn = _deltanet(qkv, ab, z, convw, par, dn_consts, bsz, seq)

        lf = df_lambda[layer].astype(F32)
        lam = jnp.exp(jnp.sum(lf[0] * lf[1])) - jnp.exp(jnp.sum(lf[2] * lf[3])) + lambda_init
        scalars = jnp.stack([lam, jnp.asarray(1.0 - lambda_init, F32)]).astype(F32)
        o_df = _diff_attention(qd, kd, vdt, btiles, df_subln_w[layer].reshape(2 * DF_D, 1), scalars, bsz, seq)

        x2 = _outproj_ln(x2, o_dn, o_df, w_out[layer].astype(BF16), ln1_g[layer].reshape(1, d),
                         ln1_b[layer].reshape(1, d), alpha)
        i = layer // 2
        g2 = ln2_g[layer].reshape(1, d)
        b2 = ln2_b[layer].reshape(1, d)
        if layer % 2 == 0:
            x2 = _ffn_ln(x2, ffn_w_gate[i].astype(BF16), ffn_w_up[i].astype(BF16), ffn_w_down[i].astype(BF16),
                         g2, b2, alpha)
        else:
            wr = jnp.concatenate([moe_router[i], jnp.zeros((d, LANES - N_EXPERTS), F32)], axis=1)
            x2 = _moe(x2, wr, moe_w_gate[i].astype(BF16), moe_w_up[i].astype(BF16), moe_w_down[i].astype(BF16),
                      g2, b2, alpha)
    return x2.reshape(bsz, seq, d)
```

```python
import functools
import math

import jax
import jax.numpy as jnp
import numpy as np
from jax import lax
from jax.experimental import pallas as pl
from jax.experimental.pallas import tpu as pltpu

F32 = jnp.float32
BF16 = jnp.bfloat16

DN_HEADS = 8
DN_D = 64
DN_CONV = 4
DN_CHUNK = 64
DN_W = DN_HEADS * DN_D
DF_HEADS = 4
DF_D = 64
DF_W = DF_HEADS * 2 * DF_D
REL_BUCKETS = 32
REL_MAX_DIST = 128
N_EXPERTS = 8
TOP_K = 2
MOE_BLK = 512
LN_EPS = 1e-5
RMS_EPS = 1e-6
LOG2E = math.log2(math.e)

LANES = 128
HALO = 8
PAIR = 2 * DN_D
N_PAIRS = DN_HEADS // 2
VMEM_LIMIT = 56 * 1024 * 1024

TM_PROJ = 512
TS_DN = 256
DN_UNROLL = 4
TQ_DF = 512
TM_FFN = 512
FF_CHUNK = 256


def _cparams(sem):
    return pltpu.CompilerParams(dimension_semantics=sem, vmem_limit_bytes=VMEM_LIMIT)


def _dot(a, b):
    return jnp.dot(a, b, preferred_element_type=F32)


def _dot_nt(a, b):
    return lax.dot_general(a, b, (((1,), (1,)), ((), ())), preferred_element_type=F32)


def _dot_tn(a, b):
    return lax.dot_general(a, b, (((0,), (0,)), ((), ())), preferred_element_type=F32)


def _split(x):
    hi = x.astype(BF16)
    lo = (x - hi.astype(F32)).astype(BF16)
    return hi, lo


def _mm_xc(x, c):
    hi, lo = _split(x)
    return _dot(hi, c) + _dot(lo, c)


def _mm_cx(c, x):
    hi, lo = _split(x)
    return _dot(c, hi) + _dot(c, lo)


def _mm3(a, b):
    ah, al = _split(a)
    bh, bl = _split(b)
    return _dot(ah, bh) + _dot(ah, bl) + _dot(al, bh)


def _layer_norm(v, g, b):
    mu = jnp.mean(v, axis=-1, keepdims=True)
    d = v - mu
    var = jnp.mean(d * d, axis=-1, keepdims=True)
    return d * lax.rsqrt(var + LN_EPS) * g + b


def _inproj_kernel(x_ref, w1_ref, w2_ref, wvt_ref, qkv_ref, z_ref, ab_ref, qd_ref, kd_ref, vd_ref):
    xb = x_ref[...].astype(BF16)
    c = 3 * DN_W
    qkv_ref[...] = _dot(xb, w1_ref[:, 0:c])
    z_ref[...] = _dot(xb, w1_ref[:, c:c + DN_W])
    ab_ref[...] = _dot(xb, w1_ref[:, c + DN_W:c + DN_W + LANES])
    qd_ref[...] = (_dot(xb, w2_ref[:, 0:DF_W]) * (DF_D ** -0.5 * LOG2E)).astype(BF16)
    kd_ref[...] = _dot(xb, w2_ref[:, DF_W:2 * DF_W]).astype(BF16)
    vd_ref[...] = _dot_nt(wvt_ref[...], xb).astype(BF16)


def _inproj(x2, w1, w2, wvt, bsz, seq):
    t, d = x2.shape
    tm = min(TM_PROJ, seq)
    nt = seq // tm
    n1, n2 = w1.shape[1], w2.shape[1]
    row = lambda i: (i, 0)
    const = lambda i: (0, 0)
    return pl.pallas_call(
        _inproj_kernel,
        grid=(t // tm,),
        in_specs=[pl.BlockSpec((tm, d), row),
                  pl.BlockSpec((d, n1), const, pipeline_mode=pl.Buffered(1)),
                  pl.BlockSpec((d, n2), const, pipeline_mode=pl.Buffered(1)),
                  pl.BlockSpec((DF_W, d), const, pipeline_mode=pl.Buffered(1))],
        out_specs=[pl.BlockSpec((tm, 3 * DN_W), row), pl.BlockSpec((tm, DN_W), row),
                   pl.BlockSpec((tm, LANES), row), pl.BlockSpec((tm, DF_W), row),
                   pl.BlockSpec((tm, DF_W), row),
                   pl.BlockSpec((None, DF_W, tm), lambda i: (i // nt, 0, i % nt))],
        out_shape=[jax.ShapeDtypeStruct((t, 3 * DN_W), F32), jax.ShapeDtypeStruct((t, DN_W), F32),
                   jax.ShapeDtypeStruct((t, LANES), F32), jax.ShapeDtypeStruct((t, DF_W), BF16),
                   jax.ShapeDtypeStruct((t, DF_W), BF16), jax.ShapeDtypeStruct((bsz, DF_W, seq), BF16)],
        compiler_params=_cparams(("parallel",)),
        name="inproj",
    )(x2, w1, w2, wvt)


def _dn_kernel(qkv_ref, halo_ref, ab_ref, z_ref, convw_ref, par_ref, ea_ref, eb_ref, seg_ref,
               tri_ref, ones_ref, o_ref,
               xe_s, q_s, k_s, kb_s, vb_s, gc_s, eg_s, kdec_s, gl_s, od_s, qe_s, m_s, n_s, state_s):
    i = pl.program_id(1)
    ts = qkv_ref.shape[0]
    n_chunks = ts // DN_CHUNK

    @pl.when(i == 0)
    def _():
        state_s[...] = jnp.zeros_like(state_s)

    halo = halo_ref[...]
    xe_s[0:HALO, :] = jnp.where(i > 0, halo, jnp.zeros_like(halo))
    xe_s[HALO:, :] = qkv_ref[...]
    xe = xe_s[...]
    y = convw_ref[3:4, :] * xe[HALO:, :]
    for tap in range(DN_CONV - 1):
        y += convw_ref[tap:tap + 1, :] * pltpu.roll(xe, DN_CONV - 1 - tap, axis=0)[HALO:, :]
    y = y * jax.nn.sigmoid(y)
    q = y[:, 0:DN_W]
    k = y[:, DN_W:2 * DN_W]
    v = y[:, 2 * DN_W:3 * DN_W]

    seg = seg_ref[...]
    q = q * lax.rsqrt(_dot((q * q).astype(BF16), seg) + 1e-6) * (DN_D ** -0.5)
    k = k * lax.rsqrt(_dot((k * k).astype(BF16), seg) + 1e-6)

    ab = ab_ref[...]
    a_l = _mm_xc(ab, ea_ref[...])
    b_l = _mm_xc(ab, eb_ref[...])
    beta = jax.nn.sigmoid(b_l)
    xa = a_l + par_ref[1:2, :]
    softplus = jnp.maximum(xa, 0.0) + jnp.log(1.0 + jnp.exp(-jnp.abs(xa)))
    g = -jnp.exp(par_ref[0:1, :]) * softplus
    gc = _mm_cx(tri_ref[...], g)
    gl = _mm_cx(ones_ref[...], g)
    eg = jnp.exp(gc)
    kb = k * beta
    q_s[...] = q
    k_s[...] = k
    kb_s[...] = kb
    vb_s[...] = v * beta
    gc_s[...] = gc
    eg_s[...] = eg
    kdec_s[...] = k * jnp.exp(gl - gc)
    gl_s[...] = jnp.exp(gl)

    lane = lax.broadcasted_iota(jnp.int32, (DN_CHUNK, PAIR), 1)
    rowi = lax.broadcasted_iota(jnp.int32, (DN_CHUNK, PAIR), 0)
    colj = jnp.where(lane >= DN_D, lane - DN_D, lane)
    even = lane < DN_D
    eye2 = rowi == colj
    lower = rowi >= colj
    lane_b = lax.broadcasted_iota(jnp.int32, (PAIR, PAIR), 1)
    row_b = lax.broadcasted_iota(jnp.int32, (PAIR, PAIR), 0)
    bdmask = (lane_b < DN_D) == (row_b < DN_D)

    def bd(xm):
        z0 = jnp.zeros_like(xm)
        return jnp.concatenate([jnp.where(even, xm, z0), jnp.where(even, z0, xm)], axis=0)

    eye_f = jnp.where(eye2, 1.0, 0.0)

    def local_body(cc, carry):
        chains = [(cc * DN_UNROLL + dc, p) for dc in range(DN_UNROLL) for p in range(N_PAIRS)]
        n = len(chains)
        rows = [pl.ds(pl.multiple_of(c * DN_CHUNK, DN_CHUNK), DN_CHUNK) for c, _ in chains]
        cols = [slice(p * PAIR, (p + 1) * PAIR) for _, p in chains]
        ld = lambda ref, i: ref[rows[i], cols[i]]

        aq = [_dot_nt(jnp.concatenate([ld(kb_s, i), ld(q_s, i)], axis=0).astype(BF16),
                      bd(ld(k_s, i).astype(BF16))) for i in range(n)]
        a_qk, l_m = [], []
        for i in range(n):
            gcc = ld(gc_s, i)
            gcj = jnp.sum(jnp.where(eye2, gcc, 0.0), axis=0, keepdims=True)
            dec = jnp.where(lower, jnp.exp(jnp.minimum(gcc - gcj, 0.0)), 0.0)
            a_qk.append((aq[i][DN_CHUNK:, :] * dec).astype(BF16))
            l_m.append(jnp.where(eye2, 0.0, aq[i][:DN_CHUNK, :] * dec))

        lhl = [_split(m) for m in l_m]
        x0 = [eye_f - m for m in l_m]
        yb = [h for h, _ in lhl]
        for _ in range(5):
            yb = [_dot(yb[i], bd(yb[i])).astype(BF16) for i in range(n)]
            x0 = [x0[i] + _dot(x0[i].astype(BF16), bd(yb[i])) for i in range(n)]
        xhl = [_split(m) for m in x0]
        res = []
        for i in range(n):
            bxh = bd(xhl[i][0])
            lx = _dot(lhl[i][0], bxh) + _dot(lhl[i][0], bd(xhl[i][1])) + _dot(lhl[i][1], bxh)
            res.append((eye_f - x0[i] - lx).astype(BF16))
        t_m = [x0[i] + _dot(xhl[i][0], bd(res[i])) for i in range(n)]

        uw = []
        for i in range(n):
            th, tl = _split(t_m[i])
            kbg = (ld(kb_s, i) * ld(eg_s, i)).astype(BF16)
            rhs = jnp.concatenate([bd(ld(vb_s, i).astype(BF16)), bd(kbg)], axis=1)
            uw.append((_dot(th, rhs) + _dot(tl, rhs)).astype(BF16))
        qo = [_dot(a_qk[i], jnp.concatenate([bd(uw[i][:, PAIR:]), bd(uw[i][:, :PAIR])], axis=1))
              for i in range(n)]
        mn = [_dot_tn(ld(kdec_s, i).astype(BF16), jnp.concatenate([uw[i][:, PAIR:], uw[i][:, :PAIR]], axis=1))
              for i in range(n)]
        for i, (c, p) in enumerate(chains):
            qe_s[rows[i], cols[i]] = (ld(q_s, i) * ld(eg_s, i) - qo[i][:, :PAIR]).astype(BF16)
            od_s[rows[i], cols[i]] = qo[i][:, PAIR:]
            m_s[c, p] = jnp.where(bdmask, mn[i][:, :PAIR], 0.0).astype(BF16)
            n_s[c, p] = jnp.where(bdmask, mn[i][:, PAIR:], 0.0)
        return carry

    lax.fori_loop(0, n_chunks // DN_UNROLL, local_body, 0)

    for c in range(n_chunks):
        rows = slice(c * DN_CHUNK, (c + 1) * DN_CHUNK)
        for p in range(N_PAIRS):
            cols = slice(p * PAIR, (p + 1) * PAIR)
            st = state_s[p]
            r = _dot(jnp.concatenate([qe_s[rows, cols], m_s[c, p]], axis=0), st.astype(BF16))
            od_s[rows, cols] += r[:DN_CHUNK, :]
            state_s[p] = gl_s[c * DN_CHUNK:c * DN_CHUNK + 1, cols] * st - r[DN_CHUNK:, :] + n_s[c, p]

    od = od_s[...]
    ms = _dot((od * od).astype(BF16), seg) * (1.0 / DN_D)
    zz = z_ref[...]
    o_ref[...] = (od * lax.rsqrt(ms + RMS_EPS) * par_ref[2:3, :] * (zz * jax.nn.sigmoid(zz))).astype(BF16)


def _deltanet(qkv, ab, z, convw, par, consts, bsz, seq):
    t = qkv.shape[0]
    ts = min(TS_DN, seq)
    nt = seq // ts
    hb = ts // HALO
    ea, eb, seg, tri, ones = consts
    row = lambda b, i: (b * nt + i, 0)
    const = lambda b, i: (0, 0)
    halo_map = lambda b, i: (jnp.maximum((b * nt + i) * hb - 1, 0), 0)
    cspec = lambda a: pl.BlockSpec(a.shape, const, pipeline_mode=pl.Buffered(1))
    big = lambda: pltpu.VMEM((ts, DN_W), F32)
    return pl.pallas_call(
        _dn_kernel,
        grid=(bsz, nt),
        in_specs=[pl.BlockSpec((ts, 3 * DN_W), row), pl.BlockSpec((HALO, 3 * DN_W), halo_map),
                  pl.BlockSpec((ts, LANES), row), pl.BlockSpec((ts, DN_W), row),
                  cspec(convw), cspec(par), cspec(ea), cspec(eb), cspec(seg), cspec(tri), cspec(ones)],
        out_specs=pl.BlockSpec((ts, DN_W), row),
        out_shape=jax.ShapeDtypeStruct((t, DN_W), BF16),
        scratch_shapes=[pltpu.VMEM((ts + HALO, 3 * DN_W), F32)] + [big() for _ in range(9)]
                       + [pltpu.VMEM((ts, DN_W), BF16),
                          pltpu.VMEM((ts // DN_CHUNK, N_PAIRS, PAIR, PAIR), BF16),
                          pltpu.VMEM((ts // DN_CHUNK, N_PAIRS, PAIR, PAIR), F32),
                          pltpu.VMEM((N_PAIRS, PAIR, PAIR), F32)],
        compiler_params=_cparams(("parallel", "arbitrary")),
        name="deltanet",
    )(qkv, qkv, ab, z, convw, par, ea, eb, seg, tri, ones)


def _df_kernel(sc_ref, q_ref, k_ref, vt_ref, bt_ref, w_ref, o_ref, sa_ref, sb_ref, acc_ref):
    qi = pl.program_id(2)
    tq = q_ref.shape[0]
    lane = lax.broadcasted_iota(jnp.int32, (tq, 2 * DF_D), 1)
    q = q_ref[...]
    zq = jnp.zeros_like(q)
    qs = (jnp.where(lane < DF_D, q, zq), jnp.where(lane < DF_D, zq, q))
    acc_ref[...] = jnp.zeros_like(acc_ref)

    def scores(j, s_ref):
        kk = k_ref[pl.ds(pl.multiple_of(j * tq, tq), tq), :]
        for mp in range(2):
            s_ref[mp] = _dot_nt(kk, qs[mp])

    def absorb(j, s_ref, bias, carry):
        vt = vt_ref[:, pl.ds(pl.multiple_of(j * tq, tq), tq)]
        out = []
        for mp in range(2):
            m_old, l_old = carry[mp]
            st = s_ref[mp]
            if bias is not None:
                st = st + bias
            m_new = jnp.maximum(m_old, jnp.max(st, axis=0, keepdims=True))
            alpha = jnp.exp2(m_old - m_new)
            pr = jnp.exp2(st - m_new)
            l_new = alpha * l_old + jnp.sum(pr, axis=0, keepdims=True)
            acc_ref[mp] = alpha * acc_ref[mp] + _dot(vt, pr.astype(BF16))
            out.append((m_new, l_new))
        return tuple(out)

    init1 = (jnp.full((1, tq), -jnp.inf, F32), jnp.zeros((1, tq), F32))
    carry = (init1, init1)
    n_far = jnp.maximum(qi - 1, 0)
    odd = n_far % 2

    @pl.when(qi == 0)
    def _():
        scores(0, sb_ref)

    @pl.when(jnp.logical_and(qi > 0, odd == 0))
    def _():
        scores(0, sa_ref)

    def odd_step(_, c):
        scores(0, sb_ref)
        scores(1, sa_ref)
        return absorb(0, sb_ref, None, c)

    carry = lax.fori_loop(0, odd, odd_step, carry)

    def pair_step(jp, c):
        j = odd + 2 * jp
        scores(j + 1, sb_ref)
        c = absorb(j, sa_ref, None, c)
        scores(j + 2, sa_ref)
        return absorb(j + 1, sb_ref, None, c)

    carry = lax.fori_loop(0, n_far // 2, pair_step, carry)

    def prev_step(j, c):
        scores(j + 1, sb_ref)
        return absorb(j, sa_ref, bt_ref[1], c)

    carry = lax.fori_loop(n_far, qi, prev_step, carry)
    carry = absorb(qi, sb_ref, bt_ref[0], carry)

    (_, l0), (_, l1) = carry
    ot = acc_ref[0] / l0 - sc_ref[0] * (acc_ref[1] / l1)
    ms = jnp.mean(ot * ot, axis=0, keepdims=True)
    ot = ot * lax.rsqrt(ms + RMS_EPS) * w_ref[...] * sc_ref[1]
    o_ref[...] = ot.T.astype(BF16)


def _attn_bias_tiles(rel_bias, seq, tq):
    rb = rel_bias.astype(F32)
    far = rb[_t5_causal_bucket(jnp.asarray(seq - 1, jnp.int32))]
    kj = jnp.arange(tq, dtype=jnp.int32)[:, None]
    qi = jnp.arange(tq, dtype=jnp.int32)[None, :]
    dist = jnp.stack([qi - kj, qi + tq - kj])
    onehot = (_t5_causal_bucket(jnp.maximum(dist, 0))[..., None]
              == jnp.arange(REL_BUCKETS, dtype=jnp.int32)).astype(F32)
    vals = (jnp.einsum('ntqb,bh->ntqh', onehot, rb, precision=lax.Precision.HIGHEST) - far) * LOG2E
    vals = jnp.where((dist >= 0)[..., None], vals, -jnp.inf)
    return jnp.transpose(vals, (3, 0, 1, 2))


def _diff_attention(qd, kd, vdt, btiles, subln_w, scalars, bsz, seq):
    t = qd.shape[0]
    tq = min(TQ_DF, seq)
    nq = seq // tq
    sbuf = pltpu.VMEM((2, tq, tq), F32)
    return pl.pallas_call(
        _df_kernel,
        grid_spec=pltpu.PrefetchScalarGridSpec(
            num_scalar_prefetch=1,
            grid=(bsz, DF_HEADS, nq),
            in_specs=[pl.BlockSpec((tq, 2 * DF_D), lambda b, h, i, sc: (b * nq + i, h)),
                      pl.BlockSpec((seq, 2 * DF_D), lambda b, h, i, sc: (b, h)),
                      pl.BlockSpec((None, 2 * DF_D, seq), lambda b, h, i, sc: (b, h, 0)),
                      pl.BlockSpec((None, 2, tq, tq), lambda b, h, i, sc: (h, 0, 0, 0)),
                      pl.BlockSpec((2 * DF_D, 1), lambda b, h, i, sc: (0, 0))],
            out_specs=pl.BlockSpec((tq, 2 * DF_D), lambda b, h, i, sc: (b * nq + i, h)),
            scratch_shapes=[sbuf, sbuf, pltpu.VMEM((2, 2 * DF_D, tq), F32)],
        ),
        out_shape=jax.ShapeDtypeStruct((t, DF_W), BF16),
        compiler_params=_cparams(("parallel", "parallel", "arbitrary")),
        name="diffattn",
    )(scalars, qd, kd, vdt, btiles, subln_w)


def _outproj_kernel(alpha, x_ref, a_ref, b_ref, w_ref, g_ref, bb_ref, o_ref):
    mix = _dot(a_ref[...], w_ref[0:DN_W, :]) + _dot(b_ref[...], w_ref[DN_W:, :])
    o_ref[...] = _layer_norm(alpha * x_ref[...] + mix, g_ref[...], bb_ref[...])


def _outproj_ln(x2, o_dn, o_df, w_out, g, b, alpha):
    t, d = x2.shape
    tm = min(TM_PROJ, t)
    row = lambda i: (i, 0)
    const = lambda i: (0, 0)
    return pl.pallas_call(
        functools.partial(_outproj_kernel, alpha),
        grid=(t // tm,),
        in_specs=[pl.BlockSpec((tm, d), row), pl.BlockSpec((tm, DN_W), row), pl.BlockSpec((tm, DF_W), row),
                  pl.BlockSpec(w_out.shape, const, pipeline_mode=pl.Buffered(1)),
                  pl.BlockSpec((1, d), const), pl.BlockSpec((1, d), const)],
        out_specs=pl.BlockSpec((tm, d), row),
        out_shape=jax.ShapeDtypeStruct((t, d), F32),
        compiler_params=_cparams(("parallel",)),
        name="outproj_ln",
    )(x2, o_dn, o_df, w_out, g, b)


def _swiglu_acc(xb, wg_ref, wu_ref, wd_ref, acc_ref):
    d_ff = wg_ref.shape[-1]
    for c0 in range(0, d_ff, FF_CHUNK):
        cs = slice(c0, c0 + FF_CHUNK)
        hg = _dot(xb, wg_ref[:, cs])
        hu = _dot(xb, wu_ref[:, cs])
        hh = (hg * jax.nn.sigmoid(hg) * hu).astype(BF16)
        contrib = _dot(hh, wd_ref[cs, :])
        if c0 == 0:
            acc_ref[...] = contrib
        else:
            acc_ref[...] += contrib


def _ffn_kernel(alpha, x_ref, wg_ref, wu_ref, wd_ref, g_ref, b_ref, o_ref, acc_ref):
    x = x_ref[...]
    _swiglu_acc(x.astype(BF16), wg_ref, wu_ref, wd_ref, acc_ref)
    o_ref[...] = _layer_norm(alpha * x + acc_ref[...], g_ref[...], b_ref[...])


def _ffn_ln(x2, wg, wu, wd, g, b, alpha):
    t, d = x2.shape
    tm = min(TM_FFN, t)
    row = lambda i: (i, 0)
    const = lambda i: (0, 0)
    wspec = lambda a: pl.BlockSpec(a.shape, const, pipeline_mode=pl.Buffered(1))
    return pl.pallas_call(
        functools.partial(_ffn_kernel, alpha),
        grid=(t // tm,),
        in_specs=[pl.BlockSpec((tm, d), row), wspec(wg), wspec(wu), wspec(wd),
                  pl.BlockSpec((1, d), const), pl.BlockSpec((1, d), const)],
        out_specs=pl.BlockSpec((tm, d), row),
        out_shape=jax.ShapeDtypeStruct((t, d), F32),
        scratch_shapes=[pltpu.VMEM((tm, d), F32)],
        compiler_params=_cparams(("parallel",)),
        name="ffn_ln",
    )(x2, wg, wu, wd, g, b)


def _router_kernel(x_ref, wr_ref, o_ref, xb_ref):
    x = x_ref[...]
    xb_ref[...] = x.astype(BF16)
    logits = jnp.dot(x, wr_ref[...], preferred_element_type=F32, precision=lax.Precision.HIGHEST)
    lane = lax.broadcasted_iota(jnp.int32, logits.shape, 1)
    lg = jnp.where(lane < N_EXPERTS, logits, -jnp.inf)
    m1 = jnp.max(lg, axis=-1, keepdims=True)
    i1 = jnp.min(jnp.where(lg == m1, lane, LANES), axis=-1, keepdims=True)
    lg2 = jnp.where(lane == i1, -jnp.inf, lg)
    m2 = jnp.max(lg2, axis=-1, keepdims=True)
    i2 = jnp.min(jnp.where(lg2 == m2, lane, LANES), axis=-1, keepdims=True)
    e = jnp.exp(m2 - m1)
    g1 = 1.0 / (1.0 + e)
    g2 = e / (1.0 + e)
    out = jnp.where(lane == 0, i1.astype(F32), 0.0)
    out = jnp.where(lane == 1, i2.astype(F32), out)
    out = jnp.where(lane == 2, g1, out)
    out = jnp.where(lane == 3, g2, out)
    o_ref[...] = out


def _router(x2, wr):
    t, d = x2.shape
    tm = min(TM_PROJ, t)
    row = lambda i: (i, 0)
    return pl.pallas_call(
        _router_kernel,
        grid=(t // tm,),
        in_specs=[pl.BlockSpec((tm, d), row), pl.BlockSpec((d, LANES), lambda i: (0, 0))],
        out_specs=[pl.BlockSpec((tm, LANES), row), pl.BlockSpec((tm, d), row)],
        out_shape=[jax.ShapeDtypeStruct((t, LANES), F32), jax.ShapeDtypeStruct((t, d), BF16)],
        compiler_params=_cparams(("parallel",)),
        name="router",
    )(x2, wr)


def _expert_kernel(be_ref, nu_ref, x_ref, wg_ref, wu_ref, wd_ref, o_ref, acc_ref):
    i = pl.program_id(0)

    @pl.when(i < nu_ref[0])
    def _():
        _swiglu_acc(x_ref[...], wg_ref, wu_ref, wd_ref, acc_ref)
        o_ref[...] = acc_ref[...].astype(BF16)

    @pl.when(i >= nu_ref[0])
    def _():
        o_ref[...] = jnp.zeros_like(o_ref)


def _experts(xb, blk_e, n_used, wg, wu, wd):
    n_slot, d = xb.shape
    n_blk = n_slot // MOE_BLK
    dff = wg.shape[-1]
    row = lambda i, be, nu: (i, 0)
    wmap = lambda i, be, nu: (be[i], 0, 0)
    return pl.pallas_call(
        _expert_kernel,
        grid_spec=pltpu.PrefetchScalarGridSpec(
            num_scalar_prefetch=2,
            grid=(n_blk,),
            in_specs=[pl.BlockSpec((MOE_BLK, d), row),
                      pl.BlockSpec((None, d, dff), wmap, pipeline_mode=pl.Buffered(1)),
                      pl.BlockSpec((None, d, dff), wmap, pipeline_mode=pl.Buffered(1)),
                      pl.BlockSpec((None, dff, d), wmap, pipeline_mode=pl.Buffered(1))],
            out_specs=pl.BlockSpec((MOE_BLK, d), row),
            scratch_shapes=[pltpu.VMEM((MOE_BLK, d), F32)],
        ),
        out_shape=jax.ShapeDtypeStruct((n_slot, d), BF16),
        compiler_params=_cparams(("arbitrary",)),
        name="experts",
    )(blk_e, n_used, xb, wg, wu, wd)


def _combine_kernel(alpha, x_ref, y0_ref, y1_ref, r_ref, g_ref, b_ref, o_ref):
    r = r_ref[...]
    f = r[:, 2:3] * y0_ref[...].astype(F32) + r[:, 3:4] * y1_ref[...].astype(F32)
    o_ref[...] = _layer_norm(alpha * x_ref[...] + f, g_ref[...], b_ref[...])


def _combine_ln(x2, y0, y1, route, g, b, alpha):
    t, d = x2.shape
    tm = min(TM_PROJ, t)
    row = lambda i: (i, 0)
    const = lambda i: (0, 0)
    return pl.pallas_call(
        functools.partial(_combine_kernel, alpha),
        grid=(t // tm,),
        in_specs=[pl.BlockSpec((tm, d), row), pl.BlockSpec((tm, d), row), pl.BlockSpec((tm, d), row),
                  pl.BlockSpec((tm, LANES), row), pl.BlockSpec((1, d), const), pl.BlockSpec((1, d), const)],
        out_specs=pl.BlockSpec((tm, d), row),
        out_shape=jax.ShapeDtypeStruct((t, d), F32),
        compiler_params=_cparams(("parallel",)),
        name="combine_ln",
    )(x2, y0, y1, route, g, b)


def _moe(x2, wr, wg, wu, wd, g, b, alpha):
    t, d = x2.shape
    route, xb16 = _router(x2, wr)
    top_idx = route[:, 0:2].astype(jnp.int32)
    n_asg = t * TOP_K
    flat_e = top_idx.reshape(n_asg)
    onehot = (flat_e[:, None] == jnp.arange(N_EXPERTS, dtype=jnp.int32)[None, :]).astype(jnp.int32)
    csum = jnp.cumsum(onehot, axis=0)
    rank = jnp.sum((csum - onehot) * onehot, axis=1)
    counts = csum[-1]
    padded = (counts + MOE_BLK - 1) // MOE_BLK * MOE_BLK
    pad_end = jnp.cumsum(padded)
    pad_start = pad_end - padded
    slot = pad_start[flat_e] + rank
    n_slot = -(-n_asg // MOE_BLK) * MOE_BLK + N_EXPERTS * MOE_BLK
    n_blk = n_slot // MOE_BLK
    blk_e = jnp.minimum(jnp.searchsorted(pad_end, jnp.arange(n_blk, dtype=jnp.int32) * MOE_BLK, side='right'),
                        N_EXPERTS - 1).astype(jnp.int32)
    n_used = (pad_end[-1:] // MOE_BLK).astype(jnp.int32)
    tok = jnp.arange(n_asg, dtype=jnp.int32) // TOP_K
    slot_tok = jnp.zeros((n_slot,), jnp.int32).at[slot].set(tok)
    xb = xb16[slot_tok]
    yb = _experts(xb, blk_e, n_used, wg, wu, wd)
    slot2 = slot.reshape(t, TOP_K)
    return _combine_ln(x2, yb[slot2[:, 0]], yb[slot2[:, 1]], route, g, b, alpha)


def _t5_causal_bucket(dist):
    max_exact = REL_BUCKETS // 2
    d = jnp.maximum(dist, 1).astype(F32)
    large = max_exact + (jnp.log(d / max_exact) / math.log(REL_MAX_DIST / max_exact)
                         * (REL_BUCKETS - max_exact)).astype(jnp.int32)
    large = jnp.minimum(large, REL_BUCKETS - 1)
    return jnp.where(dist < max_exact, dist, large)


def _dn_constants(ts):
    lanes = np.arange(DN_W)
    ea = np.zeros((LANES, DN_W), np.float32)
    eb = np.zeros((LANES, DN_W), np.float32)
    ea[lanes // DN_D, lanes] = 1.0
    eb[DN_HEADS + lanes // DN_D, lanes] = 1.0
    seg = (lanes[:, None] // DN_D == lanes[None, :] // DN_D).astype(np.float32)
    r = np.arange(ts)
    same = r[:, None] // DN_CHUNK == r[None, :] // DN_CHUNK
    tri = (same & (r[:, None] >= r[None, :])).astype(np.float32)
    ones = same.astype(np.float32)
    return tuple(jnp.asarray(a, BF16) for a in (ea, eb, seg, tri, ones))


def kernel(x, w_in, w_out, conv_w, dn_a_log, dn_dt_bias, dn_norm_w, df_lambda, df_subln_w, rel_bias,
           ln1_g, ln1_b, ln2_g, ln2_b, ffn_w_gate, ffn_w_up, ffn_w_down, moe_router, moe_w_gate,
           moe_w_up, moe_w_down):
    bsz, seq, d = x.shape
    depth = w_in.shape[0]
    alpha = (2 * depth) ** 0.25
    t = bsz * seq
    tq = min(TQ_DF, seq)
    assert tq >= REL_MAX_DIST and seq % tq == 0 and seq % min(TS_DN, seq) == 0
    btiles = _attn_bias_tiles(rel_bias, seq, tq)

    dn_consts = _dn_constants(min(TS_DN, seq))
    c_dn = 3 * DN_W
    x2 = x.reshape(t, d)
    for layer in range(depth):
        lambda_init = 0.8 - 0.6 * math.exp(-0.3 * layer)
        wl = w_in[layer]
        n1 = c_dn + DN_W + 2 * DN_HEADS
        w1 = jnp.concatenate([wl[:, :n1], jnp.zeros((d, LANES - 2 * DN_HEADS), F32)], axis=1).astype(BF16)
        w2 = wl[:, n1:n1 + 2 * DF_W].astype(BF16)
        wvt = wl[:, n1 + 2 * DF_W:].T.astype(BF16)
        qkv, z, ab, qd, kd, vdt = _inproj(x2, w1, w2, wvt, bsz, seq)

        convw = jnp.concatenate([conv_w[layer], jnp.zeros((HALO - DN_CONV, c_dn), F32)], axis=0)
        par = jnp.zeros((HALO, DN_W), F32)
        par = par.at[0].set(jnp.repeat(dn_a_log[layer], DN_D))
        par = par.at[1].set(jnp.repeat(dn_dt_bias[layer], DN_D))
        par = par.at[2].set(jnp.tile(dn_norm_w[layer], DN_HEADS))
        o_dn = _deltanet(qkv, ab, z, convw, par, dn_consts, bsz, seq)

        lf = df_lambda[layer].astype(F32)
        lam = jnp.exp(jnp.sum(lf[0] * lf[1])) - jnp.exp(jnp.sum(lf[2] * lf[3])) + lambda_init
        scalars = jnp.stack([lam, jnp.asarray(1.0 - lambda_init, F32)]).astype(F32)
        o_df = _diff_attention(qd, kd, vdt, btiles, df_subln_w[layer].reshape(2 * DF_D, 1), scalars, bsz, seq)

        x2 = _outproj_ln(x2, o_dn, o_df, w_out[layer].astype(BF16), ln1_g[layer].reshape(1, d),
                         ln1_b[layer].reshape(1, d), alpha)
        i = layer // 2
        g2 = ln2_g[layer].reshape(1, d)
        b2 = ln2_b[layer].reshape(1, d)
        if layer % 2 == 0:
            x2 = _ffn_ln(x2, ffn_w_gate[i].astype(BF16), ffn_w_up[i].astype(BF16), ffn_w_down[i].astype(BF16),
                         g2, b2, alpha)
        else:
            wr = jnp.concatenate([moe_router[i], jnp.zeros((d, LANES - N_EXPERTS), F32)], axis=1)
            x2 = _moe(x2, wr, moe_w_gate[i].astype(BF16), moe_w_up[i].astype(BF16), moe_w_down[i].astype(BF16),
                      g2, b2, alpha)
    return x2.reshape(bsz, seq, d)
```

```python
import functools
import math

import jax
import jax.numpy as jnp
import numpy as np
from jax import lax
from jax.experimental import pallas as pl
from jax.experimental.pallas import tpu as pltpu
from jax.experimental.pallas import tpu_sc as plsc

F32 = jnp.float32
BF16 = jnp.bfloat16

DN_HEADS = 8
DN_D = 64
DN_CONV = 4
DN_CHUNK = 64
DN_W = DN_HEADS * DN_D
DF_HEADS = 4
DF_D = 64
DF_W = DF_HEADS * 2 * DF_D
REL_BUCKETS = 32
REL_MAX_DIST = 128
N_EXPERTS = 8
TOP_K = 2
MOE_BLK = 512
LN_EPS = 1e-5
RMS_EPS = 1e-6
LOG2E = math.log2(math.e)

LANES = 128
HALO = 8
PAIR = 2 * DN_D
N_PAIRS = DN_HEADS // 2
VMEM_LIMIT = 56 * 1024 * 1024

TM_PROJ = 512
TS_DN = 256
DN_UNROLL = 4
TQ_DF = 512
TM_FFN = 512
FF_CHUNK = 256
SC_WIN = 64
SC_IDX_TILE = 128


def _cparams(sem):
    return pltpu.CompilerParams(dimension_semantics=sem, vmem_limit_bytes=VMEM_LIMIT)


def _dot(a, b):
    return jnp.dot(a, b, preferred_element_type=F32)


def _dot_nt(a, b):
    return lax.dot_general(a, b, (((1,), (1,)), ((), ())), preferred_element_type=F32)


def _dot_tn(a, b):
    return lax.dot_general(a, b, (((0,), (0,)), ((), ())), preferred_element_type=F32)


def _split(x):
    hi = x.astype(BF16)
    lo = (x - hi.astype(F32)).astype(BF16)
    return hi, lo


def _mm_xc(x, c):
    hi, lo = _split(x)
    return _dot(hi, c) + _dot(lo, c)


def _mm_cx(c, x):
    hi, lo = _split(x)
    return _dot(c, hi) + _dot(c, lo)


def _mm3(a, b):
    ah, al = _split(a)
    bh, bl = _split(b)
    return _dot(ah, bh) + _dot(ah, bl) + _dot(al, bh)


def _layer_norm(v, g, b):
    mu = jnp.mean(v, axis=-1, keepdims=True)
    d = v - mu
    var = jnp.mean(d * d, axis=-1, keepdims=True)
    return d * lax.rsqrt(var + LN_EPS) * g + b


def _inproj_kernel(x_ref, w1_ref, w2_ref, wvt_ref, qkv_ref, z_ref, ab_ref, qd_ref, kd_ref, vd_ref):
    xb = x_ref[...].astype(BF16)
    c = 3 * DN_W
    qkv_ref[...] = _dot(xb, w1_ref[:, 0:c])
    z_ref[...] = _dot(xb, w1_ref[:, c:c + DN_W])
    ab_ref[...] = _dot(xb, w1_ref[:, c + DN_W:c + DN_W + LANES])
    qd_ref[...] = (_dot(xb, w2_ref[:, 0:DF_W]) * (DF_D ** -0.5 * LOG2E)).astype(BF16)
    kd_ref[...] = _dot(xb, w2_ref[:, DF_W:2 * DF_W]).astype(BF16)
    vd_ref[...] = _dot_nt(wvt_ref[...], xb).astype(BF16)


def _inproj(x2, w1, w2, wvt, bsz, seq):
    t, d = x2.shape
    tm = min(TM_PROJ, seq)
    nt = seq // tm
    n1, n2 = w1.shape[1], w2.shape[1]
    row = lambda i: (i, 0)
    const = lambda i: (0, 0)
    return pl.pallas_call(
        _inproj_kernel,
        grid=(t // tm,),
        in_specs=[pl.BlockSpec((tm, d), row),
                  pl.BlockSpec((d, n1), const, pipeline_mode=pl.Buffered(1)),
                  pl.BlockSpec((d, n2), const, pipeline_mode=pl.Buffered(1)),
                  pl.BlockSpec((DF_W, d), const, pipeline_mode=pl.Buffered(1))],
        out_specs=[pl.BlockSpec((tm, 3 * DN_W), row), pl.BlockSpec((tm, DN_W), row),
                   pl.BlockSpec((tm, LANES), row), pl.BlockSpec((tm, DF_W), row),
                   pl.BlockSpec((tm, DF_W), row),
                   pl.BlockSpec((None, DF_W, tm), lambda i: (i // nt, 0, i % nt))],
        out_shape=[jax.ShapeDtypeStruct((t, 3 * DN_W), F32), jax.ShapeDtypeStruct((t, DN_W), F32),
                   jax.ShapeDtypeStruct((t, LANES), F32), jax.ShapeDtypeStruct((t, DF_W), BF16),
                   jax.ShapeDtypeStruct((t, DF_W), BF16), jax.ShapeDtypeStruct((bsz, DF_W, seq), BF16)],
        compiler_params=_cparams(("parallel",)),
        name="inproj",
    )(x2, w1, w2, wvt)


def _dn_kernel(qkv_ref, halo_ref, ab_ref, z_ref, convw_ref, par_ref, ea_ref, eb_ref, seg_ref,
               tri_ref, ones_ref, o_ref,
               xe_s, q_s, k_s, kb_s, vb_s, gc_s, eg_s, kdec_s, gl_s, od_s, qe_s, m_s, n_s, state_s):
    i = pl.program_id(1)
    ts = qkv_ref.shape[0]
    n_chunks = ts // DN_CHUNK

    @pl.when(i == 0)
    def _():
        state_s[...] = jnp.zeros_like(state_s)

    halo = halo_ref[...]
    xe_s[0:HALO, :] = jnp.where(i > 0, halo, jnp.zeros_like(halo))
    xe_s[HALO:, :] = qkv_ref[...]
    xe = xe_s[...]
    y = convw_ref[3:4, :] * xe[HALO:, :]
    for tap in range(DN_CONV - 1):
        y += convw_ref[tap:tap + 1, :] * pltpu.roll(xe, DN_CONV - 1 - tap, axis=0)[HALO:, :]
    y = y * jax.nn.sigmoid(y)
    q = y[:, 0:DN_W]
    k = y[:, DN_W:2 * DN_W]
    v = y[:, 2 * DN_W:3 * DN_W]

    seg = seg_ref[...]
    q = q * lax.rsqrt(_dot((q * q).astype(BF16), seg) + 1e-6) * (DN_D ** -0.5)
    k = k * lax.rsqrt(_dot((k * k).astype(BF16), seg) + 1e-6)

    ab = ab_ref[...]
    a_l = _mm_xc(ab, ea_ref[...])
    b_l = _mm_xc(ab, eb_ref[...])
    beta = jax.nn.sigmoid(b_l)
    xa = a_l + par_ref[1:2, :]
    softplus = jnp.maximum(xa, 0.0) + jnp.log(1.0 + jnp.exp(-jnp.abs(xa)))
    g = -jnp.exp(par_ref[0:1, :]) * softplus
    gc = _mm_cx(tri_ref[...], g)
    gl = _mm_cx(ones_ref[...], g)
    eg = jnp.exp(gc)
    kb = k * beta
    q_s[...] = q
    k_s[...] = k
    kb_s[...] = kb
    vb_s[...] = v * beta
    gc_s[...] = gc
    eg_s[...] = eg
    kdec_s[...] = k * jnp.exp(gl - gc)
    gl_s[...] = jnp.exp(gl)

    lane = lax.broadcasted_iota(jnp.int32, (DN_CHUNK, PAIR), 1)
    rowi = lax.broadcasted_iota(jnp.int32, (DN_CHUNK, PAIR), 0)
    colj = jnp.where(lane >= DN_D, lane - DN_D, lane)
    even = lane < DN_D
    eye2 = rowi == colj
    lower = rowi >= colj
    lane_b = lax.broadcasted_iota(jnp.int32, (PAIR, PAIR), 1)
    row_b = lax.broadcasted_iota(jnp.int32, (PAIR, PAIR), 0)
    bdmask = (lane_b < DN_D) == (row_b < DN_D)

    def bd(xm):
        z0 = jnp.zeros_like(xm)
        return jnp.concatenate([jnp.where(even, xm, z0), jnp.where(even, z0, xm)], axis=0)

    eye_f = jnp.where(eye2, 1.0, 0.0)

    def local_body(cc, carry):
        chains = [(cc * DN_UNROLL + dc, p) for dc in range(DN_UNROLL) for p in range(N_PAIRS)]
        n = len(chains)
        rows = [pl.ds(pl.multiple_of(c * DN_CHUNK, DN_CHUNK), DN_CHUNK) for c, _ in chains]
        cols = [slice(p * PAIR, (p + 1) * PAIR) for _, p in chains]
        ld = lambda ref, i: ref[rows[i], cols[i]]

        aq = [_dot_nt(jnp.concatenate([ld(kb_s, i), ld(q_s, i)], axis=0).astype(BF16),
                      bd(ld(k_s, i).astype(BF16))) for i in range(n)]
        a_qk, l_m = [], []
        for i in range(n):
            gcc = ld(gc_s, i)
            gcj = jnp.sum(jnp.where(eye2, gcc, 0.0), axis=0, keepdims=True)
            dec = jnp.where(lower, jnp.exp(jnp.minimum(gcc - gcj, 0.0)), 0.0)
            a_qk.append((aq[i][DN_CHUNK:, :] * dec).astype(BF16))
            l_m.append(jnp.where(eye2, 0.0, aq[i][:DN_CHUNK, :] * dec))

        lhl = [_split(m) for m in l_m]
        x0 = [eye_f - m for m in l_m]
        yb = [h for h, _ in lhl]
        for _ in range(5):
            yb = [_dot(yb[i], bd(yb[i])).astype(BF16) for i in range(n)]
            x0 = [x0[i] + _dot(x0[i].astype(BF16), bd(yb[i])) for i in range(n)]
        xhl = [_split(m) for m in x0]
        res = []
        for i in range(n):
            bxh = bd(xhl[i][0])
            lx = _dot(lhl[i][0], bxh) + _dot(lhl[i][0], bd(xhl[i][1])) + _dot(lhl[i][1], bxh)
            res.append((eye_f - x0[i] - lx).astype(BF16))
        t_m = [x0[i] + _dot(xhl[i][0], bd(res[i])) for i in range(n)]

        uw = []
        for i in range(n):
            th, tl = _split(t_m[i])
            kbg = (ld(kb_s, i) * ld(eg_s, i)).astype(BF16)
            rhs = jnp.concatenate([bd(ld(vb_s, i).astype(BF16)), bd(kbg)], axis=1)
            uw.append((_dot(th, rhs) + _dot(tl, rhs)).astype(BF16))
        qo = [_dot(a_qk[i], jnp.concatenate([bd(uw[i][:, PAIR:]), bd(uw[i][:, :PAIR])], axis=1))
              for i in range(n)]
        mn = [_dot_tn(ld(kdec_s, i).astype(BF16), jnp.concatenate([uw[i][:, PAIR:], uw[i][:, :PAIR]], axis=1))
              for i in range(n)]
        for i, (c, p) in enumerate(chains):
            qe_s[rows[i], cols[i]] = (ld(q_s, i) * ld(eg_s, i) - qo[i][:, :PAIR]).astype(BF16)
            od_s[rows[i], cols[i]] = qo[i][:, PAIR:]
            m_s[c, p] = jnp.where(bdmask, mn[i][:, :PAIR], 0.0).astype(BF16)
            n_s[c, p] = jnp.where(bdmask, mn[i][:, PAIR:], 0.0)
        return carry

    lax.fori_loop(0, n_chunks // DN_UNROLL, local_body, 0)

    for c in range(n_chunks):
        rows = slice(c * DN_CHUNK, (c + 1) * DN_CHUNK)
        for p in range(N_PAIRS):
            cols = slice(p * PAIR, (p + 1) * PAIR)
            st = state_s[p]
            r = _dot(jnp.concatenate([qe_s[rows, cols], m_s[c, p]], axis=0), st.astype(BF16))
            od_s[rows, cols] += r[:DN_CHUNK, :]
            state_s[p] = gl_s[c * DN_CHUNK:c * DN_CHUNK + 1, cols] * st - r[DN_CHUNK:, :] + n_s[c, p]

    od = od_s[...]
    ms = _dot((od * od).astype(BF16), seg) * (1.0 / DN_D)
    zz = z_ref[...]
    o_ref[...] = (od * lax.rsqrt(ms + RMS_EPS) * par_ref[2:3, :] * (zz * jax.nn.sigmoid(zz))).astype(BF16)


def _deltanet(qkv, ab, z, convw, par, consts, bsz, seq):
    t = qkv.shape[0]
    ts = min(TS_DN, seq)
    nt = seq // ts
    hb = ts // HALO
    ea, eb, seg, tri, ones = consts
    row = lambda b, i: (b * nt + i, 0)
    const = lambda b, i: (0, 0)
    halo_map = lambda b, i: (jnp.maximum((b * nt + i) * hb - 1, 0), 0)
    cspec = lambda a: pl.BlockSpec(a.shape, const, pipeline_mode=pl.Buffered(1))
    big = lambda: pltpu.VMEM((ts, DN_W), F32)
    return pl.pallas_call(
        _dn_kernel,
        grid=(bsz, nt),
        in_specs=[pl.BlockSpec((ts, 3 * DN_W), row), pl.BlockSpec((HALO, 3 * DN_W), halo_map),
                  pl.BlockSpec((ts, LANES), row), pl.BlockSpec((ts, DN_W), row),
                  cspec(convw), cspec(par), cspec(ea), cspec(eb), cspec(seg), cspec(tri), cspec(ones)],
        out_specs=pl.BlockSpec((ts, DN_W), row),
        out_shape=jax.ShapeDtypeStruct((t, DN_W), BF16),
        scratch_shapes=[pltpu.VMEM((ts + HALO, 3 * DN_W), F32)] + [big() for _ in range(9)]
                       + [pltpu.VMEM((ts, DN_W), BF16),
                          pltpu.VMEM((ts // DN_CHUNK, N_PAIRS, PAIR, PAIR), BF16),
                          pltpu.VMEM((ts // DN_CHUNK, N_PAIRS, PAIR, PAIR), F32),
                          pltpu.VMEM((N_PAIRS, PAIR, PAIR), F32)],
        compiler_params=_cparams(("parallel", "arbitrary")),
        name="deltanet",
    )(qkv, qkv, ab, z, convw, par, ea, eb, seg, tri, ones)


def _df_kernel(sc_ref, q_ref, k_ref, vt_ref, bt_ref, w_ref, o_ref, sa_ref, sb_ref, acc_ref):
    qi = pl.program_id(2)
    tq = q_ref.shape[0]
    lane = lax.broadcasted_iota(jnp.int32, (tq, 2 * DF_D), 1)
    q = q_ref[...]
    zq = jnp.zeros_like(q)
    qs = (jnp.where(lane < DF_D, q, zq), jnp.where(lane < DF_D, zq, q))
    acc_ref[...] = jnp.zeros_like(acc_ref)

    def scores(j, s_ref):
        kk = k_ref[pl.ds(pl.multiple_of(j * tq, tq), tq), :]
        for mp in range(2):
            s_ref[mp] = _dot_nt(kk, qs[mp])

    def absorb(j, s_ref, bias, carry):
        vt = vt_ref[:, pl.ds(pl.multiple_of(j * tq, tq), tq)]
        out = []
        for mp in range(2):
            m_old, l_old = carry[mp]
            st = s_ref[mp]
            if bias is not None:
                st = st + bias
            m_new = jnp.maximum(m_old, jnp.max(st, axis=0, keepdims=True))
            alpha = jnp.exp2(m_old - m_new)
            pr = jnp.exp2(st - m_new)
            l_new = alpha * l_old + jnp.sum(pr, axis=0, keepdims=True)
            acc_ref[mp] = alpha * acc_ref[mp] + _dot(vt, pr.astype(BF16))
            out.append((m_new, l_new))
        return tuple(out)

    init1 = (jnp.full((1, tq), -jnp.inf, F32), jnp.zeros((1, tq), F32))
    carry = (init1, init1)
    n_far = jnp.maximum(qi - 1, 0)
    odd = n_far % 2

    @pl.when(qi == 0)
    def _():
        scores(0, sb_ref)

    @pl.when(jnp.logical_and(qi > 0, odd == 0))
    def _():
        scores(0, sa_ref)

    def odd_step(_, c):
        scores(0, sb_ref)
        scores(1, sa_ref)
        return absorb(0, sb_ref, None, c)

    carry = lax.fori_loop(0, odd, odd_step, carry)

    def pair_step(jp, c):
        j = odd + 2 * jp
        scores(j + 1, sb_ref)
        c = absorb(j, sa_ref, None, c)
        scores(j + 2, sa_ref)
        return absorb(j + 1, sb_ref, None, c)

    carry = lax.fori_loop(0, n_far // 2, pair_step, carry)

    def tail_step(j, c):
        scores(j + 1, sb_ref)
        c = absorb(j, sa_ref, bt_ref[1], c)
        return absorb(j + 1, sb_ref, bt_ref[0], c)

    carry = lax.fori_loop(n_far, qi, tail_step, carry)
    carry = lax.fori_loop(0, jnp.where(qi == 0, 1, 0), lambda _, c: absorb(0, sb_ref, bt_ref[0], c), carry)

    (_, l0), (_, l1) = carry
    ot = acc_ref[0] / l0 - sc_ref[0] * (acc_ref[1] / l1)
    ms = jnp.mean(ot * ot, axis=0, keepdims=True)
    ot = ot * lax.rsqrt(ms + RMS_EPS) * w_ref[...] * sc_ref[1]
    o_ref[...] = ot.T.astype(BF16)


def _attn_bias_tiles(rel_bias, seq, tq):
    rb = rel_bias.astype(F32)
    far = rb[_t5_causal_bucket(jnp.asarray(seq - 1, jnp.int32))]
    kj = jnp.arange(tq, dtype=jnp.int32)[:, None]
    qi = jnp.arange(tq, dtype=jnp.int32)[None, :]
    dist = jnp.stack([qi - kj, qi + tq - kj])
    onehot = (_t5_causal_bucket(jnp.maximum(dist, 0))[..., None]
              == jnp.arange(REL_BUCKETS, dtype=jnp.int32)).astype(F32)
    vals = (jnp.einsum('ntqb,bh->ntqh', onehot, rb, precision=lax.Precision.HIGHEST) - far) * LOG2E
    vals = jnp.where((dist >= 0)[..., None], vals, -jnp.inf)
    return jnp.transpose(vals, (3, 0, 1, 2))


def _diff_attention(qd, kd, vdt, btiles, subln_w, scalars, bsz, seq):
    t = qd.shape[0]
    tq = min(TQ_DF, seq)
    nq = seq // tq
    sbuf = pltpu.VMEM((2, tq, tq), F32)
    return pl.pallas_call(
        _df_kernel,
        grid_spec=pltpu.PrefetchScalarGridSpec(
            num_scalar_prefetch=1,
            grid=(bsz, DF_HEADS, nq),
            in_specs=[pl.BlockSpec((tq, 2 * DF_D), lambda b, h, i, sc: (b * nq + i, h)),
                      pl.BlockSpec((seq, 2 * DF_D), lambda b, h, i, sc: (b, h)),
                      pl.BlockSpec((None, 2 * DF_D, seq), lambda b, h, i, sc: (b, h, 0)),
                      pl.BlockSpec((None, 2, tq, tq), lambda b, h, i, sc: (h, 0, 0, 0)),
                      pl.BlockSpec((2 * DF_D, 1), lambda b, h, i, sc: (0, 0))],
            out_specs=pl.BlockSpec((tq, 2 * DF_D), lambda b, h, i, sc: (b * nq + i, h)),
            scratch_shapes=[sbuf, sbuf, pltpu.VMEM((2, 2 * DF_D, tq), F32)],
        ),
        out_shape=jax.ShapeDtypeStruct((t, DF_W), BF16),
        compiler_params=_cparams(("parallel", "parallel", "arbitrary")),
        name="diffattn",
    )(scalars, qd, kd, vdt, btiles, subln_w)


def _outproj_kernel(alpha, x_ref, a_ref, b_ref, w_ref, g_ref, bb_ref, o_ref):
    mix = _dot(a_ref[...], w_ref[0:DN_W, :]) + _dot(b_ref[...], w_ref[DN_W:, :])
    o_ref[...] = _layer_norm(alpha * x_ref[...] + mix, g_ref[...], bb_ref[...])


def _outproj_ln(x2, o_dn, o_df, w_out, g, b, alpha):
    t, d = x2.shape
    tm = min(TM_PROJ, t)
    row = lambda i: (i, 0)
    const = lambda i: (0, 0)
    return pl.pallas_call(
        functools.partial(_outproj_kernel, alpha),
        grid=(t // tm,),
        in_specs=[pl.BlockSpec((tm, d), row), pl.BlockSpec((tm, DN_W), row), pl.BlockSpec((tm, DF_W), row),
                  pl.BlockSpec(w_out.shape, const, pipeline_mode=pl.Buffered(1)),
                  pl.BlockSpec((1, d), const), pl.BlockSpec((1, d), const)],
        out_specs=pl.BlockSpec((tm, d), row),
        out_shape=jax.ShapeDtypeStruct((t, d), F32),
        compiler_params=_cparams(("parallel",)),
        name="outproj_ln",
    )(x2, o_dn, o_df, w_out, g, b)


def _swiglu_acc(xparts, wg_ref, wu_ref, wd_ref, acc_ref):
    d_ff = wg_ref.shape[-1]
    for c0 in range(0, d_ff, FF_CHUNK):
        cs = slice(c0, c0 + FF_CHUNK)
        hg = sum(_dot(xp, wg_ref[k0:k0 + xp.shape[1], cs]) for xp, k0 in xparts)
        hu = sum(_dot(xp, wu_ref[k0:k0 + xp.shape[1], cs]) for xp, k0 in xparts)
        hh = (hg * jax.nn.sigmoid(hg) * hu).astype(BF16)
        contrib = _dot(hh, wd_ref[cs, :])
        if c0 == 0:
            acc_ref[...] = contrib
        else:
            acc_ref[...] += contrib


def _ffn_kernel(alpha, x_ref, wg_ref, wu_ref, wd_ref, g_ref, b_ref, o_ref, acc_ref):
    x = x_ref[...]
    _swiglu_acc([(x.astype(BF16), 0)], wg_ref, wu_ref, wd_ref, acc_ref)
    o_ref[...] = _layer_norm(alpha * x + acc_ref[...], g_ref[...], b_ref[...])


def _ffn_ln(x2, wg, wu, wd, g, b, alpha):
    t, d = x2.shape
    tm = min(TM_FFN, t)
    row = lambda i: (i, 0)
    const = lambda i: (0, 0)
    wspec = lambda a: pl.BlockSpec(a.shape, const, pipeline_mode=pl.Buffered(1))
    return pl.pallas_call(
        functools.partial(_ffn_kernel, alpha),
        grid=(t // tm,),
        in_specs=[pl.BlockSpec((tm, d), row), wspec(wg), wspec(wu), wspec(wd),
                  pl.BlockSpec((1, d), const), pl.BlockSpec((1, d), const)],
        out_specs=pl.BlockSpec((tm, d), row),
        out_shape=jax.ShapeDtypeStruct((t, d), F32),
        scratch_shapes=[pltpu.VMEM((tm, d), F32)],
        compiler_params=_cparams(("parallel",)),
        name="ffn_ln",
    )(x2, wg, wu, wd, g, b)


def _pack_halves(x):
    h = x.shape[1] // 2
    hi = lax.bitcast_convert_type(x[:, :h].astype(BF16).astype(F32), jnp.uint32)
    lo = lax.bitcast_convert_type(x[:, h:].astype(BF16).astype(F32), jnp.uint32)
    return hi | (lo >> 16)


def _unpack_halves(p):
    a = lax.bitcast_convert_type(p & jnp.uint32(0xFFFF0000), F32)
    b = lax.bitcast_convert_type(p << 16, F32)
    return a, b


def _router_kernel(x_ref, wr_ref, o_ref, xp_ref):
    x = x_ref[...]
    xp_ref[...] = _pack_halves(x)
    logits = jnp.dot(x, wr_ref[...], preferred_element_type=F32, precision=lax.Precision.HIGHEST)
    lane = lax.broadcasted_iota(jnp.int32, logits.shape, 1)
    lg = jnp.where(lane < N_EXPERTS, logits, -jnp.inf)
    m1 = jnp.max(lg, axis=-1, keepdims=True)
    i1 = jnp.min(jnp.where(lg == m1, lane, LANES), axis=-1, keepdims=True)
    lg2 = jnp.where(lane == i1, -jnp.inf, lg)
    m2 = jnp.max(lg2, axis=-1, keepdims=True)
    i2 = jnp.min(jnp.where(lg2 == m2, lane, LANES), axis=-1, keepdims=True)
    e = jnp.exp(m2 - m1)
    g1 = 1.0 / (1.0 + e)
    g2 = e / (1.0 + e)
    out = jnp.where(lane == 0, i1.astype(F32), 0.0)
    out = jnp.where(lane == 1, i2.astype(F32), out)
    out = jnp.where(lane == 2, g1, out)
    out = jnp.where(lane == 3, g2, out)
    o_ref[...] = out


def _router(x2, wr):
    t, d = x2.shape
    tm = min(TM_PROJ, t)
    row = lambda i: (i, 0)
    return pl.pallas_call(
        _router_kernel,
        grid=(t // tm,),
        in_specs=[pl.BlockSpec((tm, d), row), pl.BlockSpec((d, LANES), lambda i: (0, 0))],
        out_specs=[pl.BlockSpec((tm, LANES), row), pl.BlockSpec((tm, d // 2), row)],
        out_shape=[jax.ShapeDtypeStruct((t, LANES), F32), jax.ShapeDtypeStruct((t, d // 2), jnp.uint32)],
        compiler_params=_cparams(("parallel",)),
        name="router",
    )(x2, wr)


def _sc_mesh():
    return plsc.VectorSubcoreMesh(core_axis_name="core", subcore_axis_name="subcore")


def _sc_index_rows(idx):
    win = idx.reshape(-1, SC_WIN)
    return jnp.concatenate([win, jnp.zeros((win.shape[0], SC_IDX_TILE - SC_WIN), idx.dtype)], axis=1)


def _sc_dispatch(xp, slot0, slot1, n_slot):
    t, c = xp.shape
    half = t // SC_WIN // 2
    idx_spec = pl.BlockSpec((1, SC_IDX_TILE), lambda cc, i: (cc * half + i, 0))

    @pl.kernel(out_type=jax.ShapeDtypeStruct((n_slot, c), xp.dtype), mesh=_sc_mesh(), scratch_types=[],
               name="moe_dispatch")
    def run(x_hbm, i0_hbm, i1_hbm, o_hbm):
        def body(x_vmem, i0_vmem, i1_vmem):
            pltpu.sync_copy(x_vmem, o_hbm.at[i0_vmem.at[0, pl.ds(0, SC_WIN)]])
            pltpu.sync_copy(x_vmem, o_hbm.at[i1_vmem.at[0, pl.ds(0, SC_WIN)]])

        pltpu.emit_pipeline(
            body,
            grid=(2, half),
            in_specs=[pl.BlockSpec((SC_WIN, c), lambda cc, i: (cc * half + i, 0)), idx_spec, idx_spec],
            out_specs=[],
            core_axis_name=("core", "subcore"),
            dimension_semantics=(pltpu.PARALLEL, pltpu.PARALLEL),
        )(x_hbm, i0_hbm, i1_hbm)

    return run(xp, _sc_index_rows(slot0), _sc_index_rows(slot1))


def _sc_gather(yp, idx):
    t = idx.shape[0]
    c = yp.shape[1]
    half = t // SC_WIN // 2

    @pl.kernel(out_type=jax.ShapeDtypeStruct((t, c), yp.dtype), mesh=_sc_mesh(), scratch_types=[],
               name="moe_gather")
    def run(y_hbm, i_hbm, o_hbm):
        def body(i_vmem, o_vmem):
            pltpu.sync_copy(y_hbm.at[i_vmem.at[0, pl.ds(0, SC_WIN)]], o_vmem)

        pltpu.emit_pipeline(
            body,
            grid=(2, half),
            in_specs=[pl.BlockSpec((1, SC_IDX_TILE), lambda cc, i: (cc * half + i, 0))],
            out_specs=[pl.BlockSpec((SC_WIN, c), lambda cc, i: (cc * half + i, 0))],
            core_axis_name=("core", "subcore"),
            dimension_semantics=(pltpu.PARALLEL, pltpu.PARALLEL),
        )(i_hbm, o_hbm)

    return run(yp, _sc_index_rows(idx))


def _expert_kernel(be_ref, nv_ref, x_ref, wg_ref, wu_ref, wd_ref, o_ref, acc_ref):
    i = pl.program_id(0)
    n_valid = nv_ref[i]

    @pl.when(n_valid > 0)
    def _():
        row = lax.broadcasted_iota(jnp.int32, x_ref.shape, 0)
        xa, xb = _unpack_halves(jnp.where(row < n_valid, x_ref[...], jnp.uint32(0)))
        h = xa.shape[1]
        _swiglu_acc([(xa.astype(BF16), 0), (xb.astype(BF16), h)], wg_ref, wu_ref, wd_ref, acc_ref)
        o_ref[...] = _pack_halves(acc_ref[...])

    @pl.when(n_valid <= 0)
    def _():
        o_ref[...] = jnp.zeros_like(o_ref)


def _experts(xbp, blk_e, n_valid, wg, wu, wd):
    n_slot, dh = xbp.shape
    d = 2 * dh
    n_blk = n_slot // MOE_BLK
    dff = wg.shape[-1]
    row = lambda i, be, nv: (i, 0)
    wmap = lambda i, be, nv: (be[i], 0, 0)
    return pl.pallas_call(
        _expert_kernel,
        grid_spec=pltpu.PrefetchScalarGridSpec(
            num_scalar_prefetch=2,
            grid=(n_blk,),
            in_specs=[pl.BlockSpec((MOE_BLK, dh), row),
                      pl.BlockSpec((None, d, dff), wmap, pipeline_mode=pl.Buffered(1)),
                      pl.BlockSpec((None, d, dff), wmap, pipeline_mode=pl.Buffered(1)),
                      pl.BlockSpec((None, dff, d), wmap, pipeline_mode=pl.Buffered(1))],
            out_specs=pl.BlockSpec((MOE_BLK, dh), row),
            scratch_shapes=[pltpu.VMEM((MOE_BLK, d), F32)],
        ),
        out_shape=jax.ShapeDtypeStruct((n_slot, dh), jnp.uint32),
        compiler_params=_cparams(("arbitrary",)),
        name="experts",
    )(blk_e, n_valid, xbp, wg, wu, wd)


def _combine_kernel(alpha, x_ref, y0_ref, y1_ref, r_ref, g_ref, b_ref, o_ref):
    r = r_ref[...]
    a0, b0 = _unpack_halves(y0_ref[...])
    a1, b1 = _unpack_halves(y1_ref[...])
    g0 = r[:, 2:3]
    g1 = r[:, 3:4]
    f = jnp.concatenate([g0 * a0 + g1 * a1, g0 * b0 + g1 * b1], axis=1)
    o_ref[...] = _layer_norm(alpha * x_ref[...] + f, g_ref[...], b_ref[...])


def _combine_ln(x2, y0, y1, route, g, b, alpha):
    t, d = x2.shape
    tm = min(TM_PROJ, t)
    row = lambda i: (i, 0)
    const = lambda i: (0, 0)
    return pl.pallas_call(
        functools.partial(_combine_kernel, alpha),
        grid=(t // tm,),
        in_specs=[pl.BlockSpec((tm, d), row), pl.BlockSpec((tm, d // 2), row), pl.BlockSpec((tm, d // 2), row),
                  pl.BlockSpec((tm, LANES), row), pl.BlockSpec((1, d), const), pl.BlockSpec((1, d), const)],
        out_specs=pl.BlockSpec((tm, d), row),
        out_shape=jax.ShapeDtypeStruct((t, d), F32),
        compiler_params=_cparams(("parallel",)),
        name="combine_ln",
    )(x2, y0, y1, route, g, b)


def _moe(x2, wr, wg, wu, wd, g, b, alpha):
    t, d = x2.shape
    route, xp = _router(x2, wr)
    top_idx = route[:, 0:2].astype(jnp.int32)
    n_asg = t * TOP_K
    flat_e = top_idx.reshape(n_asg)
    onehot = (flat_e[:, None] == jnp.arange(N_EXPERTS, dtype=jnp.int32)[None, :]).astype(jnp.int32)
    csum = jnp.cumsum(onehot, axis=0)
    rank = jnp.sum((csum - onehot) * onehot, axis=1)
    counts = csum[-1]
    padded = (counts + MOE_BLK - 1) // MOE_BLK * MOE_BLK
    pad_end = jnp.cumsum(padded)
    pad_start = pad_end - padded
    slot = (pad_start[flat_e] + rank).reshape(t, TOP_K)
    n_slot = -(-n_asg // MOE_BLK) * MOE_BLK + N_EXPERTS * MOE_BLK
    n_blk = n_slot // MOE_BLK
    blk_start = jnp.arange(n_blk, dtype=jnp.int32) * MOE_BLK
    blk_e = jnp.minimum(jnp.searchsorted(pad_end, blk_start, side='right'), N_EXPERTS - 1).astype(jnp.int32)
    n_valid = jnp.clip(pad_start[blk_e] + counts[blk_e] - blk_start, 0, MOE_BLK).astype(jnp.int32)
    slot0 = slot[:, 0]
    slot1 = slot[:, 1]
    xbp = _sc_dispatch(xp, slot0, slot1, n_slot)
    ybp = _experts(xbp, blk_e, n_valid, wg, wu, wd)
    return _combine_ln(x2, _sc_gather(ybp, slot0), _sc_gather(ybp, slot1), route, g, b, alpha)


def _t5_causal_bucket(dist):
    max_exact = REL_BUCKETS // 2
    d = jnp.maximum(dist, 1).astype(F32)
    large = max_exact + (jnp.log(d / max_exact) / math.log(REL_MAX_DIST / max_exact)
                         * (REL_BUCKETS - max_exact)).astype(jnp.int32)
    large = jnp.minimum(large, REL_BUCKETS - 1)
    return jnp.where(dist < max_exact, dist, large)


def _dn_constants(ts):
    lanes = np.arange(DN_W)
    ea = np.zeros((LANES, DN_W), np.float32)
    eb = np.zeros((LANES, DN_W), np.float32)
    ea[lanes // DN_D, lanes] = 1.0
    eb[DN_HEADS + lanes // DN_D, lanes] = 1.0
    seg = (lanes[:, None] // DN_D == lanes[None, :] // DN_D).astype(np.float32)
    r = np.arange(ts)
    same = r[:, None] // DN_CHUNK == r[None, :] // DN_CHUNK
    tri = (same & (r[:, None] >= r[None, :])).astype(np.float32)
    ones = same.astype(np.float32)
    return tuple(jnp.asarray(a, BF16) for a in (ea, eb, seg, tri, ones))


def kernel(x, w_in, w_out, conv_w, dn_a_log, dn_dt_bias, dn_norm_w, df_lambda, df_subln_w, rel_bias,
           ln1_g, ln1_b, ln2_g, ln2_b, ffn_w_gate, ffn_w_up, ffn_w_down, moe_router, moe_w_gate,
           moe_w_up, moe_w_down):
    bsz, seq, d = x.shape
    depth = w_in.shape[0]
    alpha = (2 * depth) ** 0.25
    t = bsz * seq
    tq = min(TQ_DF, seq)
    assert tq >= REL_MAX_DIST and seq % tq == 0 and seq % min(TS_DN, seq) == 0
    btiles = _attn_bias_tiles(rel_bias, seq, tq)

    dn_consts = _dn_constants(min(TS_DN, seq))
    c_dn = 3 * DN_W
    x2 = x.reshape(t, d)
    for layer in range(depth):
        lambda_init = 0.8 - 0.6 * math.exp(-0.3 * layer)
        wl = w_in[layer]
        n1 = c_dn + DN_W + 2 * DN_HEADS
        w1 = jnp.concatenate([wl[:, :n1], jnp.zeros((d, LANES - 2 * DN_HEADS), F32)], axis=1).astype(BF16)
        w2 = wl[:, n1:n1 + 2 * DF_W].astype(BF16)
        wvt = wl[:, n1 + 2 * DF_W:].T.astype(BF16)
        qkv, z, ab, qd, kd, vdt = _inproj(x2, w1, w2, wvt, bsz, seq)

        convw = jnp.concatenate([conv_w[layer], jnp.zeros((HALO - DN_CONV, c_dn), F32)], axis=0)
        par = jnp.zeros((HALO, DN_W), F32)
        par = par.at[0].set(jnp.repeat(dn_a_log[layer], DN_D))
        par = par.at[1].set(jnp.repeat(dn_dt_bias[layer], DN_D))
        par = par.at[2].set(jnp.tile(dn_norm_w[layer], DN_HEADS))
        o_dn = _deltanet(qkv, ab, z, convw, par, dn_consts, bsz, seq)

        lf = df_lambda[layer].astype(F32)
        lam = jnp.exp(jnp.sum(lf[0] * lf[1])) - jnp.exp(jnp.sum(lf[2] * lf[3])) + lambda_init
        scalars = jnp.stack([lam, jnp.asarray(1.0 - lambda_init, F32)]).astype(F32)
        o_df = _diff_attention(qd, kd, vdt, btiles, df_subln_w[layer].reshape(2 * DF_D, 1), scalars, bsz, seq)

        x2 = _outproj_ln(x2, o_dn, o_df, w_out[layer].astype(BF16), ln1_g[layer].reshape(1, d),
                         ln1_b[layer].reshape(1, d), alpha)
        i = layer // 2
        g2 = ln2_g[layer].reshape(1, d)
        b2 = ln2_b[layer].reshape(1, d)
        if layer % 2 == 0:
            x2 = _ffn_ln(x2, ffn_w_gate[i].astype(BF16), ffn_w_up[i].astype(BF16), ffn_w_down[i].astype(BF16),
                         g2, b2, alpha)
        else:
            wr = jnp.concatenate([moe_router[i], jnp.zeros((d, LANES - N_EXPERTS), F32)], axis=1)
            x2 = _moe(x2, wr, moe_w_gate[i].astype(BF16), moe_w_up[i].astype(BF16), moe_w_down[i].astype(BF16),
                      g2, b2, alpha)
    return x2.reshape(bsz, seq, d)
```

```python
import functools
import math

import jax
import jax.numpy as jnp
import numpy as np
from jax import lax
from jax.experimental import pallas as pl
from jax.experimental.pallas import tpu as pltpu
from jax.experimental.pallas import tpu_sc as plsc

F32 = jnp.float32
BF16 = jnp.bfloat16

DN_HEADS = 8
DN_D = 64
DN_CONV = 4
DN_CHUNK = 64
DN_W = DN_HEADS * DN_D
DF_HEADS = 4
DF_D = 64
DF_W = DF_HEADS * 2 * DF_D
REL_BUCKETS = 32
REL_MAX_DIST = 128
N_EXPERTS = 8
TOP_K = 2
MOE_BLK = 512
LN_EPS = 1e-5
RMS_EPS = 1e-6
LOG2E = math.log2(math.e)

LANES = 128
HALO = 8
PAIR = 2 * DN_D
N_PAIRS = DN_HEADS // 2
VMEM_LIMIT = 56 * 1024 * 1024

TM_PROJ = 512
TS_DN = 256
DN_UNROLL = 4
TQ_DF = 1024
TM_FFN = 512
FF_CHUNK = 256
SC_WIN = 64
SC_IDX_TILE = 128


def _cparams(sem):
    return pltpu.CompilerParams(dimension_semantics=sem, vmem_limit_bytes=VMEM_LIMIT)


def _dot(a, b):
    return jnp.dot(a, b, preferred_element_type=F32)


def _dot_nt(a, b):
    return lax.dot_general(a, b, (((1,), (1,)), ((), ())), preferred_element_type=F32)


def _dot_tn(a, b):
    return lax.dot_general(a, b, (((0,), (0,)), ((), ())), preferred_element_type=F32)


def _split(x):
    hi = x.astype(BF16)
    lo = (x - hi.astype(F32)).astype(BF16)
    return hi, lo


def _mm_xc(x, c):
    hi, lo = _split(x)
    return _dot(hi, c) + _dot(lo, c)


def _mm_cx(c, x):
    hi, lo = _split(x)
    return _dot(c, hi) + _dot(c, lo)


def _mm3(a, b):
    ah, al = _split(a)
    bh, bl = _split(b)
    return _dot(ah, bh) + _dot(ah, bl) + _dot(al, bh)


def _layer_norm(v, g, b):
    mu = jnp.mean(v, axis=-1, keepdims=True)
    d = v - mu
    var = jnp.mean(d * d, axis=-1, keepdims=True)
    return d * lax.rsqrt(var + LN_EPS) * g + b


def _inproj_kernel(nt, x_ref, halo_ref, w1_ref, w2_ref, wvt_ref, convw_ref,
                   qkv_ref, z_ref, ab_ref, qd_ref, kd_ref, vd_ref, xe_s):
    xb = x_ref[...].astype(BF16)
    c = 3 * DN_W
    tm = x_ref.shape[0]
    first = pl.program_id(0) % nt == 0
    halo = _dot(halo_ref[...].astype(BF16), w1_ref[:, 0:c])
    xe_s[0:HALO, :] = jnp.where(first, jnp.zeros_like(halo), halo)
    xe_s[HALO:, :] = _dot(xb, w1_ref[:, 0:c])
    z_ref[...] = _dot(xb, w1_ref[:, c:c + DN_W])
    ab_ref[...] = _dot(xb, w1_ref[:, c + DN_W:c + DN_W + LANES])
    qd_ref[...] = (_dot(xb, w2_ref[:, 0:DF_W]) * (DF_D ** -0.5 * LOG2E)).astype(BF16)
    kd_ref[...] = _dot(xb, w2_ref[:, DF_W:2 * DF_W]).astype(BF16)
    vd_ref[...] = _dot_nt(wvt_ref[...], xb).astype(BF16)
    for c0 in range(0, c, DN_W):
        xe = xe_s[:, c0:c0 + DN_W]
        y = convw_ref[DN_CONV - 1:DN_CONV, c0:c0 + DN_W] * xe[HALO:, :]
        for tap in range(DN_CONV - 1):
            y += convw_ref[tap:tap + 1, c0:c0 + DN_W] * pltpu.roll(xe, DN_CONV - 1 - tap, axis=0)[HALO:, :]
        qkv_ref[:, c0:c0 + DN_W] = y * jax.nn.sigmoid(y)


def _inproj(x2, w1, w2, wvt, convw, bsz, seq):
    t, d = x2.shape
    tm = min(TM_PROJ, seq)
    nt = seq // tm
    n1, n2 = w1.shape[1], w2.shape[1]
    row = lambda i: (i, 0)
    const = lambda i: (0, 0)
    return pl.pallas_call(
        functools.partial(_inproj_kernel, nt),
        grid=(t // tm,),
        in_specs=[pl.BlockSpec((tm, d), row),
                  pl.BlockSpec((HALO, d), lambda i: (jnp.maximum(i * (tm // HALO) - 1, 0), 0)),
                  pl.BlockSpec((d, n1), const, pipeline_mode=pl.Buffered(1)),
                  pl.BlockSpec((d, n2), const, pipeline_mode=pl.Buffered(1)),
                  pl.BlockSpec((DF_W, d), const, pipeline_mode=pl.Buffered(1)),
                  pl.BlockSpec(convw.shape, const, pipeline_mode=pl.Buffered(1))],
        out_specs=[pl.BlockSpec((tm, 3 * DN_W), row), pl.BlockSpec((tm, DN_W), row),
                   pl.BlockSpec((tm, LANES), row), pl.BlockSpec((tm, DF_W), row),
                   pl.BlockSpec((tm, DF_W), row),
                   pl.BlockSpec((None, DF_W, tm), lambda i: (i // nt, 0, i % nt))],
        out_shape=[jax.ShapeDtypeStruct((t, 3 * DN_W), F32), jax.ShapeDtypeStruct((t, DN_W), F32),
                   jax.ShapeDtypeStruct((t, LANES), F32), jax.ShapeDtypeStruct((t, DF_W), BF16),
                   jax.ShapeDtypeStruct((t, DF_W), BF16), jax.ShapeDtypeStruct((bsz, DF_W, seq), BF16)],
        scratch_shapes=[pltpu.VMEM((tm + HALO, 3 * DN_W), F32)],
        compiler_params=_cparams(("parallel",)),
        name="inproj",
    )(x2, x2, w1, w2, wvt, convw)


def _dn_kernel(qkv_ref, ab_ref, z_ref, par_ref, ea_ref, eb_ref, seg_ref, tri_ref, ones_ref, o_ref,
               q_s, k_s, kb_s, vb_s, gc_s, eg_s, kdec_s, gl_s, od_s, qe_s, m_s, n_s, state_s):
    i = pl.program_id(1)
    ts = qkv_ref.shape[0]
    n_chunks = ts // DN_CHUNK

    @pl.when(i == 0)
    def _():
        state_s[...] = jnp.zeros_like(state_s)

    q = qkv_ref[:, 0:DN_W]
    k = qkv_ref[:, DN_W:2 * DN_W]
    v = qkv_ref[:, 2 * DN_W:3 * DN_W]

    seg = seg_ref[...]
    q = q * lax.rsqrt(_dot((q * q).astype(BF16), seg) + 1e-6) * (DN_D ** -0.5)
    k = k * lax.rsqrt(_dot((k * k).astype(BF16), seg) + 1e-6)

    ab = ab_ref[...]
    xa = ab + par_ref[1:2, 0:LANES]
    softplus = jnp.maximum(xa, 0.0) + jnp.log(1.0 + jnp.exp(-jnp.abs(xa)))
    g = _mm_xc(-jnp.exp(par_ref[0:1, 0:LANES]) * softplus, ea_ref[...])
    beta = _mm_xc(jax.nn.sigmoid(ab), eb_ref[...])
    gc = _mm_cx(tri_ref[...], g)
    gl = _mm_cx(ones_ref[...], g)
    eg = jnp.exp(gc)
    kb = k * beta
    q_s[...] = q
    k_s[...] = k
    kb_s[...] = kb
    vb_s[...] = v * beta
    gc_s[...] = gc
    eg_s[...] = eg
    kdec_s[...] = k * jnp.exp(gl - gc)
    gl_s[...] = jnp.exp(gl)

    lane = lax.broadcasted_iota(jnp.int32, (DN_CHUNK, PAIR), 1)
    rowi = lax.broadcasted_iota(jnp.int32, (DN_CHUNK, PAIR), 0)
    colj = jnp.where(lane >= DN_D, lane - DN_D, lane)
    even = lane < DN_D
    eye2 = rowi == colj
    lower = rowi >= colj
    lane_b = lax.broadcasted_iota(jnp.int32, (PAIR, PAIR), 1)
    row_b = lax.broadcasted_iota(jnp.int32, (PAIR, PAIR), 0)
    bdmask = (lane_b < DN_D) == (row_b < DN_D)

    def bd(xm):
        z0 = jnp.zeros_like(xm)
        return jnp.concatenate([jnp.where(even, xm, z0), jnp.where(even, z0, xm)], axis=0)

    eye_f = jnp.where(eye2, 1.0, 0.0)

    def local_body(cc, carry):
        chains = [(cc * DN_UNROLL + dc, p) for dc in range(DN_UNROLL) for p in range(N_PAIRS)]
        n = len(chains)
        rows = [pl.ds(pl.multiple_of(c * DN_CHUNK, DN_CHUNK), DN_CHUNK) for c, _ in chains]
        cols = [slice(p * PAIR, (p + 1) * PAIR) for _, p in chains]
        ld = lambda ref, i: ref[rows[i], cols[i]]

        aq = [_dot_nt(jnp.concatenate([ld(kb_s, i), ld(q_s, i)], axis=0).astype(BF16),
                      bd(ld(k_s, i).astype(BF16))) for i in range(n)]
        a_qk, l_m = [], []
        for i in range(n):
            gcc = ld(gc_s, i)
            gcj = jnp.sum(jnp.where(eye2, gcc, 0.0), axis=0, keepdims=True)
            dec = jnp.where(lower, jnp.exp(jnp.minimum(gcc - gcj, 0.0)), 0.0)
            a_qk.append((aq[i][DN_CHUNK:, :] * dec).astype(BF16))
            l_m.append(jnp.where(eye2, 0.0, aq[i][:DN_CHUNK, :] * dec))

        lhl = [_split(m) for m in l_m]
        x0 = [eye_f - m for m in l_m]
        yb = [h for h, _ in lhl]
        for _ in range(5):
            yb = [_dot(yb[i], bd(yb[i])).astype(BF16) for i in range(n)]
            x0 = [x0[i] + _dot(x0[i].astype(BF16), bd(yb[i])) for i in range(n)]
        xhl = [_split(m) for m in x0]
        res = []
        for i in range(n):
            bxh = bd(xhl[i][0])
            lx = _dot(lhl[i][0], bxh) + _dot(lhl[i][0], bd(xhl[i][1])) + _dot(lhl[i][1], bxh)
            res.append((eye_f - x0[i] - lx).astype(BF16))
        t_m = [x0[i] + _dot(xhl[i][0], bd(res[i])) for i in range(n)]

        uw = []
        for i in range(n):
            th, tl = _split(t_m[i])
            kbg = (ld(kb_s, i) * ld(eg_s, i)).astype(BF16)
            rhs = jnp.concatenate([bd(ld(vb_s, i).astype(BF16)), bd(kbg)], axis=1)
            uw.append((_dot(th, rhs) + _dot(tl, rhs)).astype(BF16))
        qo = [_dot(a_qk[i], jnp.concatenate([bd(uw[i][:, PAIR:]), bd(uw[i][:, :PAIR])], axis=1))
              for i in range(n)]
        mn = [_dot_tn(ld(kdec_s, i).astype(BF16), jnp.concatenate([uw[i][:, PAIR:], uw[i][:, :PAIR]], axis=1))
              for i in range(n)]
        for i, (c, p) in enumerate(chains):
            qe_s[rows[i], cols[i]] = (ld(q_s, i) * ld(eg_s, i) - qo[i][:, :PAIR]).astype(BF16)
            od_s[rows[i], cols[i]] = qo[i][:, PAIR:]
            m_s[c, p] = jnp.where(bdmask, mn[i][:, :PAIR], 0.0).astype(BF16)
            n_s[c, p] = jnp.where(bdmask, mn[i][:, PAIR:], 0.0)
        return carry

    lax.fori_loop(0, n_chunks // DN_UNROLL, local_body, 0)

    for c in range(n_chunks):
        rows = slice(c * DN_CHUNK, (c + 1) * DN_CHUNK)
        for p in range(N_PAIRS):
            cols = slice(p * PAIR, (p + 1) * PAIR)
            st = state_s[p]
            r = _dot(jnp.concatenate([qe_s[rows, cols], m_s[c, p]], axis=0), st.astype(BF16))
            od_s[rows, cols] += r[:DN_CHUNK, :]
            state_s[p] = gl_s[c * DN_CHUNK:c * DN_CHUNK + 1, cols] * st - r[DN_CHUNK:, :] + n_s[c, p]

    od = od_s[...]
    ms = _dot((od * od).astype(BF16), seg) * (1.0 / DN_D)
    zz = z_ref[...]
    o_ref[...] = (od * lax.rsqrt(ms + RMS_EPS) * par_ref[2:3, :] * (zz * jax.nn.sigmoid(zz))).astype(BF16)


def _deltanet(qkv, ab, z, par, consts, bsz, seq):
    t = qkv.shape[0]
    ts = min(TS_DN, seq)
    nt = seq // ts
    ea, eb, seg, tri, ones = consts
    row = lambda b, i: (b * nt + i, 0)
    const = lambda b, i: (0, 0)
    cspec = lambda a: pl.BlockSpec(a.shape, const, pipeline_mode=pl.Buffered(1))
    big = lambda: pltpu.VMEM((ts, DN_W), F32)
    return pl.pallas_call(
        _dn_kernel,
        grid=(bsz, nt),
        in_specs=[pl.BlockSpec((ts, 3 * DN_W), row), pl.BlockSpec((ts, LANES), row), pl.BlockSpec((ts, DN_W), row),
                  cspec(par), cspec(ea), cspec(eb), cspec(seg), cspec(tri), cspec(ones)],
        out_specs=pl.BlockSpec((ts, DN_W), row),
        out_shape=jax.ShapeDtypeStruct((t, DN_W), BF16),
        scratch_shapes=[big() for _ in range(9)]
                       + [pltpu.VMEM((ts, DN_W), BF16),
                          pltpu.VMEM((ts // DN_CHUNK, N_PAIRS, PAIR, PAIR), BF16),
                          pltpu.VMEM((ts // DN_CHUNK, N_PAIRS, PAIR, PAIR), F32),
                          pltpu.VMEM((N_PAIRS, PAIR, PAIR), F32)],
        compiler_params=_cparams(("parallel", "arbitrary")),
        name="deltanet",
    )(qkv, ab, z, par, ea, eb, seg, tri, ones)


def _df_kernel(sc_ref, q_ref, k_ref, vt_ref, bt_ref, w_ref, o_ref, sa_ref, sb_ref, acc_ref):
    qi = pl.program_id(2)
    tq = q_ref.shape[0]
    lane = lax.broadcasted_iota(jnp.int32, (tq, 2 * DF_D), 1)
    q = q_ref[...]
    zq = jnp.zeros_like(q)
    qs = (jnp.where(lane < DF_D, q, zq), jnp.where(lane < DF_D, zq, q))
    acc_ref[...] = jnp.zeros_like(acc_ref)

    def scores(j, s_ref):
        kk = k_ref[pl.ds(pl.multiple_of(j * tq, tq), tq), :]
        for mp in range(2):
            s_ref[mp] = _dot_nt(kk, qs[mp])

    def absorb(j, s_ref, bias, carry):
        vt = vt_ref[:, pl.ds(pl.multiple_of(j * tq, tq), tq)]
        out = []
        for mp in range(2):
            m_old, l_old = carry[mp]
            st = s_ref[mp]
            if bias is not None:
                st = st + bias
            m_new = jnp.maximum(m_old, jnp.max(st, axis=0, keepdims=True))
            alpha = jnp.exp2(m_old - m_new)
            pr = jnp.exp2(st - m_new)
            l_new = alpha * l_old + jnp.sum(pr, axis=0, keepdims=True)
            acc_ref[mp] = alpha * acc_ref[mp] + _dot(vt, pr.astype(BF16))
            out.append((m_new, l_new))
        return tuple(out)

    init1 = (jnp.full((1, tq), -jnp.inf, F32), jnp.zeros((1, tq), F32))
    carry = (init1, init1)
    n_far = jnp.maximum(qi - 1, 0)
    odd = n_far % 2

    @pl.when(qi == 0)
    def _():
        scores(0, sb_ref)

    @pl.when(jnp.logical_and(qi > 0, odd == 0))
    def _():
        scores(0, sa_ref)

    def odd_step(_, c):
        scores(0, sb_ref)
        scores(1, sa_ref)
        return absorb(0, sb_ref, None, c)

    carry = lax.fori_loop(0, odd, odd_step, carry)

    def pair_step(jp, c):
        j = odd + 2 * jp
        scores(j + 1, sb_ref)
        c = absorb(j, sa_ref, None, c)
        scores(j + 2, sa_ref)
        return absorb(j + 1, sb_ref, None, c)

    carry = lax.fori_loop(0, n_far // 2, pair_step, carry)

    def tail_step(j, c):
        scores(j + 1, sb_ref)
        c = absorb(j, sa_ref, bt_ref[1], c)
        return absorb(j + 1, sb_ref, bt_ref[0], c)

    carry = lax.fori_loop(n_far, qi, tail_step, carry)
    carry = lax.fori_loop(0, jnp.where(qi == 0, 1, 0), lambda _, c: absorb(0, sb_ref, bt_ref[0], c), carry)

    (_, l0), (_, l1) = carry
    ot = acc_ref[0] / l0 - sc_ref[0] * (acc_ref[1] / l1)
    ms = jnp.mean(ot * ot, axis=0, keepdims=True)
    ot = ot * lax.rsqrt(ms + RMS_EPS) * w_ref[...] * sc_ref[1]
    o_ref[...] = ot.T.astype(BF16)


def _attn_bias_tiles(rel_bias, seq, tq):
    rb = rel_bias.astype(F32)
    far = rb[_t5_causal_bucket(jnp.asarray(seq - 1, jnp.int32))]
    kj = jnp.arange(tq, dtype=jnp.int32)[:, None]
    qi = jnp.arange(tq, dtype=jnp.int32)[None, :]
    dist = jnp.stack([qi - kj, qi + tq - kj])
    onehot = (_t5_causal_bucket(jnp.maximum(dist, 0))[..., None]
              == jnp.arange(REL_BUCKETS, dtype=jnp.int32)).astype(F32)
    vals = (jnp.einsum('ntqb,bh->ntqh', onehot, rb, precision=lax.Precision.HIGHEST) - far) * LOG2E
    vals = jnp.where((dist >= 0)[..., None], vals, -jnp.inf)
    return jnp.transpose(vals, (3, 0, 1, 2))


def _diff_attention(qd, kd, vdt, btiles, subln_w, scalars, bsz, seq):
    t = qd.shape[0]
    tq = min(TQ_DF, seq)
    nq = seq // tq
    sbuf = pltpu.VMEM((2, tq, tq), F32)
    return pl.pallas_call(
        _df_kernel,
        grid_spec=pltpu.PrefetchScalarGridSpec(
            num_scalar_prefetch=1,
            grid=(bsz, DF_HEADS, nq),
            in_specs=[pl.BlockSpec((tq, 2 * DF_D), lambda b, h, i, sc: (b * nq + i, h)),
                      pl.BlockSpec((seq, 2 * DF_D), lambda b, h, i, sc: (b, h)),
                      pl.BlockSpec((None, 2 * DF_D, seq), lambda b, h, i, sc: (b, h, 0)),
                      pl.BlockSpec((None, 2, tq, tq), lambda b, h, i, sc: (h, 0, 0, 0)),
                      pl.BlockSpec((2 * DF_D, 1), lambda b, h, i, sc: (0, 0))],
            out_specs=pl.BlockSpec((tq, 2 * DF_D), lambda b, h, i, sc: (b * nq + i, h)),
            scratch_shapes=[sbuf, sbuf, pltpu.VMEM((2, 2 * DF_D, tq), F32)],
        ),
        out_shape=jax.ShapeDtypeStruct((t, DF_W), BF16),
        compiler_params=_cparams(("parallel", "parallel", "arbitrary")),
        name="diffattn",
    )(scalars, qd, kd, vdt, btiles, subln_w)


def _outproj_kernel(alpha, x_ref, a_ref, b_ref, w_ref, g_ref, bb_ref, o_ref):
    mix = _dot(a_ref[...], w_ref[0:DN_W, :]) + _dot(b_ref[...], w_ref[DN_W:, :])
    o_ref[...] = _layer_norm(alpha * x_ref[...] + mix, g_ref[...], bb_ref[...])


def _outproj_ln(x2, o_dn, o_df, w_out, g, b, alpha):
    t, d = x2.shape
    tm = min(TM_PROJ, t)
    row = lambda i: (i, 0)
    const = lambda i: (0, 0)
    return pl.pallas_call(
        functools.partial(_outproj_kernel, alpha),
        grid=(t // tm,),
        in_specs=[pl.BlockSpec((tm, d), row), pl.BlockSpec((tm, DN_W), row), pl.BlockSpec((tm, DF_W), row),
                  pl.BlockSpec(w_out.shape, const, pipeline_mode=pl.Buffered(1)),
                  pl.BlockSpec((1, d), const), pl.BlockSpec((1, d), const)],
        out_specs=pl.BlockSpec((tm, d), row),
        out_shape=jax.ShapeDtypeStruct((t, d), F32),
        compiler_params=_cparams(("parallel",)),
        name="outproj_ln",
    )(x2, o_dn, o_df, w_out, g, b)


def _swiglu_acc(xparts, wg_ref, wu_ref, wd_ref, acc_ref):
    d_ff = wg_ref.shape[-1]
    for c0 in range(0, d_ff, FF_CHUNK):
        cs = slice(c0, c0 + FF_CHUNK)
        hg = sum(_dot(xp, wg_ref[k0:k0 + xp.shape[1], cs]) for xp, k0 in xparts)
        hu = sum(_dot(xp, wu_ref[k0:k0 + xp.shape[1], cs]) for xp, k0 in xparts)
        hh = (hg * jax.nn.sigmoid(hg) * hu).astype(BF16)
        contrib = _dot(hh, wd_ref[cs, :])
        if c0 == 0:
            acc_ref[...] = contrib
        else:
            acc_ref[...] += contrib


def _ffn_kernel(alpha, x_ref, wg_ref, wu_ref, wd_ref, g_ref, b_ref, o_ref, acc_ref):
    x = x_ref[...]
    _swiglu_acc([(x.astype(BF16), 0)], wg_ref, wu_ref, wd_ref, acc_ref)
    o_ref[...] = _layer_norm(alpha * x + acc_ref[...], g_ref[...], b_ref[...])


def _ffn_ln(x2, wg, wu, wd, g, b, alpha):
    t, d = x2.shape
    tm = min(TM_FFN, t)
    row = lambda i: (i, 0)
    const = lambda i: (0, 0)
    wspec = lambda a: pl.BlockSpec(a.shape, const, pipeline_mode=pl.Buffered(1))
    return pl.pallas_call(
        functools.partial(_ffn_kernel, alpha),
        grid=(t // tm,),
        in_specs=[pl.BlockSpec((tm, d), row), wspec(wg), wspec(wu), wspec(wd),
                  pl.BlockSpec((1, d), const), pl.BlockSpec((1, d), const)],
        out_specs=pl.BlockSpec((tm, d), row),
        out_shape=jax.ShapeDtypeStruct((t, d), F32),
        scratch_shapes=[pltpu.VMEM((tm, d), F32)],
        compiler_params=_cparams(("parallel",)),
        name="ffn_ln",
    )(x2, wg, wu, wd, g, b)


def _pack_halves(x):
    h = x.shape[1] // 2
    hi = lax.bitcast_convert_type(x[:, :h].astype(BF16).astype(F32), jnp.uint32)
    lo = lax.bitcast_convert_type(x[:, h:].astype(BF16).astype(F32), jnp.uint32)
    return hi | (lo >> 16)


def _unpack_halves(p):
    a = lax.bitcast_convert_type(p & jnp.uint32(0xFFFF0000), F32)
    b = lax.bitcast_convert_type(p << 16, F32)
    return a, b


def _router_kernel(x_ref, wr_ref, o_ref, xp_ref):
    x = x_ref[...]
    xp_ref[...] = _pack_halves(x)
    logits = jnp.dot(x, wr_ref[...], preferred_element_type=F32, precision=lax.Precision.HIGHEST)
    lane = lax.broadcasted_iota(jnp.int32, logits.shape, 1)
    lg = jnp.where(lane < N_EXPERTS, logits, -jnp.inf)
    m1 = jnp.max(lg, axis=-1, keepdims=True)
    i1 = jnp.min(jnp.where(lg == m1, lane, LANES), axis=-1, keepdims=True)
    lg2 = jnp.where(lane == i1, -jnp.inf, lg)
    m2 = jnp.max(lg2, axis=-1, keepdims=True)
    i2 = jnp.min(jnp.where(lg2 == m2, lane, LANES), axis=-1, keepdims=True)
    e = jnp.exp(m2 - m1)
    g1 = 1.0 / (1.0 + e)
    g2 = e / (1.0 + e)
    out = jnp.where(lane == 0, i1.astype(F32), 0.0)
    out = jnp.where(lane == 1, i2.astype(F32), out)
    out = jnp.where(lane == 2, g1, out)
    out = jnp.where(lane == 3, g2, out)
    o_ref[...] = out


def _router(x2, wr):
    t, d = x2.shape
    tm = min(TM_PROJ, t)
    row = lambda i: (i, 0)
    return pl.pallas_call(
        _router_kernel,
        grid=(t // tm,),
        in_specs=[pl.BlockSpec((tm, d), row), pl.BlockSpec((d, LANES), lambda i: (0, 0))],
        out_specs=[pl.BlockSpec((tm, LANES), row), pl.BlockSpec((tm, d // 2), row)],
        out_shape=[jax.ShapeDtypeStruct((t, LANES), F32), jax.ShapeDtypeStruct((t, d // 2), jnp.uint32)],
        compiler_params=_cparams(("parallel",)),
        name="router",
    )(x2, wr)


def _sc_mesh():
    return plsc.VectorSubcoreMesh(core_axis_name="core", subcore_axis_name="subcore")


def _sc_index_rows(idx):
    win = idx.reshape(-1, SC_WIN)
    return jnp.concatenate([win, jnp.zeros((win.shape[0], SC_IDX_TILE - SC_WIN), idx.dtype)], axis=1)


def _sc_dispatch(xp, slot0, slot1, n_slot):
    t, c = xp.shape
    half = t // SC_WIN // 2
    idx_spec = pl.BlockSpec((1, SC_IDX_TILE), lambda cc, i: (cc * half + i, 0))

    @pl.kernel(out_type=jax.ShapeDtypeStruct((n_slot, c), xp.dtype), mesh=_sc_mesh(), scratch_types=[],
               name="moe_dispatch")
    def run(x_hbm, i0_hbm, i1_hbm, o_hbm):
        def body(x_vmem, i0_vmem, i1_vmem):
            pltpu.sync_copy(x_vmem, o_hbm.at[i0_vmem.at[0, pl.ds(0, SC_WIN)]])
            pltpu.sync_copy(x_vmem, o_hbm.at[i1_vmem.at[0, pl.ds(0, SC_WIN)]])

        pltpu.emit_pipeline(
            body,
            grid=(2, half),
            in_specs=[pl.BlockSpec((SC_WIN, c), lambda cc, i: (cc * half + i, 0)), idx_spec, idx_spec],
            out_specs=[],
            core_axis_name=("core", "subcore"),
            dimension_semantics=(pltpu.PARALLEL, pltpu.PARALLEL),
        )(x_hbm, i0_hbm, i1_hbm)

    return run(xp, _sc_index_rows(slot0), _sc_index_rows(slot1))


def _sc_gather(yp, idx):
    t = idx.shape[0]
    c = yp.shape[1]
    half = t // SC_WIN // 2

    @pl.kernel(out_type=jax.ShapeDtypeStruct((t, c), yp.dtype), mesh=_sc_mesh(), scratch_types=[],
               name="moe_gather")
    def run(y_hbm, i_hbm, o_hbm):
        def body(i_vmem, o_vmem):
            pltpu.sync_copy(y_hbm.at[i_vmem.at[0, pl.ds(0, SC_WIN)]], o_vmem)

        pltpu.emit_pipeline(
            body,
            grid=(2, half),
            in_specs=[pl.BlockSpec((1, SC_IDX_TILE), lambda cc, i: (cc * half + i, 0))],
            out_specs=[pl.BlockSpec((SC_WIN, c), lambda cc, i: (cc * half + i, 0))],
            core_axis_name=("core", "subcore"),
            dimension_semantics=(pltpu.PARALLEL, pltpu.PARALLEL),
        )(i_hbm, o_hbm)

    return run(yp, _sc_index_rows(idx))


def _expert_kernel(be_ref, nv_ref, x_ref, wg_ref, wu_ref, wd_ref, o_ref, acc_ref):
    i = pl.program_id(0)
    n_valid = nv_ref[i]

    @pl.when(n_valid > 0)
    def _():
        row = lax.broadcasted_iota(jnp.int32, x_ref.shape, 0)
        xa, xb = _unpack_halves(jnp.where(row < n_valid, x_ref[...], jnp.uint32(0)))
        h = xa.shape[1]
        _swiglu_acc([(xa.astype(BF16), 0), (xb.astype(BF16), h)], wg_ref, wu_ref, wd_ref, acc_ref)
        o_ref[...] = _pack_halves(acc_ref[...])

    @pl.when(n_valid <= 0)
    def _():
        o_ref[...] = jnp.zeros_like(o_ref)


def _experts(xbp, blk_e, n_valid, wg, wu, wd):
    n_slot, dh = xbp.shape
    d = 2 * dh
    n_blk = n_slot // MOE_BLK
    dff = wg.shape[-1]
    row = lambda i, be, nv: (i, 0)
    wmap = lambda i, be, nv: (be[i], 0, 0)
    return pl.pallas_call(
        _expert_kernel,
        grid_spec=pltpu.PrefetchScalarGridSpec(
            num_scalar_prefetch=2,
            grid=(n_blk,),
            in_specs=[pl.BlockSpec((MOE_BLK, dh), row),
                      pl.BlockSpec((None, d, dff), wmap, pipeline_mode=pl.Buffered(1)),
                      pl.BlockSpec((None, d, dff), wmap, pipeline_mode=pl.Buffered(1)),
                      pl.BlockSpec((None, dff, d), wmap, pipeline_mode=pl.Buffered(1))],
            out_specs=pl.BlockSpec((MOE_BLK, dh), row),
            scratch_shapes=[pltpu.VMEM((MOE_BLK, d), F32)],
        ),
        out_shape=jax.ShapeDtypeStruct((n_slot, dh), jnp.uint32),
        compiler_params=_cparams(("arbitrary",)),
        name="experts",
    )(blk_e, n_valid, xbp, wg, wu, wd)


def _combine_kernel(alpha, x_ref, y0_ref, y1_ref, r_ref, g_ref, b_ref, o_ref):
    r = r_ref[...]
    a0, b0 = _unpack_halves(y0_ref[...])
    a1, b1 = _unpack_halves(y1_ref[...])
    g0 = r[:, 2:3]
    g1 = r[:, 3:4]
    f = jnp.concatenate([g0 * a0 + g1 * a1, g0 * b0 + g1 * b1], axis=1)
    o_ref[...] = _layer_norm(alpha * x_ref[...] + f, g_ref[...], b_ref[...])


def _combine_ln(x2, y0, y1, route, g, b, alpha):
    t, d = x2.shape
    tm = min(TM_PROJ, t)
    row = lambda i: (i, 0)
    const = lambda i: (0, 0)
    return pl.pallas_call(
        functools.partial(_combine_kernel, alpha),
        grid=(t // tm,),
        in_specs=[pl.BlockSpec((tm, d), row), pl.BlockSpec((tm, d // 2), row), pl.BlockSpec((tm, d // 2), row),
                  pl.BlockSpec((tm, LANES), row), pl.BlockSpec((1, d), const), pl.BlockSpec((1, d), const)],
        out_specs=pl.BlockSpec((tm, d), row),
        out_shape=jax.ShapeDtypeStruct((t, d), F32),
        compiler_params=_cparams(("parallel",)),
        name="combine_ln",
    )(x2, y0, y1, route, g, b)


def _moe(x2, wr, wg, wu, wd, g, b, alpha):
    t, d = x2.shape
    route, xp = _router(x2, wr)
    top_idx = route[:, 0:2].astype(jnp.int32)
    n_asg = t * TOP_K
    flat_e = top_idx.reshape(n_asg)
    onehot = (flat_e[:, None] == jnp.arange(N_EXPERTS, dtype=jnp.int32)[None, :]).astype(jnp.int32)
    csum = jnp.cumsum(onehot, axis=0)
    rank = jnp.sum((csum - onehot) * onehot, axis=1)
    counts = csum[-1]
    padded = (counts + MOE_BLK - 1) // MOE_BLK * MOE_BLK
    pad_end = jnp.cumsum(padded)
    pad_start = pad_end - padded
    slot = (pad_start[flat_e] + rank).reshape(t, TOP_K)
    n_slot = -(-n_asg // MOE_BLK) * MOE_BLK + N_EXPERTS * MOE_BLK
    n_blk = n_slot // MOE_BLK
    blk_start = jnp.arange(n_blk, dtype=jnp.int32) * MOE_BLK
    blk_e = jnp.minimum(jnp.searchsorted(pad_end, blk_start, side='right'), N_EXPERTS - 1).astype(jnp.int32)
    n_valid = jnp.clip(pad_start[blk_e] + counts[blk_e] - blk_start, 0, MOE_BLK).astype(jnp.int32)
    slot0 = slot[:, 0]
    slot1 = slot[:, 1]
    xbp = _sc_dispatch(xp, slot0, slot1, n_slot)
    ybp = _experts(xbp, blk_e, n_valid, wg, wu, wd)
    return _combine_ln(x2, _sc_gather(ybp, slot0), _sc_gather(ybp, slot1), route, g, b, alpha)


def _t5_causal_bucket(dist):
    max_exact = REL_BUCKETS // 2
    d = jnp.maximum(dist, 1).astype(F32)
    large = max_exact + (jnp.log(d / max_exact) / math.log(REL_MAX_DIST / max_exact)
                         * (REL_BUCKETS - max_exact)).astype(jnp.int32)
    large = jnp.minimum(large, REL_BUCKETS - 1)
    return jnp.where(dist < max_exact, dist, large)


def _dn_constants(ts):
    lanes = np.arange(DN_W)
    ea = np.zeros((LANES, DN_W), np.float32)
    eb = np.zeros((LANES, DN_W), np.float32)
    ea[lanes // DN_D, lanes] = 1.0
    eb[DN_HEADS + lanes // DN_D, lanes] = 1.0
    seg = (lanes[:, None] // DN_D == lanes[None, :] // DN_D).astype(np.float32)
    r = np.arange(ts)
    same = r[:, None] // DN_CHUNK == r[None, :] // DN_CHUNK
    tri = (same & (r[:, None] >= r[None, :])).astype(np.float32)
    ones = same.astype(np.float32)
    return tuple(jnp.asarray(a, BF16) for a in (ea, eb, seg, tri, ones))


def kernel(x, w_in, w_out, conv_w, dn_a_log, dn_dt_bias, dn_norm_w, df_lambda, df_subln_w, rel_bias,
           ln1_g, ln1_b, ln2_g, ln2_b, ffn_w_gate, ffn_w_up, ffn_w_down, moe_router, moe_w_gate,
           moe_w_up, moe_w_down):
    bsz, seq, d = x.shape
    depth = w_in.shape[0]
    alpha = (2 * depth) ** 0.25
    t = bsz * seq
    tq = min(TQ_DF, seq)
    assert tq >= REL_MAX_DIST and seq % tq == 0 and seq % min(TS_DN, seq) == 0
    btiles = _attn_bias_tiles(rel_bias, seq, tq)

    dn_consts = _dn_constants(min(TS_DN, seq))
    c_dn = 3 * DN_W
    x2 = x.reshape(t, d)
    for layer in range(depth):
        lambda_init = 0.8 - 0.6 * math.exp(-0.3 * layer)
        wl = w_in[layer]
        n1 = c_dn + DN_W + 2 * DN_HEADS
        w1 = jnp.concatenate([wl[:, :n1], jnp.zeros((d, LANES - 2 * DN_HEADS), F32)], axis=1).astype(BF16)
        w2 = wl[:, n1:n1 + 2 * DF_W].astype(BF16)
        wvt = wl[:, n1 + 2 * DF_W:].T.astype(BF16)
        convw = jnp.concatenate([conv_w[layer], jnp.zeros((HALO - DN_CONV, c_dn), F32)], axis=0)
        qkv, z, ab, qd, kd, vdt = _inproj(x2, w1, w2, wvt, convw, bsz, seq)

        par = jnp.zeros((HALO, DN_W), F32)
        par = par.at[0, 0:DN_HEADS].set(dn_a_log[layer])
        par = par.at[1, 0:DN_HEADS].set(dn_dt_bias[layer])
        par = par.at[2].set(jnp.tile(dn_norm_w[layer], DN_HEADS))
        o_dn = _deltanet(qkv, ab, z, par, dn_consts, bsz, seq)

        lf = df_lambda[layer].astype(F32)
        lam = jnp.exp(jnp.sum(lf[0] * lf[1])) - jnp.exp(jnp.sum(lf[2] * lf[3])) + lambda_init
        scalars = jnp.stack([lam, jnp.asarray(1.0 - lambda_init, F32)]).astype(F32)
        o_df = _diff_attention(qd, kd, vdt, btiles, df_subln_w[layer].reshape(2 * DF_D, 1), scalars, bsz, seq)

        x2 = _outproj_ln(x2, o_dn, o_df, w_out[layer].astype(BF16), ln1_g[layer].reshape(1, d),
                         ln1_b[layer].reshape(1, d), alpha)
        i = layer // 2
        g2 = ln2_g[layer].reshape(1, d)
        b2 = ln2_b[layer].reshape(1, d)
        if layer % 2 == 0:
            x2 = _ffn_ln(x2, ffn_w_gate[i].astype(BF16), ffn_w_up[i].astype(BF16), ffn_w_down[i].astype(BF16),
                         g2, b2, alpha)
        else:
            wr = jnp.concatenate([moe_router[i], jnp.zeros((d, LANES - N_EXPERTS), F32)], axis=1)
            x2 = _moe(x2, wr, moe_w_gate[i].astype(BF16), moe_w_up[i].astype(BF16), moe_w_down[i].astype(BF16),
                      g2, b2, alpha)
    return x2.reshape(bsz, seq, d)
```

```python
import functools
import math

import jax
import jax.numpy as jnp
import numpy as np
from jax import lax
from jax.experimental import pallas as pl
from jax.experimental.pallas import tpu as pltpu
from jax.experimental.pallas import tpu_sc as plsc

F32 = jnp.float32
BF16 = jnp.bfloat16

DN_HEADS = 8
DN_D = 64
DN_CONV = 4
DN_CHUNK = 64
DN_W = DN_HEADS * DN_D
DF_HEADS = 4
DF_D = 64
DF_W = DF_HEADS * 2 * DF_D
REL_BUCKETS = 32
REL_MAX_DIST = 128
N_EXPERTS = 8
TOP_K = 2
MOE_BLK = 512
LN_EPS = 1e-5
RMS_EPS = 1e-6
LOG2E = math.log2(math.e)

LANES = 128
HALO = 8
PAIR = 2 * DN_D
N_PAIRS = DN_HEADS // 2
VMEM_LIMIT = 56 * 1024 * 1024

TM_PROJ = 512
TS_DN = 256
DN_UNROLL = 4
TQ_DF = 1024
TM_FFN = 512
FF_CHUNK = 256
SC_WIN = 64
SC_IDX_TILE = 128


def _cparams(sem):
    return pltpu.CompilerParams(dimension_semantics=sem, vmem_limit_bytes=VMEM_LIMIT)


def _dot(a, b):
    return jnp.dot(a, b, preferred_element_type=F32)


def _dot_nt(a, b):
    return lax.dot_general(a, b, (((1,), (1,)), ((), ())), preferred_element_type=F32)


def _dot_tn(a, b):
    return lax.dot_general(a, b, (((0,), (0,)), ((), ())), preferred_element_type=F32)


def _split(x):
    hi = x.astype(BF16)
    lo = (x - hi.astype(F32)).astype(BF16)
    return hi, lo


def _mm_xc(x, c):
    hi, lo = _split(x)
    return _dot(hi, c) + _dot(lo, c)


def _mm_cx(c, x):
    hi, lo = _split(x)
    return _dot(c, hi) + _dot(c, lo)


def _mm3(a, b):
    ah, al = _split(a)
    bh, bl = _split(b)
    return _dot(ah, bh) + _dot(ah, bl) + _dot(al, bh)


def _layer_norm(v, g, b):
    mu = jnp.mean(v, axis=-1, keepdims=True)
    d = v - mu
    var = jnp.mean(d * d, axis=-1, keepdims=True)
    return d * lax.rsqrt(var + LN_EPS) * g + b


def _inproj_kernel(x_ref, w1_ref, w2_ref, wvt_ref, qkv_ref, z_ref, ab_ref, qd_ref, kd_ref, vd_ref):
    xb = x_ref[...].astype(BF16)
    c = 3 * DN_W
    qkv_ref[...] = _dot(xb, w1_ref[:, 0:c])
    z_ref[...] = _dot(xb, w1_ref[:, c:c + DN_W])
    ab_ref[...] = _dot(xb, w1_ref[:, c + DN_W:c + DN_W + LANES])
    qd_ref[...] = (_dot(xb, w2_ref[:, 0:DF_W]) * (DF_D ** -0.5 * LOG2E)).astype(BF16)
    kd_ref[...] = _dot(xb, w2_ref[:, DF_W:2 * DF_W]).astype(BF16)
    vd_ref[...] = _dot_nt(wvt_ref[...], xb).astype(BF16)


def _inproj(x2, w1, w2, wvt, bsz, seq):
    t, d = x2.shape
    tm = min(TM_PROJ, seq)
    nt = seq // tm
    n1, n2 = w1.shape[1], w2.shape[1]
    row = lambda i: (i, 0)
    const = lambda i: (0, 0)
    return pl.pallas_call(
        _inproj_kernel,
        grid=(t // tm,),
        in_specs=[pl.BlockSpec((tm, d), row),
                  pl.BlockSpec((d, n1), const, pipeline_mode=pl.Buffered(1)),
                  pl.BlockSpec((d, n2), const, pipeline_mode=pl.Buffered(1)),
                  pl.BlockSpec((DF_W, d), const, pipeline_mode=pl.Buffered(1))],
        out_specs=[pl.BlockSpec((tm, 3 * DN_W), row), pl.BlockSpec((tm, DN_W), row),
                   pl.BlockSpec((tm, LANES), row), pl.BlockSpec((tm, DF_W), row),
                   pl.BlockSpec((tm, DF_W), row),
                   pl.BlockSpec((None, DF_W, tm), lambda i: (i // nt, 0, i % nt))],
        out_shape=[jax.ShapeDtypeStruct((t, 3 * DN_W), F32), jax.ShapeDtypeStruct((t, DN_W), F32),
                   jax.ShapeDtypeStruct((t, LANES), F32), jax.ShapeDtypeStruct((t, DF_W), BF16),
                   jax.ShapeDtypeStruct((t, DF_W), BF16), jax.ShapeDtypeStruct((bsz, DF_W, seq), BF16)],
        compiler_params=_cparams(("parallel",)),
        name="inproj",
    )(x2, w1, w2, wvt)


def _dn_kernel(qkv_ref, halo_ref, ab_ref, z_ref, convw_ref, par_ref, ea_ref, eb_ref, seg_ref,
               tri_ref, ones_ref, o_ref,
               xe_s, q_s, k_s, kb_s, vb_s, gc_s, eg_s, kdec_s, gl_s, od_s, qe_s, m_s, n_s, state_s):
    i = pl.program_id(1)
    ts = qkv_ref.shape[0]
    n_chunks = ts // DN_CHUNK

    @pl.when(i == 0)
    def _():
        state_s[...] = jnp.zeros_like(state_s)

    halo = halo_ref[...]
    xe_s[0:HALO, :] = jnp.where(i > 0, halo, jnp.zeros_like(halo))
    xe_s[HALO:, :] = qkv_ref[...]
    xe = xe_s[...]
    y = convw_ref[DN_CONV - 1:DN_CONV, :] * xe[HALO:, :]
    for tap in range(DN_CONV - 1):
        y += convw_ref[tap:tap + 1, :] * pltpu.roll(xe, DN_CONV - 1 - tap, axis=0)[HALO:, :]
    y = y * jax.nn.sigmoid(y)
    q = y[:, 0:DN_W]
    k = y[:, DN_W:2 * DN_W]
    v = y[:, 2 * DN_W:3 * DN_W]

    seg = seg_ref[...]
    q = q * lax.rsqrt(_dot((q * q).astype(BF16), seg) + 1e-6) * (DN_D ** -0.5)
    k = k * lax.rsqrt(_dot((k * k).astype(BF16), seg) + 1e-6)

    ab = ab_ref[...]
    xa = ab + par_ref[1:2, 0:LANES]
    softplus = jnp.maximum(xa, 0.0) + jnp.log(1.0 + jnp.exp(-jnp.abs(xa)))
    g = _mm_xc(-jnp.exp(par_ref[0:1, 0:LANES]) * softplus, ea_ref[...])
    beta = _mm_xc(jax.nn.sigmoid(ab), eb_ref[...])
    gc = _mm_cx(tri_ref[...], g)
    gl = _mm_cx(ones_ref[...], g)
    eg = jnp.exp(gc)
    kb = k * beta
    q_s[...] = q
    k_s[...] = k
    kb_s[...] = kb
    vb_s[...] = v * beta
    gc_s[...] = gc
    eg_s[...] = eg
    kdec_s[...] = k * jnp.exp(gl - gc)
    gl_s[...] = jnp.exp(gl)

    lane = lax.broadcasted_iota(jnp.int32, (DN_CHUNK, PAIR), 1)
    rowi = lax.broadcasted_iota(jnp.int32, (DN_CHUNK, PAIR), 0)
    colj = jnp.where(lane >= DN_D, lane - DN_D, lane)
    even = lane < DN_D
    eye2 = rowi == colj
    lower = rowi >= colj
    lane_b = lax.broadcasted_iota(jnp.int32, (PAIR, PAIR), 1)
    row_b = lax.broadcasted_iota(jnp.int32, (PAIR, PAIR), 0)
    bdmask = (lane_b < DN_D) == (row_b < DN_D)

    def bd(xm):
        z0 = jnp.zeros_like(xm)
        return jnp.concatenate([jnp.where(even, xm, z0), jnp.where(even, z0, xm)], axis=0)

    eye_f = jnp.where(eye2, 1.0, 0.0)

    def local_body(cc, carry):
        chains = [(cc * DN_UNROLL + dc, p) for dc in range(DN_UNROLL) for p in range(N_PAIRS)]
        n = len(chains)
        rows = [pl.ds(pl.multiple_of(c * DN_CHUNK, DN_CHUNK), DN_CHUNK) for c, _ in chains]
        cols = [slice(p * PAIR, (p + 1) * PAIR) for _, p in chains]
        ld = lambda ref, i: ref[rows[i], cols[i]]

        aq = [_dot_nt(jnp.concatenate([ld(kb_s, i), ld(q_s, i)], axis=0).astype(BF16),
                      bd(ld(k_s, i).astype(BF16))) for i in range(n)]
        a_qk, l_m = [], []
        for i in range(n):
            gcc = ld(gc_s, i)
            gcj = jnp.sum(jnp.where(eye2, gcc, 0.0), axis=0, keepdims=True)
            dec = jnp.where(lower, jnp.exp(jnp.minimum(gcc - gcj, 0.0)), 0.0)
            a_qk.append((aq[i][DN_CHUNK:, :] * dec).astype(BF16))
            l_m.append(jnp.where(eye2, 0.0, aq[i][:DN_CHUNK, :] * dec))

        lhl = [_split(m) for m in l_m]
        x0 = [eye_f - m for m in l_m]
        yb = [h for h, _ in lhl]
        for _ in range(5):
            yb = [_dot(yb[i], bd(yb[i])).astype(BF16) for i in range(n)]
            x0 = [x0[i] + _dot(x0[i].astype(BF16), bd(yb[i])) for i in range(n)]
        xhl = [_split(m) for m in x0]
        res = []
        for i in range(n):
            bxh = bd(xhl[i][0])
            lx = _dot(lhl[i][0], bxh) + _dot(lhl[i][0], bd(xhl[i][1])) + _dot(lhl[i][1], bxh)
            res.append((eye_f - x0[i] - lx).astype(BF16))
        t_m = [x0[i] + _dot(xhl[i][0], bd(res[i])) for i in range(n)]

        uw = []
        for i in range(n):
            th, tl = _split(t_m[i])
            kbg = (ld(kb_s, i) * ld(eg_s, i)).astype(BF16)
            rhs = jnp.concatenate([bd(ld(vb_s, i).astype(BF16)), bd(kbg)], axis=1)
            uw.append((_dot(th, rhs) + _dot(tl, rhs)).astype(BF16))
        qo = [_dot(a_qk[i], jnp.concatenate([bd(uw[i][:, PAIR:]), bd(uw[i][:, :PAIR])], axis=1))
              for i in range(n)]
        mn = [_dot_tn(ld(kdec_s, i).astype(BF16), jnp.concatenate([uw[i][:, PAIR:], uw[i][:, :PAIR]], axis=1))
              for i in range(n)]
        for i, (c, p) in enumerate(chains):
            qe_s[rows[i], cols[i]] = (ld(q_s, i) * ld(eg_s, i) - qo[i][:, :PAIR]).astype(BF16)
            od_s[rows[i], cols[i]] = qo[i][:, PAIR:]
            m_s[c, p] = jnp.where(bdmask, mn[i][:, :PAIR], 0.0).astype(BF16)
            n_s[c, p] = jnp.where(bdmask, mn[i][:, PAIR:], 0.0)
        return carry

    lax.fori_loop(0, n_chunks // DN_UNROLL, local_body, 0)

    for c in range(n_chunks):
        rows = slice(c * DN_CHUNK, (c + 1) * DN_CHUNK)
        for p in range(N_PAIRS):
            cols = slice(p * PAIR, (p + 1) * PAIR)
            st = state_s[p]
            r = _dot(jnp.concatenate([qe_s[rows, cols], m_s[c, p]], axis=0), st.astype(BF16))
            od_s[rows, cols] += r[:DN_CHUNK, :]
            state_s[p] = gl_s[c * DN_CHUNK:c * DN_CHUNK + 1, cols] * st - r[DN_CHUNK:, :] + n_s[c, p]

    od = od_s[...]
    ms = _dot((od * od).astype(BF16), seg) * (1.0 / DN_D)
    zz = z_ref[...]
    o_ref[...] = (od * lax.rsqrt(ms + RMS_EPS) * par_ref[2:3, :] * (zz * jax.nn.sigmoid(zz))).astype(BF16)


def _deltanet(qkv, ab, z, convw, par, consts, bsz, seq):
    t = qkv.shape[0]
    ts = min(TS_DN, seq)
    nt = seq // ts
    hb = ts // HALO
    ea, eb, seg, tri, ones = consts
    row = lambda b, i: (b * nt + i, 0)
    const = lambda b, i: (0, 0)
    halo_map = lambda b, i: (jnp.maximum((b * nt + i) * hb - 1, 0), 0)
    cspec = lambda a: pl.BlockSpec(a.shape, const, pipeline_mode=pl.Buffered(1))
    big = lambda: pltpu.VMEM((ts, DN_W), F32)
    return pl.pallas_call(
        _dn_kernel,
        grid=(bsz, nt),
        in_specs=[pl.BlockSpec((ts, 3 * DN_W), row), pl.BlockSpec((HALO, 3 * DN_W), halo_map),
                  pl.BlockSpec((ts, LANES), row), pl.BlockSpec((ts, DN_W), row),
                  cspec(convw), cspec(par), cspec(ea), cspec(eb), cspec(seg), cspec(tri), cspec(ones)],
        out_specs=pl.BlockSpec((ts, DN_W), row),
        out_shape=jax.ShapeDtypeStruct((t, DN_W), BF16),
        scratch_shapes=[pltpu.VMEM((ts + HALO, 3 * DN_W), F32)] + [big() for _ in range(9)]
                       + [pltpu.VMEM((ts, DN_W), BF16),
                          pltpu.VMEM((ts // DN_CHUNK, N_PAIRS, PAIR, PAIR), BF16),
                          pltpu.VMEM((ts // DN_CHUNK, N_PAIRS, PAIR, PAIR), F32),
                          pltpu.VMEM((N_PAIRS, PAIR, PAIR), F32)],
        compiler_params=_cparams(("parallel", "arbitrary")),
        name="deltanet",
    )(qkv, qkv, ab, z, convw, par, ea, eb, seg, tri, ones)


def _df_kernel(sc_ref, q_ref, k_ref, vt_ref, bt_ref, w_ref, o_ref, sa_ref, sb_ref, acc_ref):
    qi = pl.program_id(2)
    tq = q_ref.shape[0]
    lane = lax.broadcasted_iota(jnp.int32, (tq, 2 * DF_D), 1)
    q = q_ref[...]
    zq = jnp.zeros_like(q)
    qs = (jnp.where(lane < DF_D, q, zq), jnp.where(lane < DF_D, zq, q))
    acc_ref[...] = jnp.zeros_like(acc_ref)

    def scores(j, s_ref):
        kk = k_ref[pl.ds(pl.multiple_of(j * tq, tq), tq), :]
        for mp in range(2):
            s_ref[mp] = _dot_nt(kk, qs[mp])

    def absorb(j, s_ref, bias, carry):
        vt = vt_ref[:, pl.ds(pl.multiple_of(j * tq, tq), tq)]
        out = []
        for mp in range(2):
            m_old, l_old = carry[mp]
            st = s_ref[mp]
            if bias is not None:
                st = st + bias
            m_new = jnp.maximum(m_old, jnp.max(st, axis=0, keepdims=True))
            alpha = jnp.exp2(m_old - m_new)
            pr = jnp.exp2(st - m_new)
            l_new = alpha * l_old + jnp.sum(pr, axis=0, keepdims=True)
            acc_ref[mp] = alpha * acc_ref[mp] + _dot(vt, pr.astype(BF16))
            out.append((m_new, l_new))
        return tuple(out)

    init1 = (jnp.full((1, tq), -jnp.inf, F32), jnp.zeros((1, tq), F32))
    carry = (init1, init1)
    n_far = jnp.maximum(qi - 1, 0)
    odd = n_far % 2

    @pl.when(qi == 0)
    def _():
        scores(0, sb_ref)

    @pl.when(jnp.logical_and(qi > 0, odd == 0))
    def _():
        scores(0, sa_ref)

    def odd_step(_, c):
        scores(0, sb_ref)
        scores(1, sa_ref)
        return absorb(0, sb_ref, None, c)

    carry = lax.fori_loop(0, odd, odd_step, carry)

    def pair_step(jp, c):
        j = odd + 2 * jp
        scores(j + 1, sb_ref)
        c = absorb(j, sa_ref, None, c)
        scores(j + 2, sa_ref)
        return absorb(j + 1, sb_ref, None, c)

    carry = lax.fori_loop(0, n_far // 2, pair_step, carry)

    def tail_step(j, c):
        scores(j + 1, sb_ref)
        c = absorb(j, sa_ref, bt_ref[1], c)
        return absorb(j + 1, sb_ref, bt_ref[0], c)

    carry = lax.fori_loop(n_far, qi, tail_step, carry)
    carry = lax.fori_loop(0, jnp.where(qi == 0, 1, 0), lambda _, c: absorb(0, sb_ref, bt_ref[0], c), carry)

    (_, l0), (_, l1) = carry
    ot = acc_ref[0] / l0 - sc_ref[0] * (acc_ref[1] / l1)
    ms = jnp.mean(ot * ot, axis=0, keepdims=True)
    ot = ot * lax.rsqrt(ms + RMS_EPS) * w_ref[...] * sc_ref[1]
    o_ref[...] = ot.T.astype(BF16)


def _attn_bias_tiles(rel_bias, seq, tq):
    rb = rel_bias.astype(F32)
    far = rb[_t5_causal_bucket(jnp.asarray(seq - 1, jnp.int32))]
    kj = jnp.arange(tq, dtype=jnp.int32)[:, None]
    qi = jnp.arange(tq, dtype=jnp.int32)[None, :]
    dist = jnp.stack([qi - kj, qi + tq - kj])
    onehot = (_t5_causal_bucket(jnp.maximum(dist, 0))[..., None]
              == jnp.arange(REL_BUCKETS, dtype=jnp.int32)).astype(F32)
    vals = (jnp.einsum('ntqb,bh->ntqh', onehot, rb, precision=lax.Precision.HIGHEST) - far) * LOG2E
    vals = jnp.where((dist >= 0)[..., None], vals, -jnp.inf)
    return jnp.transpose(vals, (3, 0, 1, 2))


def _diff_attention(qd, kd, vdt, btiles, subln_w, scalars, bsz, seq):
    t = qd.shape[0]
    tq = min(TQ_DF, seq)
    nq = seq // tq
    sbuf = pltpu.VMEM((2, tq, tq), F32)
    return pl.pallas_call(
        _df_kernel,
        grid_spec=pltpu.PrefetchScalarGridSpec(
            num_scalar_prefetch=1,
            grid=(bsz, DF_HEADS, nq),
            in_specs=[pl.BlockSpec((tq, 2 * DF_D), lambda b, h, i, sc: (b * nq + i, h)),
                      pl.BlockSpec((seq, 2 * DF_D), lambda b, h, i, sc: (b, h)),
                      pl.BlockSpec((None, 2 * DF_D, seq), lambda b, h, i, sc: (b, h, 0)),
                      pl.BlockSpec((None, 2, tq, tq), lambda b, h, i, sc: (h, 0, 0, 0)),
                      pl.BlockSpec((2 * DF_D, 1), lambda b, h, i, sc: (0, 0))],
            out_specs=pl.BlockSpec((tq, 2 * DF_D), lambda b, h, i, sc: (b * nq + i, h)),
            scratch_shapes=[sbuf, sbuf, pltpu.VMEM((2, 2 * DF_D, tq), F32)],
        ),
        out_shape=jax.ShapeDtypeStruct((t, DF_W), BF16),
        compiler_params=_cparams(("parallel", "parallel", "arbitrary")),
        name="diffattn",
    )(scalars, qd, kd, vdt, btiles, subln_w)


def _outproj_kernel(alpha, route, x_ref, a_ref, b_ref, w_ref, g_ref, bb_ref, *rest):
    mix = _dot(a_ref[...], w_ref[0:DN_W, :]) + _dot(b_ref[...], w_ref[DN_W:, :])
    y = _layer_norm(alpha * x_ref[...] + mix, g_ref[...], bb_ref[...])
    if route:
        wr_ref, o_ref, r_ref, xp_ref = rest
        r_ref[...] = _route_top2(y, wr_ref)
        xp_ref[...] = _pack_halves(y)
    else:
        o_ref, = rest
    o_ref[...] = y


def _outproj_ln(x2, o_dn, o_df, w_out, g, b, alpha, wr=None):
    t, d = x2.shape
    tm = min(TM_PROJ, t)
    row = lambda i: (i, 0)
    const = lambda i: (0, 0)
    in_specs = [pl.BlockSpec((tm, d), row), pl.BlockSpec((tm, DN_W), row), pl.BlockSpec((tm, DF_W), row),
                pl.BlockSpec(w_out.shape, const, pipeline_mode=pl.Buffered(1)),
                pl.BlockSpec((1, d), const), pl.BlockSpec((1, d), const)]
    out_specs = [pl.BlockSpec((tm, d), row)]
    out_shape = [jax.ShapeDtypeStruct((t, d), F32)]
    args = [x2, o_dn, o_df, w_out, g, b]
    if wr is not None:
        in_specs.append(pl.BlockSpec(wr.shape, const))
        out_specs += [pl.BlockSpec((tm, LANES), row), pl.BlockSpec((tm, d // 2), row)]
        out_shape += [jax.ShapeDtypeStruct((t, LANES), F32), jax.ShapeDtypeStruct((t, d // 2), jnp.uint32)]
        args.append(wr)
    return pl.pallas_call(
        functools.partial(_outproj_kernel, alpha, wr is not None),
        grid=(t // tm,),
        in_specs=in_specs,
        out_specs=out_specs,
        out_shape=out_shape,
        compiler_params=_cparams(("parallel",)),
        name="outproj_ln",
    )(*args)


def _swiglu_acc(xparts, wg_ref, wu_ref, wd_ref, acc_ref):
    d_ff = wg_ref.shape[-1]
    for c0 in range(0, d_ff, FF_CHUNK):
        cs = slice(c0, c0 + FF_CHUNK)
        hg = sum(_dot(xp, wg_ref[k0:k0 + xp.shape[1], cs]) for xp, k0 in xparts)
        hu = sum(_dot(xp, wu_ref[k0:k0 + xp.shape[1], cs]) for xp, k0 in xparts)
        hh = (hg * jax.nn.sigmoid(hg) * hu).astype(BF16)
        contrib = _dot(hh, wd_ref[cs, :])
        if c0 == 0:
            acc_ref[...] = contrib
        else:
            acc_ref[...] += contrib


def _ffn_kernel(alpha, x_ref, wg_ref, wu_ref, wd_ref, g_ref, b_ref, o_ref, acc_ref):
    x = x_ref[...]
    _swiglu_acc([(x.astype(BF16), 0)], wg_ref, wu_ref, wd_ref, acc_ref)
    o_ref[...] = _layer_norm(alpha * x + acc_ref[...], g_ref[...], b_ref[...])


def _ffn_ln(x2, wg, wu, wd, g, b, alpha):
    t, d = x2.shape
    tm = min(TM_FFN, t)
    row = lambda i: (i, 0)
    const = lambda i: (0, 0)
    wspec = lambda a: pl.BlockSpec(a.shape, const, pipeline_mode=pl.Buffered(1))
    return pl.pallas_call(
        functools.partial(_ffn_kernel, alpha),
        grid=(t // tm,),
        in_specs=[pl.BlockSpec((tm, d), row), wspec(wg), wspec(wu), wspec(wd),
                  pl.BlockSpec((1, d), const), pl.BlockSpec((1, d), const)],
        out_specs=pl.BlockSpec((tm, d), row),
        out_shape=jax.ShapeDtypeStruct((t, d), F32),
        scratch_shapes=[pltpu.VMEM((tm, d), F32)],
        compiler_params=_cparams(("parallel",)),
        name="ffn_ln",
    )(x2, wg, wu, wd, g, b)


def _pack_halves(x):
    h = x.shape[1] // 2
    hi = lax.bitcast_convert_type(x[:, :h].astype(BF16).astype(F32), jnp.uint32)
    lo = lax.bitcast_convert_type(x[:, h:].astype(BF16).astype(F32), jnp.uint32)
    return hi | (lo >> 16)


def _unpack_halves(p):
    a = lax.bitcast_convert_type(p & jnp.uint32(0xFFFF0000), F32)
    b = lax.bitcast_convert_type(p << 16, F32)
    return a, b


def _route_top2(x, wr_ref):
    logits = jnp.dot(x, wr_ref[...], preferred_element_type=F32, precision=lax.Precision.HIGHEST)
    lane = lax.broadcasted_iota(jnp.int32, logits.shape, 1)
    lg = jnp.where(lane < N_EXPERTS, logits, -jnp.inf)
    m1 = jnp.max(lg, axis=-1, keepdims=True)
    i1 = jnp.min(jnp.where(lg == m1, lane, LANES), axis=-1, keepdims=True)
    lg2 = jnp.where(lane == i1, -jnp.inf, lg)
    m2 = jnp.max(lg2, axis=-1, keepdims=True)
    i2 = jnp.min(jnp.where(lg2 == m2, lane, LANES), axis=-1, keepdims=True)
    e = jnp.exp(m2 - m1)
    g1 = 1.0 / (1.0 + e)
    g2 = e / (1.0 + e)
    out = jnp.where(lane == 0, i1.astype(F32), 0.0)
    out = jnp.where(lane == 1, i2.astype(F32), out)
    out = jnp.where(lane == 2, g1, out)
    return jnp.where(lane == 3, g2, out)


def _sc_mesh():
    return plsc.VectorSubcoreMesh(core_axis_name="core", subcore_axis_name="subcore")


def _sc_index_rows(idx):
    win = idx.reshape(-1, SC_WIN)
    return jnp.concatenate([win, jnp.zeros((win.shape[0], SC_IDX_TILE - SC_WIN), idx.dtype)], axis=1)


def _sc_dispatch(xp, slot0, slot1, n_slot):
    t, c = xp.shape
    half = t // SC_WIN // 2
    idx_spec = pl.BlockSpec((1, SC_IDX_TILE), lambda cc, i: (cc * half + i, 0))

    @pl.kernel(out_type=jax.ShapeDtypeStruct((n_slot, c), xp.dtype), mesh=_sc_mesh(), scratch_types=[],
               name="moe_dispatch")
    def run(x_hbm, i0_hbm, i1_hbm, o_hbm):
        def body(x_vmem, i0_vmem, i1_vmem):
            pltpu.sync_copy(x_vmem, o_hbm.at[i0_vmem.at[0, pl.ds(0, SC_WIN)]])
            pltpu.sync_copy(x_vmem, o_hbm.at[i1_vmem.at[0, pl.ds(0, SC_WIN)]])

        pltpu.emit_pipeline(
            body,
            grid=(2, half),
            in_specs=[pl.BlockSpec((SC_WIN, c), lambda cc, i: (cc * half + i, 0)), idx_spec, idx_spec],
            out_specs=[],
            core_axis_name=("core", "subcore"),
            dimension_semantics=(pltpu.PARALLEL, pltpu.PARALLEL),
        )(x_hbm, i0_hbm, i1_hbm)

    return run(xp, _sc_index_rows(slot0), _sc_index_rows(slot1))


def _sc_gather(yp, idx):
    t = idx.shape[0]
    c = yp.shape[1]
    half = t // SC_WIN // 2

    @pl.kernel(out_type=jax.ShapeDtypeStruct((t, c), yp.dtype), mesh=_sc_mesh(), scratch_types=[],
               name="moe_gather")
    def run(y_hbm, i_hbm, o_hbm):
        def body(i_vmem, o_vmem):
            pltpu.sync_copy(y_hbm.at[i_vmem.at[0, pl.ds(0, SC_WIN)]], o_vmem)

        pltpu.emit_pipeline(
            body,
            grid=(2, half),
            in_specs=[pl.BlockSpec((1, SC_IDX_TILE), lambda cc, i: (cc * half + i, 0))],
            out_specs=[pl.BlockSpec((SC_WIN, c), lambda cc, i: (cc * half + i, 0))],
            core_axis_name=("core", "subcore"),
            dimension_semantics=(pltpu.PARALLEL, pltpu.PARALLEL),
        )(i_hbm, o_hbm)

    return run(yp, _sc_index_rows(idx))


def _expert_kernel(be_ref, nv_ref, x_ref, wg_ref, wu_ref, wd_ref, o_ref, acc_ref):
    i = pl.program_id(0)
    n_valid = nv_ref[i]

    @pl.when(n_valid > 0)
    def _():
        row = lax.broadcasted_iota(jnp.int32, x_ref.shape, 0)
        xa, xb = _unpack_halves(jnp.where(row < n_valid, x_ref[...], jnp.uint32(0)))
        h = xa.shape[1]
        _swiglu_acc([(xa.astype(BF16), 0), (xb.astype(BF16), h)], wg_ref, wu_ref, wd_ref, acc_ref)
        o_ref[...] = _pack_halves(acc_ref[...])

    @pl.when(n_valid <= 0)
    def _():
        o_ref[...] = jnp.zeros_like(o_ref)


def _experts(xbp, blk_e, n_valid, wg, wu, wd):
    n_slot, dh = xbp.shape
    d = 2 * dh
    n_blk = n_slot // MOE_BLK
    dff = wg.shape[-1]
    row = lambda i, be, nv: (i, 0)
    wmap = lambda i, be, nv: (be[i], 0, 0)
    return pl.pallas_call(
        _expert_kernel,
        grid_spec=pltpu.PrefetchScalarGridSpec(
            num_scalar_prefetch=2,
            grid=(n_blk,),
            in_specs=[pl.BlockSpec((MOE_BLK, dh), row),
                      pl.BlockSpec((None, d, dff), wmap, pipeline_mode=pl.Buffered(1)),
                      pl.BlockSpec((None, d, dff), wmap, pipeline_mode=pl.Buffered(1)),
                      pl.BlockSpec((None, dff, d), wmap, pipeline_mode=pl.Buffered(1))],
            out_specs=pl.BlockSpec((MOE_BLK, dh), row),
            scratch_shapes=[pltpu.VMEM((MOE_BLK, d), F32)],
        ),
        out_shape=jax.ShapeDtypeStruct((n_slot, dh), jnp.uint32),
        compiler_params=_cparams(("arbitrary",)),
        name="experts",
    )(blk_e, n_valid, xbp, wg, wu, wd)


def _combine_kernel(alpha, x_ref, y0_ref, y1_ref, r_ref, g_ref, b_ref, o_ref):
    r = r_ref[...]
    a0, b0 = _unpack_halves(y0_ref[...])
    a1, b1 = _unpack_halves(y1_ref[...])
    g0 = r[:, 2:3]
    g1 = r[:, 3:4]
    f = jnp.concatenate([g0 * a0 + g1 * a1, g0 * b0 + g1 * b1], axis=1)
    o_ref[...] = _layer_norm(alpha * x_ref[...] + f, g_ref[...], b_ref[...])


def _combine_ln(x2, y0, y1, route, g, b, alpha):
    t, d = x2.shape
    tm = min(TM_PROJ, t)
    row = lambda i: (i, 0)
    const = lambda i: (0, 0)
    return pl.pallas_call(
        functools.partial(_combine_kernel, alpha),
        grid=(t // tm,),
        in_specs=[pl.BlockSpec((tm, d), row), pl.BlockSpec((tm, d // 2), row), pl.BlockSpec((tm, d // 2), row),
                  pl.BlockSpec((tm, LANES), row), pl.BlockSpec((1, d), const), pl.BlockSpec((1, d), const)],
        out_specs=pl.BlockSpec((tm, d), row),
        out_shape=jax.ShapeDtypeStruct((t, d), F32),
        compiler_params=_cparams(("parallel",)),
        name="combine_ln",
    )(x2, y0, y1, route, g, b)


def _moe(x2, route, xp, wg, wu, wd, g, b, alpha):
    t, d = x2.shape
    top_idx = route[:, 0:2].astype(jnp.int32)
    n_asg = t * TOP_K
    flat_e = top_idx.reshape(n_asg)
    onehot = (flat_e[:, None] == jnp.arange(N_EXPERTS, dtype=jnp.int32)[None, :]).astype(jnp.int32)
    csum = jnp.cumsum(onehot, axis=0)
    rank = jnp.sum((csum - onehot) * onehot, axis=1)
    counts = csum[-1]
    padded = (counts + MOE_BLK - 1) // MOE_BLK * MOE_BLK
    pad_end = jnp.cumsum(padded)
    pad_start = pad_end - padded
    slot = (pad_start[flat_e] + rank).reshape(t, TOP_K)
    n_slot = -(-n_asg // MOE_BLK) * MOE_BLK + N_EXPERTS * MOE_BLK
    n_blk = n_slot // MOE_BLK
    blk_start = jnp.arange(n_blk, dtype=jnp.int32) * MOE_BLK
    blk_e = jnp.minimum(jnp.searchsorted(pad_end, blk_start, side='right'), N_EXPERTS - 1).astype(jnp.int32)
    n_valid = jnp.clip(pad_start[blk_e] + counts[blk_e] - blk_start, 0, MOE_BLK).astype(jnp.int32)
    slot0 = slot[:, 0]
    slot1 = slot[:, 1]
    xbp = _sc_dispatch(xp, slot0, slot1, n_slot)
    ybp = _experts(xbp, blk_e, n_valid, wg, wu, wd)
    return _combine_ln(x2, _sc_gather(ybp, slot0), _sc_gather(ybp, slot1), route, g, b, alpha)


def _t5_causal_bucket(dist):
    max_exact = REL_BUCKETS // 2
    d = jnp.maximum(dist, 1).astype(F32)
    large = max_exact + (jnp.log(d / max_exact) / math.log(REL_MAX_DIST / max_exact)
                         * (REL_BUCKETS - max_exact)).astype(jnp.int32)
    large = jnp.minimum(large, REL_BUCKETS - 1)
    return jnp.where(dist < max_exact, dist, large)


def _dn_constants(ts):
    lanes = np.arange(DN_W)
    ea = np.zeros((LANES, DN_W), np.float32)
    eb = np.zeros((LANES, DN_W), np.float32)
    ea[lanes // DN_D, lanes] = 1.0
    eb[DN_HEADS + lanes // DN_D, lanes] = 1.0
    seg = (lanes[:, None] // DN_D == lanes[None, :] // DN_D).astype(np.float32)
    r = np.arange(ts)
    same = r[:, None] // DN_CHUNK == r[None, :] // DN_CHUNK
    tri = (same & (r[:, None] >= r[None, :])).astype(np.float32)
    ones = same.astype(np.float32)
    return tuple(jnp.asarray(a, BF16) for a in (ea, eb, seg, tri, ones))


def kernel(x, w_in, w_out, conv_w, dn_a_log, dn_dt_bias, dn_norm_w, df_lambda, df_subln_w, rel_bias,
           ln1_g, ln1_b, ln2_g, ln2_b, ffn_w_gate, ffn_w_up, ffn_w_down, moe_router, moe_w_gate,
           moe_w_up, moe_w_down):
    bsz, seq, d = x.shape
    depth = w_in.shape[0]
    alpha = (2 * depth) ** 0.25
    t = bsz * seq
    tq = min(TQ_DF, seq)
    assert tq >= REL_MAX_DIST and seq % tq == 0 and seq % min(TS_DN, seq) == 0
    btiles = _attn_bias_tiles(rel_bias, seq, tq)

    dn_consts = _dn_constants(min(TS_DN, seq))
    c_dn = 3 * DN_W
    x2 = x.reshape(t, d)
    for layer in range(depth):
        lambda_init = 0.8 - 0.6 * math.exp(-0.3 * layer)
        wl = w_in[layer]
        n1 = c_dn + DN_W + 2 * DN_HEADS
        w1 = jnp.concatenate([wl[:, :n1], jnp.zeros((d, LANES - 2 * DN_HEADS), F32)], axis=1).astype(BF16)
        w2 = wl[:, n1:n1 + 2 * DF_W].astype(BF16)
        wvt = wl[:, n1 + 2 * DF_W:].T.astype(BF16)
        qkv, z, ab, qd, kd, vdt = _inproj(x2, w1, w2, wvt, bsz, seq)

        convw = jnp.concatenate([conv_w[layer], jnp.zeros((HALO - DN_CONV, c_dn), F32)], axis=0)
        par = jnp.zeros((HALO, DN_W), F32)
        par = par.at[0, 0:DN_HEADS].set(dn_a_log[layer])
        par = par.at[1, 0:DN_HEADS].set(dn_dt_bias[layer])
        par = par.at[2].set(jnp.tile(dn_norm_w[layer], DN_HEADS))
        o_dn = _deltanet(qkv, ab, z, convw, par, dn_consts, bsz, seq)

        lf = df_lambda[layer].astype(F32)
        lam = jnp.exp(jnp.sum(lf[0] * lf[1])) - jnp.exp(jnp.sum(lf[2] * lf[3])) + lambda_init
        scalars = jnp.stack([lam, jnp.asarray(1.0 - lambda_init, F32)]).astype(F32)
        o_df = _diff_attention(qd, kd, vdt, btiles, df_subln_w[layer].reshape(2 * DF_D, 1), scalars, bsz, seq)

        i = layer // 2
        g1 = ln1_g[layer].reshape(1, d)
        b1 = ln1_b[layer].reshape(1, d)
        g2 = ln2_g[layer].reshape(1, d)
        b2 = ln2_b[layer].reshape(1, d)
        wo = w_out[layer].astype(BF16)
        if layer % 2 == 0:
            x2, = _outproj_ln(x2, o_dn, o_df, wo, g1, b1, alpha)
            x2 = _ffn_ln(x2, ffn_w_gate[i].astype(BF16), ffn_w_up[i].astype(BF16), ffn_w_down[i].astype(BF16),
                         g2, b2, alpha)
        else:
            wr = jnp.concatenate([moe_router[i], jnp.zeros((d, LANES - N_EXPERTS), F32)], axis=1)
            x2, route, xp = _outproj_ln(x2, o_dn, o_df, wo, g1, b1, alpha, wr)
            x2 = _moe(x2, route, xp, moe_w_gate[i].astype(BF16), moe_w_up[i].astype(BF16),
                      moe_w_down[i].astype(BF16), g2, b2, alpha)
    return x2.reshape(bsz, seq, d)
```

```python
import functools
import math

import jax
import jax.numpy as jnp
import numpy as np
from jax import lax
from jax.experimental import pallas as pl
from jax.experimental.pallas import tpu as pltpu
from jax.experimental.pallas import tpu_sc as plsc

F32 = jnp.float32
BF16 = jnp.bfloat16

DN_HEADS = 8
DN_D = 64
DN_CONV = 4
DN_CHUNK = 64
DN_W = DN_HEADS * DN_D
DF_HEADS = 4
DF_D = 64
DF_W = DF_HEADS * 2 * DF_D
REL_BUCKETS = 32
REL_MAX_DIST = 128
N_EXPERTS = 8
TOP_K = 2
MOE_BLK = 512
LN_EPS = 1e-5
RMS_EPS = 1e-6
LOG2E = math.log2(math.e)

LANES = 128
HALO = 8
PAIR = 2 * DN_D
N_PAIRS = DN_HEADS // 2
VMEM_LIMIT = 56 * 1024 * 1024

TM_PROJ = 512
TS_DN = 256
DN_UNROLL = 4
TQ_DF = 1024
TM_FFN = 512
FF_CHUNK = 256
SC_WIN = 64
SC_IDX_TILE = 128


def _cparams(sem):
    return pltpu.CompilerParams(dimension_semantics=sem, vmem_limit_bytes=VMEM_LIMIT)


def _dot(a, b):
    return jnp.dot(a, b, preferred_element_type=F32)


def _dot_nt(a, b):
    return lax.dot_general(a, b, (((1,), (1,)), ((), ())), preferred_element_type=F32)


def _dot_tn(a, b):
    return lax.dot_general(a, b, (((0,), (0,)), ((), ())), preferred_element_type=F32)


def _split(x):
    hi = x.astype(BF16)
    lo = (x - hi.astype(F32)).astype(BF16)
    return hi, lo


def _mm_xc(x, c):
    hi, lo = _split(x)
    return _dot(hi, c) + _dot(lo, c)


def _mm_cx(c, x):
    hi, lo = _split(x)
    return _dot(c, hi) + _dot(c, lo)


def _mm3(a, b):
    ah, al = _split(a)
    bh, bl = _split(b)
    return _dot(ah, bh) + _dot(ah, bl) + _dot(al, bh)


def _layer_norm(v, g, b):
    mu = jnp.mean(v, axis=-1, keepdims=True)
    d = v - mu
    var = jnp.mean(d * d, axis=-1, keepdims=True)
    return d * lax.rsqrt(var + LN_EPS) * g + b


def _inproj_kernel(x_ref, w1_ref, w2_ref, wvt_ref, qkv_ref, z_ref, ab_ref, qd_ref, kd_ref, vd_ref):
    xb = x_ref[...].astype(BF16)
    c = 3 * DN_W
    qkv_ref[...] = _dot(xb, w1_ref[:, 0:c])
    z_ref[...] = _dot(xb, w1_ref[:, c:c + DN_W])
    ab_ref[...] = _dot(xb, w1_ref[:, c + DN_W:c + DN_W + LANES])
    qd_ref[...] = (_dot(xb, w2_ref[:, 0:DF_W]) * (DF_D ** -0.5 * LOG2E)).astype(BF16)
    kd_ref[...] = _dot(xb, w2_ref[:, DF_W:2 * DF_W]).astype(BF16)
    vd_ref[...] = _dot_nt(wvt_ref[...], xb).astype(BF16)


def _inproj(x2, w1, w2, wvt, bsz, seq):
    t, d = x2.shape
    tm = min(TM_PROJ, seq)
    nt = seq // tm
    n1, n2 = w1.shape[1], w2.shape[1]
    row = lambda i: (i, 0)
    const = lambda i: (0, 0)
    return pl.pallas_call(
        _inproj_kernel,
        grid=(t // tm,),
        in_specs=[pl.BlockSpec((tm, d), row),
                  pl.BlockSpec((d, n1), const, pipeline_mode=pl.Buffered(1)),
                  pl.BlockSpec((d, n2), const, pipeline_mode=pl.Buffered(1)),
                  pl.BlockSpec((DF_W, d), const, pipeline_mode=pl.Buffered(1))],
        out_specs=[pl.BlockSpec((tm, 3 * DN_W), row), pl.BlockSpec((tm, DN_W), row),
                   pl.BlockSpec((tm, LANES), row), pl.BlockSpec((tm, DF_W), row),
                   pl.BlockSpec((tm, DF_W), row),
                   pl.BlockSpec((None, DF_W, tm), lambda i: (i // nt, 0, i % nt))],
        out_shape=[jax.ShapeDtypeStruct((t, 3 * DN_W), F32), jax.ShapeDtypeStruct((t, DN_W), F32),
                   jax.ShapeDtypeStruct((t, LANES), F32), jax.ShapeDtypeStruct((t, DF_W), BF16),
                   jax.ShapeDtypeStruct((t, DF_W), BF16), jax.ShapeDtypeStruct((bsz, DF_W, seq), BF16)],
        compiler_params=_cparams(("parallel",)),
        name="inproj",
    )(x2, w1, w2, wvt)


def _dn_kernel(qkv_ref, halo_ref, ab_ref, z_ref, convw_ref, par_ref, ea_ref, eb_ref, seg_ref,
               tri_ref, ones_ref, o_ref,
               xe_s, q_s, k_s, kb_s, vb_s, gc_s, eg_s, kdec_s, gl_s, od_s, qe_s, m_s, n_s, state_s):
    i = pl.program_id(1)
    ts = qkv_ref.shape[0]
    n_chunks = ts // DN_CHUNK

    @pl.when(i == 0)
    def _():
        state_s[...] = jnp.zeros_like(state_s)

    halo = halo_ref[...]
    xe_s[0:HALO, :] = jnp.where(i > 0, halo, jnp.zeros_like(halo))
    xe_s[HALO:, :] = qkv_ref[...]
    xe = xe_s[...]
    y = convw_ref[DN_CONV - 1:DN_CONV, :] * xe[HALO:, :]
    for tap in range(DN_CONV - 1):
        y += convw_ref[tap:tap + 1, :] * pltpu.roll(xe, DN_CONV - 1 - tap, axis=0)[HALO:, :]
    y = y * jax.nn.sigmoid(y)
    q = y[:, 0:DN_W]
    k = y[:, DN_W:2 * DN_W]
    v = y[:, 2 * DN_W:3 * DN_W]

    seg = seg_ref[...]
    q = q * lax.rsqrt(_dot((q * q).astype(BF16), seg) + 1e-6) * (DN_D ** -0.5)
    k = k * lax.rsqrt(_dot((k * k).astype(BF16), seg) + 1e-6)

    ab = ab_ref[...]
    xa = ab + par_ref[1:2, 0:LANES]
    softplus = jnp.maximum(xa, 0.0) + jnp.log(1.0 + jnp.exp(-jnp.abs(xa)))
    g = _mm_xc(-jnp.exp(par_ref[0:1, 0:LANES]) * softplus, ea_ref[...])
    beta = _mm_xc(jax.nn.sigmoid(ab), eb_ref[...])
    gc = _mm_cx(tri_ref[...], g)
    gl = _mm_cx(ones_ref[...], g)
    eg = jnp.exp(gc)
    kb = k * beta
    q_s[...] = q
    k_s[...] = k
    kb_s[...] = kb
    vb_s[...] = v * beta
    gc_s[...] = gc
    eg_s[...] = eg
    kdec_s[...] = k * jnp.exp(gl - gc)
    gl_s[...] = jnp.exp(gl)

    lane = lax.broadcasted_iota(jnp.int32, (DN_CHUNK, PAIR), 1)
    rowi = lax.broadcasted_iota(jnp.int32, (DN_CHUNK, PAIR), 0)
    colj = jnp.where(lane >= DN_D, lane - DN_D, lane)
    even = lane < DN_D
    eye2 = rowi == colj
    lower = rowi >= colj
    lane_b = lax.broadcasted_iota(jnp.int32, (PAIR, PAIR), 1)
    row_b = lax.broadcasted_iota(jnp.int32, (PAIR, PAIR), 0)
    bdmask = (lane_b < DN_D) == (row_b < DN_D)

    def bd(xm):
        z0 = jnp.zeros_like(xm)
        return jnp.concatenate([jnp.where(even, xm, z0), jnp.where(even, z0, xm)], axis=0)

    eye_f = jnp.where(eye2, 1.0, 0.0)

    def local_body(cc, carry):
        chains = [(cc * DN_UNROLL + dc, p) for dc in range(DN_UNROLL) for p in range(N_PAIRS)]
        n = len(chains)
        rows = [pl.ds(pl.multiple_of(c * DN_CHUNK, DN_CHUNK), DN_CHUNK) for c, _ in chains]
        cols = [slice(p * PAIR, (p + 1) * PAIR) for _, p in chains]
        ld = lambda ref, i: ref[rows[i], cols[i]]

        aq = [_dot_nt(jnp.concatenate([ld(kb_s, i), ld(q_s, i)], axis=0).astype(BF16),
                      bd(ld(k_s, i).astype(BF16))) for i in range(n)]
        a_qk, l_m = [], []
        for i in range(n):
            gcc = ld(gc_s, i)
            gcj = jnp.sum(jnp.where(eye2, gcc, 0.0), axis=0, keepdims=True)
            dec = jnp.where(lower, jnp.exp(jnp.minimum(gcc - gcj, 0.0)), 0.0)
            a_qk.append((aq[i][DN_CHUNK:, :] * dec).astype(BF16))
            l_m.append(jnp.where(eye2, 0.0, aq[i][:DN_CHUNK, :] * dec))

        lhl = [_split(m) for m in l_m]
        x0 = [eye_f - m for m in l_m]
        yb = [h for h, _ in lhl]
        for _ in range(5):
            yb = [_dot(yb[i], bd(yb[i])).astype(BF16) for i in range(n)]
            x0 = [x0[i] + _dot(x0[i].astype(BF16), bd(yb[i])) for i in range(n)]
        xhl = [_split(m) for m in x0]
        res = []
        for i in range(n):
            bxh = bd(xhl[i][0])
            lx = _dot(lhl[i][0], bxh) + _dot(lhl[i][0], bd(xhl[i][1])) + _dot(lhl[i][1], bxh)
            res.append((eye_f - x0[i] - lx).astype(BF16))
        t_m = [x0[i] + _dot(xhl[i][0], bd(res[i])) for i in range(n)]

        uw = []
        for i in range(n):
            th, tl = _split(t_m[i])
            kbg = (ld(kb_s, i) * ld(eg_s, i)).astype(BF16)
            rhs = jnp.concatenate([bd(ld(vb_s, i).astype(BF16)), bd(kbg)], axis=1)
            uw.append((_dot(th, rhs) + _dot(tl, rhs)).astype(BF16))
        qo = [_dot(a_qk[i], jnp.concatenate([bd(uw[i][:, PAIR:]), bd(uw[i][:, :PAIR])], axis=1))
              for i in range(n)]
        mn = [_dot_tn(ld(kdec_s, i).astype(BF16), jnp.concatenate([uw[i][:, PAIR:], uw[i][:, :PAIR]], axis=1))
              for i in range(n)]
        for i, (c, p) in enumerate(chains):
            qe_s[rows[i], cols[i]] = (ld(q_s, i) * ld(eg_s, i) - qo[i][:, :PAIR]).astype(BF16)
            od_s[rows[i], cols[i]] = qo[i][:, PAIR:]
            m_s[c, p] = jnp.where(bdmask, mn[i][:, :PAIR], 0.0).astype(BF16)
            n_s[c, p] = jnp.where(bdmask, mn[i][:, PAIR:], 0.0)
        return carry

    lax.fori_loop(0, n_chunks // DN_UNROLL, local_body, 0)

    for c in range(n_chunks):
        rows = slice(c * DN_CHUNK, (c + 1) * DN_CHUNK)
        for p in range(N_PAIRS):
            cols = slice(p * PAIR, (p + 1) * PAIR)
            st = state_s[p]
            r = _dot(jnp.concatenate([qe_s[rows, cols], m_s[c, p]], axis=0), st.astype(BF16))
            od_s[rows, cols] += r[:DN_CHUNK, :]
            state_s[p] = gl_s[c * DN_CHUNK:c * DN_CHUNK + 1, cols] * st - r[DN_CHUNK:, :] + n_s[c, p]

    od = od_s[...]
    ms = _dot((od * od).astype(BF16), seg) * (1.0 / DN_D)
    zz = z_ref[...]
    o_ref[...] = (od * lax.rsqrt(ms + RMS_EPS) * par_ref[2:3, :] * (zz * jax.nn.sigmoid(zz))).astype(BF16)


def _deltanet(qkv, ab, z, convw, par, consts, bsz, seq):
    t = qkv.shape[0]
    ts = min(TS_DN, seq)
    nt = seq // ts
    hb = ts // HALO
    ea, eb, seg, tri, ones = consts
    row = lambda b, i: (b * nt + i, 0)
    const = lambda b, i: (0, 0)
    halo_map = lambda b, i: (jnp.maximum((b * nt + i) * hb - 1, 0), 0)
    cspec = lambda a: pl.BlockSpec(a.shape, const, pipeline_mode=pl.Buffered(1))
    big = lambda: pltpu.VMEM((ts, DN_W), F32)
    return pl.pallas_call(
        _dn_kernel,
        grid=(bsz, nt),
        in_specs=[pl.BlockSpec((ts, 3 * DN_W), row), pl.BlockSpec((HALO, 3 * DN_W), halo_map),
                  pl.BlockSpec((ts, LANES), row), pl.BlockSpec((ts, DN_W), row),
                  cspec(convw), cspec(par), cspec(ea), cspec(eb), cspec(seg), cspec(tri), cspec(ones)],
        out_specs=pl.BlockSpec((ts, DN_W), row),
        out_shape=jax.ShapeDtypeStruct((t, DN_W), BF16),
        scratch_shapes=[pltpu.VMEM((ts + HALO, 3 * DN_W), F32)] + [big() for _ in range(9)]
                       + [pltpu.VMEM((ts, DN_W), BF16),
                          pltpu.VMEM((ts // DN_CHUNK, N_PAIRS, PAIR, PAIR), BF16),
                          pltpu.VMEM((ts // DN_CHUNK, N_PAIRS, PAIR, PAIR), F32),
                          pltpu.VMEM((N_PAIRS, PAIR, PAIR), F32)],
        compiler_params=_cparams(("parallel", "arbitrary")),
        name="deltanet",
    )(qkv, qkv, ab, z, convw, par, ea, eb, seg, tri, ones)


def _df_kernel(sc_ref, q_ref, k_ref, vt_ref, bt_ref, w_ref, o_ref, sa_ref, sb_ref, acc_ref):
    qi = pl.program_id(2)
    tq = q_ref.shape[0]
    lane = lax.broadcasted_iota(jnp.int32, (tq, 2 * DF_D), 1)
    q = q_ref[...]
    zq = jnp.zeros_like(q)
    qs = (jnp.where(lane < DF_D, q, zq), jnp.where(lane < DF_D, zq, q))
    acc_ref[...] = jnp.zeros_like(acc_ref)

    def scores(j, s_ref):
        kk = k_ref[pl.ds(pl.multiple_of(j * tq, tq), tq), :]
        for mp in range(2):
            s_ref[mp] = _dot_nt(kk, qs[mp])

    def absorb(j, s_ref, bias, carry):
        vt = vt_ref[:, pl.ds(pl.multiple_of(j * tq, tq), tq)]
        out = []
        for mp in range(2):
            m_old, l_old = carry[mp]
            st = s_ref[mp]
            if bias is not None:
                st = st + bias
            m_new = jnp.maximum(m_old, jnp.max(st, axis=0, keepdims=True))
            alpha = jnp.exp2(m_old - m_new)
            pr = jnp.exp2(st - m_new)
            l_new = alpha * l_old + jnp.sum(pr, axis=0, keepdims=True)
            acc_ref[mp] = alpha * acc_ref[mp] + _dot(vt, pr.astype(BF16))
            out.append((m_new, l_new))
        return tuple(out)

    init1 = (jnp.full((1, tq), -jnp.inf, F32), jnp.zeros((1, tq), F32))
    carry = (init1, init1)
    n_far = jnp.maximum(qi - 1, 0)
    odd = n_far % 2

    @pl.when(qi == 0)
    def _():
        scores(0, sb_ref)

    @pl.when(jnp.logical_and(qi > 0, odd == 0))
    def _():
        scores(0, sa_ref)

    def odd_step(_, c):
        scores(0, sb_ref)
        scores(1, sa_ref)
        return absorb(0, sb_ref, None, c)

    carry = lax.fori_loop(0, odd, odd_step, carry)

    def pair_step(jp, c):
        j = odd + 2 * jp
        scores(j + 1, sb_ref)
        c = absorb(j, sa_ref, None, c)
        scores(j + 2, sa_ref)
        return absorb(j + 1, sb_ref, None, c)

    carry = lax.fori_loop(0, n_far // 2, pair_step, carry)

    def tail_step(j, c):
        scores(j + 1, sb_ref)
        c = absorb(j, sa_ref, bt_ref[1], c)
        return absorb(j + 1, sb_ref, bt_ref[0], c)

    carry = lax.fori_loop(n_far, qi, tail_step, carry)
    carry = lax.fori_loop(0, jnp.where(qi == 0, 1, 0), lambda _, c: absorb(0, sb_ref, bt_ref[0], c), carry)

    (_, l0), (_, l1) = carry
    ot = acc_ref[0] / l0 - sc_ref[0] * (acc_ref[1] / l1)
    ms = jnp.mean(ot * ot, axis=0, keepdims=True)
    ot = ot * lax.rsqrt(ms + RMS_EPS) * w_ref[...] * sc_ref[1]
    o_ref[...] = ot.T.astype(BF16)


def _attn_bias_tiles(rel_bias, seq, tq):
    rb = rel_bias.astype(F32)
    far = rb[_t5_causal_bucket(jnp.asarray(seq - 1, jnp.int32))]
    kj = jnp.arange(tq, dtype=jnp.int32)[:, None]
    qi = jnp.arange(tq, dtype=jnp.int32)[None, :]
    dist = jnp.stack([qi - kj, qi + tq - kj])
    onehot = (_t5_causal_bucket(jnp.maximum(dist, 0))[..., None]
              == jnp.arange(REL_BUCKETS, dtype=jnp.int32)).astype(F32)
    vals = (jnp.einsum('ntqb,bh->ntqh', onehot, rb, precision=lax.Precision.HIGHEST) - far) * LOG2E
    vals = jnp.where((dist >= 0)[..., None], vals, -jnp.inf)
    return jnp.transpose(vals, (3, 0, 1, 2))


def _diff_attention(qd, kd, vdt, btiles, subln_w, scalars, bsz, seq):
    t = qd.shape[0]
    tq = min(TQ_DF, seq)
    nq = seq // tq
    sbuf = pltpu.VMEM((2, tq, tq), F32)
    return pl.pallas_call(
        _df_kernel,
        grid_spec=pltpu.PrefetchScalarGridSpec(
            num_scalar_prefetch=1,
            grid=(bsz, DF_HEADS, nq),
            in_specs=[pl.BlockSpec((tq, 2 * DF_D), lambda b, h, i, sc: (b * nq + i, h)),
                      pl.BlockSpec((seq, 2 * DF_D), lambda b, h, i, sc: (b, h)),
                      pl.BlockSpec((None, 2 * DF_D, seq), lambda b, h, i, sc: (b, h, 0)),
                      pl.BlockSpec((None, 2, tq, tq), lambda b, h, i, sc: (h, 0, 0, 0)),
                      pl.BlockSpec((2 * DF_D, 1), lambda b, h, i, sc: (0, 0))],
            out_specs=pl.BlockSpec((tq, 2 * DF_D), lambda b, h, i, sc: (b * nq + i, h)),
            scratch_shapes=[sbuf, sbuf, pltpu.VMEM((2, 2 * DF_D, tq), F32)],
        ),
        out_shape=jax.ShapeDtypeStruct((t, DF_W), BF16),
        compiler_params=_cparams(("parallel", "parallel", "arbitrary")),
        name="diffattn",
    )(scalars, qd, kd, vdt, btiles, subln_w)


def _outproj_kernel(alpha, x_ref, a_ref, b_ref, w_ref, g_ref, bb_ref, o_ref):
    mix = _dot(a_ref[...], w_ref[0:DN_W, :]) + _dot(b_ref[...], w_ref[DN_W:, :])
    o_ref[...] = _layer_norm(alpha * x_ref[...] + mix, g_ref[...], bb_ref[...])


def _outproj_ln(x2, o_dn, o_df, w_out, g, b, alpha):
    t, d = x2.shape
    tm = min(TM_PROJ, t)
    row = lambda i: (i, 0)
    const = lambda i: (0, 0)
    return pl.pallas_call(
        functools.partial(_outproj_kernel, alpha),
        grid=(t // tm,),
        in_specs=[pl.BlockSpec((tm, d), row), pl.BlockSpec((tm, DN_W), row), pl.BlockSpec((tm, DF_W), row),
                  pl.BlockSpec(w_out.shape, const, pipeline_mode=pl.Buffered(1)),
                  pl.BlockSpec((1, d), const), pl.BlockSpec((1, d), const)],
        out_specs=pl.BlockSpec((tm, d), row),
        out_shape=jax.ShapeDtypeStruct((t, d), F32),
        compiler_params=_cparams(("parallel",)),
        name="outproj_ln",
    )(x2, o_dn, o_df, w_out, g, b)


def _swiglu_acc(xparts, wg_ref, wu_ref, wd_ref, acc_ref):
    d_ff = wg_ref.shape[-1]
    for c0 in range(0, d_ff, FF_CHUNK):
        cs = slice(c0, c0 + FF_CHUNK)
        hg = sum(_dot(xp, wg_ref[k0:k0 + xp.shape[1], cs]) for xp, k0 in xparts)
        hu = sum(_dot(xp, wu_ref[k0:k0 + xp.shape[1], cs]) for xp, k0 in xparts)
        hh = (hg * jax.nn.sigmoid(hg) * hu).astype(BF16)
        contrib = _dot(hh, wd_ref[cs, :])
        if c0 == 0:
            acc_ref[...] = contrib
        else:
            acc_ref[...] += contrib


def _ffn_kernel(alpha, x_ref, wg_ref, wu_ref, wd_ref, g_ref, b_ref, o_ref, acc_ref):
    x = x_ref[...]
    _swiglu_acc([(x.astype(BF16), 0)], wg_ref, wu_ref, wd_ref, acc_ref)
    o_ref[...] = _layer_norm(alpha * x + acc_ref[...], g_ref[...], b_ref[...])


def _ffn_ln(x2, wg, wu, wd, g, b, alpha):
    t, d = x2.shape
    tm = min(TM_FFN, t)
    row = lambda i: (i, 0)
    const = lambda i: (0, 0)
    wspec = lambda a: pl.BlockSpec(a.shape, const, pipeline_mode=pl.Buffered(1))
    return pl.pallas_call(
        functools.partial(_ffn_kernel, alpha),
        grid=(t // tm,),
        in_specs=[pl.BlockSpec((tm, d), row), wspec(wg), wspec(wu), wspec(wd),
                  pl.BlockSpec((1, d), const), pl.BlockSpec((1, d), const)],
        out_specs=pl.BlockSpec((tm, d), row),
        out_shape=jax.ShapeDtypeStruct((t, d), F32),
        scratch_shapes=[pltpu.VMEM((tm, d), F32)],
        compiler_params=_cparams(("parallel",)),
        name="ffn_ln",
    )(x2, wg, wu, wd, g, b)


def _pack_halves(x):
    h = x.shape[1] // 2
    hi = lax.bitcast_convert_type(x[:, :h].astype(BF16).astype(F32), jnp.uint32)
    lo = lax.bitcast_convert_type(x[:, h:].astype(BF16).astype(F32), jnp.uint32)
    return hi | (lo >> 16)


def _unpack_halves(p):
    a = lax.bitcast_convert_type(p & jnp.uint32(0xFFFF0000), F32)
    b = lax.bitcast_convert_type(p << 16, F32)
    return a, b


def _router_kernel(x_ref, wr_ref, o_ref, xp_ref):
    x = x_ref[...]
    xp_ref[...] = _pack_halves(x)
    logits = jnp.dot(x, wr_ref[...], preferred_element_type=F32, precision=lax.Precision.HIGHEST)
    lane = lax.broadcasted_iota(jnp.int32, logits.shape, 1)
    lg = jnp.where(lane < N_EXPERTS, logits, -jnp.inf)
    m1 = jnp.max(lg, axis=-1, keepdims=True)
    i1 = jnp.min(jnp.where(lg == m1, lane, LANES), axis=-1, keepdims=True)
    lg2 = jnp.where(lane == i1, -jnp.inf, lg)
    m2 = jnp.max(lg2, axis=-1, keepdims=True)
    i2 = jnp.min(jnp.where(lg2 == m2, lane, LANES), axis=-1, keepdims=True)
    e = jnp.exp(m2 - m1)
    g1 = 1.0 / (1.0 + e)
    g2 = e / (1.0 + e)
    out = jnp.where(lane == 0, i1.astype(F32), 0.0)
    out = jnp.where(lane == 1, i2.astype(F32), out)
    out = jnp.where(lane == 2, g1, out)
    o_ref[...] = jnp.where(lane == 3, g2, out)


def _router(x2, wr):
    t, d = x2.shape
    tm = min(TM_PROJ, t)
    row = lambda i: (i, 0)
    return pl.pallas_call(
        _router_kernel,
        grid=(t // tm,),
        in_specs=[pl.BlockSpec((tm, d), row), pl.BlockSpec((d, LANES), lambda i: (0, 0))],
        out_specs=[pl.BlockSpec((tm, LANES), row), pl.BlockSpec((tm, d // 2), row)],
        out_shape=[jax.ShapeDtypeStruct((t, LANES), F32), jax.ShapeDtypeStruct((t, d // 2), jnp.uint32)],
        compiler_params=_cparams(("parallel",)),
        name="router",
    )(x2, wr)


def _sc_mesh():
    return plsc.VectorSubcoreMesh(core_axis_name="core", subcore_axis_name="subcore")


def _sc_index_rows(idx):
    win = idx.reshape(-1, SC_WIN)
    return jnp.concatenate([win, jnp.zeros((win.shape[0], SC_IDX_TILE - SC_WIN), idx.dtype)], axis=1)


def _sc_dispatch(xp, slot0, slot1, n_slot):
    t, c = xp.shape
    half = t // SC_WIN // 2
    idx_spec = pl.BlockSpec((1, SC_IDX_TILE), lambda cc, i: (cc * half + i, 0))

    @pl.kernel(out_type=jax.ShapeDtypeStruct((n_slot, c), xp.dtype), mesh=_sc_mesh(), scratch_types=[],
               name="moe_dispatch")
    def run(x_hbm, i0_hbm, i1_hbm, o_hbm):
        def body(x_vmem, i0_vmem, i1_vmem):
            pltpu.sync_copy(x_vmem, o_hbm.at[i0_vmem.at[0, pl.ds(0, SC_WIN)]])
            pltpu.sync_copy(x_vmem, o_hbm.at[i1_vmem.at[0, pl.ds(0, SC_WIN)]])

        pltpu.emit_pipeline(
            body,
            grid=(2, half),
            in_specs=[pl.BlockSpec((SC_WIN, c), lambda cc, i: (cc * half + i, 0)), idx_spec, idx_spec],
            out_specs=[],
            core_axis_name=("core", "subcore"),
            dimension_semantics=(pltpu.PARALLEL, pltpu.PARALLEL),
        )(x_hbm, i0_hbm, i1_hbm)

    return run(xp, _sc_index_rows(slot0), _sc_index_rows(slot1))


def _sc_gather(yp, idx):
    t = idx.shape[0]
    c = yp.shape[1]
    half = t // SC_WIN // 2

    @pl.kernel(out_type=jax.ShapeDtypeStruct((t, c), yp.dtype), mesh=_sc_mesh(), scratch_types=[],
               name="moe_gather")
    def run(y_hbm, i_hbm, o_hbm):
        def body(i_vmem, o_vmem):
            pltpu.sync_copy(y_hbm.at[i_vmem.at[0, pl.ds(0, SC_WIN)]], o_vmem)

        pltpu.emit_pipeline(
            body,
            grid=(2, half),
            in_specs=[pl.BlockSpec((1, SC_IDX_TILE), lambda cc, i: (cc * half + i, 0))],
            out_specs=[pl.BlockSpec((SC_WIN, c), lambda cc, i: (cc * half + i, 0))],
            core_axis_name=("core", "subcore"),
            dimension_semantics=(pltpu.PARALLEL, pltpu.PARALLEL),
        )(i_hbm, o_hbm)

    return run(yp, _sc_index_rows(idx))


def _expert_kernel(be_ref, nv_ref, x_ref, wg_ref, wu_ref, wd_ref, o_ref, acc_ref):
    i = pl.program_id(0)
    n_valid = nv_ref[i]

    @pl.when(n_valid > 0)
    def _():
        row = lax.broadcasted_iota(jnp.int32, x_ref.shape, 0)
        xa, xb = _unpack_halves(jnp.where(row < n_valid, x_ref[...], jnp.uint32(0)))
        h = xa.shape[1]
        _swiglu_acc([(xa.astype(BF16), 0), (xb.astype(BF16), h)], wg_ref, wu_ref, wd_ref, acc_ref)
        o_ref[...] = _pack_halves(acc_ref[...])

    @pl.when(n_valid <= 0)
    def _():
        o_ref[...] = jnp.zeros_like(o_ref)


def _experts(xbp, blk_e, n_valid, wg, wu, wd):
    n_slot, dh = xbp.shape
    d = 2 * dh
    n_blk = n_slot // MOE_BLK
    dff = wg.shape[-1]
    row = lambda i, be, nv: (i, 0)
    wmap = lambda i, be, nv: (be[i], 0, 0)
    return pl.pallas_call(
        _expert_kernel,
        grid_spec=pltpu.PrefetchScalarGridSpec(
            num_scalar_prefetch=2,
            grid=(n_blk,),
            in_specs=[pl.BlockSpec((MOE_BLK, dh), row),
                      pl.BlockSpec((None, d, dff), wmap, pipeline_mode=pl.Buffered(1)),
                      pl.BlockSpec((None, d, dff), wmap, pipeline_mode=pl.Buffered(1)),
                      pl.BlockSpec((None, dff, d), wmap, pipeline_mode=pl.Buffered(1))],
            out_specs=pl.BlockSpec((MOE_BLK, dh), row),
            scratch_shapes=[pltpu.VMEM((MOE_BLK, d), F32)],
        ),
        out_shape=jax.ShapeDtypeStruct((n_slot, dh), jnp.uint32),
        compiler_params=_cparams(("arbitrary",)),
        name="experts",
    )(blk_e, n_valid, xbp, wg, wu, wd)


def _combine_kernel(alpha, x_ref, y0_ref, y1_ref, r_ref, g_ref, b_ref, o_ref):
    r = r_ref[...]
    a0, b0 = _unpack_halves(y0_ref[...])
    a1, b1 = _unpack_halves(y1_ref[...])
    g0 = r[:, 2:3]
    g1 = r[:, 3:4]
    f = jnp.concatenate([g0 * a0 + g1 * a1, g0 * b0 + g1 * b1], axis=1)
    o_ref[...] = _layer_norm(alpha * x_ref[...] + f, g_ref[...], b_ref[...])


def _combine_ln(x2, y0, y1, route, g, b, alpha):
    t, d = x2.shape
    tm = min(TM_PROJ, t)
    row = lambda i: (i, 0)
    const = lambda i: (0, 0)
    return pl.pallas_call(
        functools.partial(_combine_kernel, alpha),
        grid=(t // tm,),
        in_specs=[pl.BlockSpec((tm, d), row), pl.BlockSpec((tm, d // 2), row), pl.BlockSpec((tm, d // 2), row),
                  pl.BlockSpec((tm, LANES), row), pl.BlockSpec((1, d), const), pl.BlockSpec((1, d), const)],
        out_specs=pl.BlockSpec((tm, d), row),
        out_shape=jax.ShapeDtypeStruct((t, d), F32),
        compiler_params=_cparams(("parallel",)),
        name="combine_ln",
    )(x2, y0, y1, route, g, b)


def _moe(x2, wr, wg, wu, wd, g, b, alpha):
    t, d = x2.shape
    route, xp = _router(x2, wr)
    top_idx = route[:, 0:2].astype(jnp.int32)
    n_asg = t * TOP_K
    flat_e = top_idx.reshape(n_asg)
    onehot = (flat_e[:, None] == jnp.arange(N_EXPERTS, dtype=jnp.int32)[None, :]).astype(jnp.int32)
    csum = jnp.cumsum(onehot, axis=0)
    rank = jnp.sum((csum - onehot) * onehot, axis=1)
    counts = csum[-1]
    padded = (counts + MOE_BLK - 1) // MOE_BLK * MOE_BLK
    pad_end = jnp.cumsum(padded)
    pad_start = pad_end - padded
    slot = (pad_start[flat_e] + rank).reshape(t, TOP_K)
    n_slot = -(-n_asg // MOE_BLK) * MOE_BLK + N_EXPERTS * MOE_BLK
    n_blk = n_slot // MOE_BLK
    blk_start = jnp.arange(n_blk, dtype=jnp.int32) * MOE_BLK
    blk_e = jnp.minimum(jnp.searchsorted(pad_end, blk_start, side='right'), N_EXPERTS - 1).astype(jnp.int32)
    n_valid = jnp.clip(pad_start[blk_e] + counts[blk_e] - blk_start, 0, MOE_BLK).astype(jnp.int32)
    slot0 = slot[:, 0]
    slot1 = slot[:, 1]
    xbp = _sc_dispatch(xp, slot0, slot1, n_slot)
    ybp = _experts(xbp, blk_e, n_valid, wg, wu, wd)
    return _combine_ln(x2, _sc_gather(ybp, slot0), _sc_gather(ybp, slot1), route, g, b, alpha)


def _t5_causal_bucket(dist):
    max_exact = REL_BUCKETS // 2
    d = jnp.maximum(dist, 1).astype(F32)
    large = max_exact + (jnp.log(d / max_exact) / math.log(REL_MAX_DIST / max_exact)
                         * (REL_BUCKETS - max_exact)).astype(jnp.int32)
    large = jnp.minimum(large, REL_BUCKETS - 1)
    return jnp.where(dist < max_exact, dist, large)


def _dn_constants(ts):
    lanes = np.arange(DN_W)
    ea = np.zeros((LANES, DN_W), np.float32)
    eb = np.zeros((LANES, DN_W), np.float32)
    ea[lanes // DN_D, lanes] = 1.0
    eb[DN_HEADS + lanes // DN_D, lanes] = 1.0
    seg = (lanes[:, None] // DN_D == lanes[None, :] // DN_D).astype(np.float32)
    r = np.arange(ts)
    same = r[:, None] // DN_CHUNK == r[None, :] // DN_CHUNK
    tri = (same & (r[:, None] >= r[None, :])).astype(np.float32)
    ones = same.astype(np.float32)
    return tuple(jnp.asarray(a, BF16) for a in (ea, eb, seg, tri, ones))


def kernel(x, w_in, w_out, conv_w, dn_a_log, dn_dt_bias, dn_norm_w, df_lambda, df_subln_w, rel_bias,
           ln1_g, ln1_b, ln2_g, ln2_b, ffn_w_gate, ffn_w_up, ffn_w_down, moe_router, moe_w_gate,
           moe_w_up, moe_w_down):
    bsz, seq, d = x.shape
    depth = w_in.shape[0]
    alpha = (2 * depth) ** 0.25
    t = bsz * seq
    tq = min(TQ_DF, seq)
    assert tq >= REL_MAX_DIST and seq % tq == 0 and seq % min(TS_DN, seq) == 0
    btiles = _attn_bias_tiles(rel_bias, seq, tq)

    dn_consts = _dn_constants(min(TS_DN, seq))
    c_dn = 3 * DN_W
    x2 = x.reshape(t, d)
    for layer in range(depth):
        lambda_init = 0.8 - 0.6 * math.exp(-0.3 * layer)
        wl = w_in[layer]
        n1 = c_dn + DN_W + 2 * DN_HEADS
        w1 = jnp.concatenate([wl[:, :n1], jnp.zeros((d, LANES - 2 * DN_HEADS), F32)], axis=1).astype(BF16)
        w2 = wl[:, n1:n1 + 2 * DF_W].astype(BF16)
        wvt = wl[:, n1 + 2 * DF_W:].T.astype(BF16)
        qkv, z, ab, qd, kd, vdt = _inproj(x2, w1, w2, wvt, bsz, seq)

        convw = jnp.concatenate([conv_w[layer], jnp.zeros((HALO - DN_CONV, c_dn), F32)], axis=0)
        par = jnp.zeros((HALO, DN_W), F32)
        par = par.at[0, 0:DN_HEADS].set(dn_a_log[layer])
        par = par.at[1, 0:DN_HEADS].set(dn_dt_bias[layer])
        par = par.at[2].set(jnp.tile(dn_norm_w[layer], DN_HEADS))
        o_dn = _deltanet(qkv, ab, z, convw, par, dn_consts, bsz, seq)

        lf = df_lambda[layer].astype(F32)
        lam = jnp.exp(jnp.sum(lf[0] * lf[1])) - jnp.exp(jnp.sum(lf[2] * lf[3])) + lambda_init
        scalars = jnp.stack([lam, jnp.asarray(1.0 - lambda_init, F32)]).astype(F32)
        o_df = _diff_attention(qd, kd, vdt, btiles, df_subln_w[layer].reshape(2 * DF_D, 1), scalars, bsz, seq)

        i = layer // 2
        g1 = ln1_g[layer].reshape(1, d)
        b1 = ln1_b[layer].reshape(1, d)
        g2 = ln2_g[layer].reshape(1, d)
        b2 = ln2_b[layer].reshape(1, d)
        x2 = _outproj_ln(x2, o_dn, o_df, w_out[layer].astype(BF16), g1, b1, alpha)
        if layer % 2 == 0:
            x2 = _ffn_ln(x2, ffn_w_gate[i].astype(BF16), ffn_w_up[i].astype(BF16), ffn_w_down[i].astype(BF16),
                         g2, b2, alpha)
        else:
            wr = jnp.concatenate([moe_router[i], jnp.zeros((d, LANES - N_EXPERTS), F32)], axis=1)
            x2 = _moe(x2, wr, moe_w_gate[i].astype(BF16), moe_w_up[i].astype(BF16), moe_w_down[i].astype(BF16),
                      g2, b2, alpha)
    return x2.reshape(bsz, seq, d)
```

```python
import functools
import math

import jax
import jax.numpy as jnp
import numpy as np
from jax import lax
from jax.experimental import pallas as pl
from jax.experimental.pallas import tpu as pltpu
from jax.experimental.pallas import tpu_sc as plsc

F32 = jnp.float32
BF16 = jnp.bfloat16

DN_HEADS = 8
DN_D = 64
DN_CONV = 4
DN_CHUNK = 64
DN_W = DN_HEADS * DN_D
DF_HEADS = 4
DF_D = 64
DF_W = DF_HEADS * 2 * DF_D
REL_BUCKETS = 32
REL_MAX_DIST = 128
N_EXPERTS = 8
TOP_K = 2
MOE_BLK = 512
LN_EPS = 1e-5
RMS_EPS = 1e-6
LOG2E = math.log2(math.e)

LANES = 128
HALO = 8
PAIR = 2 * DN_D
N_PAIRS = DN_HEADS // 2
VMEM_LIMIT = 56 * 1024 * 1024

TM_PROJ = 512
TS_DN = 256
DN_UNROLL = 4
TQ_DF = 1024
TM_FFN = 512
FF_CHUNK = 256
SC_WIN = 64
SC_IDX_TILE = 128


def _cparams(sem):
    return pltpu.CompilerParams(dimension_semantics=sem, vmem_limit_bytes=VMEM_LIMIT)


def _dot(a, b):
    return jnp.dot(a, b, preferred_element_type=F32)


def _dot_nt(a, b):
    return lax.dot_general(a, b, (((1,), (1,)), ((), ())), preferred_element_type=F32)


def _dot_tn(a, b):
    return lax.dot_general(a, b, (((0,), (0,)), ((), ())), preferred_element_type=F32)


def _split(x):
    hi = x.astype(BF16)
    lo = (x - hi.astype(F32)).astype(BF16)
    return hi, lo


def _mm_xc(x, c):
    hi, lo = _split(x)
    return _dot(hi, c) + _dot(lo, c)


def _mm_cx(c, x):
    hi, lo = _split(x)
    return _dot(c, hi) + _dot(c, lo)


def _mm3(a, b):
    ah, al = _split(a)
    bh, bl = _split(b)
    return _dot(ah, bh) + _dot(ah, bl) + _dot(al, bh)


def _layer_norm(v, g, b):
    mu = jnp.mean(v, axis=-1, keepdims=True)
    d = v - mu
    var = jnp.mean(d * d, axis=-1, keepdims=True)
    return d * lax.rsqrt(var + LN_EPS) * g + b


def _inproj_kernel(x_ref, w1_ref, w2_ref, wvt_ref, qkv_ref, z_ref, ab_ref, qd_ref, kd_ref, vd_ref):
    xb = x_ref[...].astype(BF16)
    c = 3 * DN_W
    qkv_ref[...] = _dot(xb, w1_ref[:, 0:c])
    z_ref[...] = _dot(xb, w1_ref[:, c:c + DN_W])
    ab_ref[...] = _dot(xb, w1_ref[:, c + DN_W:c + DN_W + LANES])
    qd_ref[...] = (_dot(xb, w2_ref[:, 0:DF_W]) * (DF_D ** -0.5 * LOG2E)).astype(BF16)
    kd_ref[...] = _dot(xb, w2_ref[:, DF_W:2 * DF_W]).astype(BF16)
    vd_ref[...] = _dot_nt(wvt_ref[...], xb).astype(BF16)


def _inproj(x2, w1, w2, wvt, bsz, seq):
    t, d = x2.shape
    tm = min(TM_PROJ, seq)
    nt = seq // tm
    n1, n2 = w1.shape[1], w2.shape[1]
    row = lambda i: (i, 0)
    const = lambda i: (0, 0)
    return pl.pallas_call(
        _inproj_kernel,
        grid=(t // tm,),
        in_specs=[pl.BlockSpec((tm, d), row),
                  pl.BlockSpec((d, n1), const, pipeline_mode=pl.Buffered(1)),
                  pl.BlockSpec((d, n2), const, pipeline_mode=pl.Buffered(1)),
                  pl.BlockSpec((DF_W, d), const, pipeline_mode=pl.Buffered(1))],
        out_specs=[pl.BlockSpec((tm, 3 * DN_W), row), pl.BlockSpec((tm, DN_W), row),
                   pl.BlockSpec((tm, LANES), row), pl.BlockSpec((tm, DF_W), row),
                   pl.BlockSpec((tm, DF_W), row),
                   pl.BlockSpec((None, DF_W, tm), lambda i: (i // nt, 0, i % nt))],
        out_shape=[jax.ShapeDtypeStruct((t, 3 * DN_W), F32), jax.ShapeDtypeStruct((t, DN_W), F32),
                   jax.ShapeDtypeStruct((t, LANES), F32), jax.ShapeDtypeStruct((t, DF_W), BF16),
                   jax.ShapeDtypeStruct((t, DF_W), BF16), jax.ShapeDtypeStruct((bsz, DF_W, seq), BF16)],
        compiler_params=_cparams(("parallel",)),
        name="inproj",
    )(x2, w1, w2, wvt)


def _dn_kernel(qkv_ref, halo_ref, ab_ref, z_ref, convw_ref, par_ref, ea_ref, eb_ref, seg_ref,
               tri_ref, ones_ref, o_ref,
               xe_s, q_s, k_s, kb_s, vb_s, gc_s, eg_s, kdec_s, gl_s, od_s, qe_s, m_s, n_s, state_s):
    i = pl.program_id(1)
    ts = qkv_ref.shape[0]
    n_chunks = ts // DN_CHUNK

    @pl.when(i == 0)
    def _():
        state_s[...] = jnp.zeros_like(state_s)

    halo = halo_ref[...]
    xe_s[0:HALO, :] = jnp.where(i > 0, halo, jnp.zeros_like(halo))
    xe_s[HALO:, :] = qkv_ref[...]
    xe = xe_s[...]
    y = convw_ref[DN_CONV - 1:DN_CONV, :] * xe[HALO:, :]
    for tap in range(DN_CONV - 1):
        y += convw_ref[tap:tap + 1, :] * pltpu.roll(xe, DN_CONV - 1 - tap, axis=0)[HALO:, :]
    y = y * jax.nn.sigmoid(y)
    q = y[:, 0:DN_W]
    k = y[:, DN_W:2 * DN_W]
    v = y[:, 2 * DN_W:3 * DN_W]

    seg = seg_ref[...]
    q = q * lax.rsqrt(_dot((q * q).astype(BF16), seg) + 1e-6) * (DN_D ** -0.5)
    k = k * lax.rsqrt(_dot((k * k).astype(BF16), seg) + 1e-6)

    ab = ab_ref[...]
    xa = ab + par_ref[1:2, 0:LANES]
    softplus = jnp.maximum(xa, 0.0) + jnp.log(1.0 + jnp.exp(-jnp.abs(xa)))
    g = _mm_xc(-jnp.exp(par_ref[0:1, 0:LANES]) * softplus, ea_ref[...])
    beta = _mm_xc(jax.nn.sigmoid(ab), eb_ref[...])
    gc = _mm_cx(tri_ref[...], g)
    gl = _mm_cx(ones_ref[...], g)
    eg = jnp.exp(gc)
    kb = k * beta
    q_s[...] = q
    k_s[...] = k
    kb_s[...] = kb
    vb_s[...] = v * beta
    gc_s[...] = gc
    eg_s[...] = eg
    kdec_s[...] = k * jnp.exp(gl - gc)
    gl_s[...] = jnp.exp(gl)

    lane = lax.broadcasted_iota(jnp.int32, (DN_CHUNK, PAIR), 1)
    rowi = lax.broadcasted_iota(jnp.int32, (DN_CHUNK, PAIR), 0)
    colj = jnp.where(lane >= DN_D, lane - DN_D, lane)
    even = lane < DN_D
    eye2 = rowi == colj
    lower = rowi >= colj
    lane_b = lax.broadcasted_iota(jnp.int32, (PAIR, PAIR), 1)
    row_b = lax.broadcasted_iota(jnp.int32, (PAIR, PAIR), 0)
    bdmask = (lane_b < DN_D) == (row_b < DN_D)

    def bd(xm):
        z0 = jnp.zeros_like(xm)
        return jnp.concatenate([jnp.where(even, xm, z0), jnp.where(even, z0, xm)], axis=0)

    eye_f = jnp.where(eye2, 1.0, 0.0)

    def local_body(cc, carry):
        chains = [(cc * DN_UNROLL + dc, p) for dc in range(DN_UNROLL) for p in range(N_PAIRS)]
        n = len(chains)
        rows = [pl.ds(pl.multiple_of(c * DN_CHUNK, DN_CHUNK), DN_CHUNK) for c, _ in chains]
        cols = [slice(p * PAIR, (p + 1) * PAIR) for _, p in chains]
        ld = lambda ref, i: ref[rows[i], cols[i]]

        aq = [_dot_nt(jnp.concatenate([ld(kb_s, i), ld(q_s, i)], axis=0).astype(BF16),
                      bd(ld(k_s, i).astype(BF16))) for i in range(n)]
        a_qk, l_m = [], []
        for i in range(n):
            gcc = ld(gc_s, i)
            gcj = jnp.sum(jnp.where(eye2, gcc, 0.0), axis=0, keepdims=True)
            dec = jnp.where(lower, jnp.exp(jnp.minimum(gcc - gcj, 0.0)), 0.0)
            a_qk.append((aq[i][DN_CHUNK:, :] * dec).astype(BF16))
            l_m.append(jnp.where(eye2, 0.0, aq[i][:DN_CHUNK, :] * dec))

        lhl = [_split(m) for m in l_m]
        x0 = [eye_f - m for m in l_m]
        yb = [h for h, _ in lhl]
        for _ in range(5):
            yb = [_dot(yb[i], bd(yb[i])).astype(BF16) for i in range(n)]
            x0 = [x0[i] + _dot(x0[i].astype(BF16), bd(yb[i])) for i in range(n)]
        xhl = [_split(m) for m in x0]
        res = []
        for i in range(n):
            bxh = bd(xhl[i][0])
            lx = _dot(lhl[i][0], bxh) + _dot(lhl[i][0], bd(xhl[i][1])) + _dot(lhl[i][1], bxh)
            res.append((eye_f - x0[i] - lx).astype(BF16))
        t_m = [x0[i] + _dot(xhl[i][0], bd(res[i])) for i in range(n)]

        uw = []
        for i in range(n):
            th, tl = _split(t_m[i])
            kbg = (ld(kb_s, i) * ld(eg_s, i)).astype(BF16)
            rhs = jnp.concatenate([bd(ld(vb_s, i).astype(BF16)), bd(kbg)], axis=1)
            uw.append((_dot(th, rhs) + _dot(tl, rhs)).astype(BF16))
        qo = [_dot(a_qk[i], jnp.concatenate([bd(uw[i][:, PAIR:]), bd(uw[i][:, :PAIR])], axis=1))
              for i in range(n)]
        mn = [_dot_tn(ld(kdec_s, i).astype(BF16), jnp.concatenate([uw[i][:, PAIR:], uw[i][:, :PAIR]], axis=1))
              for i in range(n)]
        for i, (c, p) in enumerate(chains):
            qe_s[rows[i], cols[i]] = (ld(q_s, i) * ld(eg_s, i) - qo[i][:, :PAIR]).astype(BF16)
            od_s[rows[i], cols[i]] = qo[i][:, PAIR:]
            m_s[c, p] = jnp.where(bdmask, mn[i][:, :PAIR], 0.0).astype(BF16)
            n_s[c, p] = jnp.where(bdmask, mn[i][:, PAIR:], 0.0)
        return carry

    lax.fori_loop(0, n_chunks // DN_UNROLL, local_body, 0)

    for c in range(n_chunks):
        rows = slice(c * DN_CHUNK, (c + 1) * DN_CHUNK)
        for p in range(N_PAIRS):
            cols = slice(p * PAIR, (p + 1) * PAIR)
            st = state_s[p]
            r = _dot(jnp.concatenate([qe_s[rows, cols], m_s[c, p]], axis=0), st.astype(BF16))
            od_s[rows, cols] += r[:DN_CHUNK, :]
            state_s[p] = gl_s[c * DN_CHUNK:c * DN_CHUNK + 1, cols] * st - r[DN_CHUNK:, :] + n_s[c, p]

    od = od_s[...]
    ms = _dot((od * od).astype(BF16), seg) * (1.0 / DN_D)
    zz = z_ref[...]
    o_ref[...] = (od * lax.rsqrt(ms + RMS_EPS) * par_ref[2:3, :] * (zz * jax.nn.sigmoid(zz))).astype(BF16)


def _deltanet(qkv, ab, z, convw, par, consts, bsz, seq):
    t = qkv.shape[0]
    ts = min(TS_DN, seq)
    nt = seq // ts
    hb = ts // HALO
    ea, eb, seg, tri, ones = consts
    row = lambda b, i: (b * nt + i, 0)
    const = lambda b, i: (0, 0)
    halo_map = lambda b, i: (jnp.maximum((b * nt + i) * hb - 1, 0), 0)
    cspec = lambda a: pl.BlockSpec(a.shape, const, pipeline_mode=pl.Buffered(1))
    big = lambda: pltpu.VMEM((ts, DN_W), F32)
    return pl.pallas_call(
        _dn_kernel,
        grid=(bsz, nt),
        in_specs=[pl.BlockSpec((ts, 3 * DN_W), row), pl.BlockSpec((HALO, 3 * DN_W), halo_map),
                  pl.BlockSpec((ts, LANES), row), pl.BlockSpec((ts, DN_W), row),
                  cspec(convw), cspec(par), cspec(ea), cspec(eb), cspec(seg), cspec(tri), cspec(ones)],
        out_specs=pl.BlockSpec((ts, DN_W), row),
        out_shape=jax.ShapeDtypeStruct((t, DN_W), BF16),
        scratch_shapes=[pltpu.VMEM((ts + HALO, 3 * DN_W), F32)] + [big() for _ in range(9)]
                       + [pltpu.VMEM((ts, DN_W), BF16),
                          pltpu.VMEM((ts // DN_CHUNK, N_PAIRS, PAIR, PAIR), BF16),
                          pltpu.VMEM((ts // DN_CHUNK, N_PAIRS, PAIR, PAIR), F32),
                          pltpu.VMEM((N_PAIRS, PAIR, PAIR), F32)],
        compiler_params=_cparams(("parallel", "arbitrary")),
        name="deltanet",
    )(qkv, qkv, ab, z, convw, par, ea, eb, seg, tri, ones)


def _df_kernel(sc_ref, q_ref, k_ref, vt_ref, bt_ref, w_ref, o_ref, sa_ref, sb_ref, acc_ref):
    qi = pl.program_id(2)
    tq = q_ref.shape[0]
    lane = lax.broadcasted_iota(jnp.int32, (tq, 2 * DF_D), 1)
    q = q_ref[...]
    zq = jnp.zeros_like(q)
    qs = (jnp.where(lane < DF_D, q, zq), jnp.where(lane < DF_D, zq, q))
    acc_ref[...] = jnp.zeros_like(acc_ref)

    def scores(j, s_ref):
        kk = k_ref[pl.ds(pl.multiple_of(j * tq, tq), tq), :]
        for mp in range(2):
            s_ref[mp] = _dot_nt(kk, qs[mp])

    def absorb(j, s_ref, bias, carry):
        vt = vt_ref[:, pl.ds(pl.multiple_of(j * tq, tq), tq)]
        out = []
        for mp in range(2):
            m_old, l_old = carry[mp]
            st = s_ref[mp]
            if bias is not None:
                st = st + bias
            m_new = jnp.maximum(m_old, jnp.max(st, axis=0, keepdims=True))
            alpha = jnp.exp2(m_old - m_new)
            pr = jnp.exp2(st - m_new)
            l_new = alpha * l_old + jnp.sum(pr, axis=0, keepdims=True)
            acc_ref[mp] = alpha * acc_ref[mp] + _dot(vt, pr.astype(BF16))
            out.append((m_new, l_new))
        return tuple(out)

    def absorb_prev(j, s_ref, carry):
        c0 = tq - LANES
        for mp in range(2):
            s_ref[mp, c0:, 0:LANES] += bt_ref[1, c0:, 0:LANES]
        return absorb(j, s_ref, None, carry)

    h2 = tq // 2
    d0 = pl.multiple_of(qi * tq, tq)

    def scores_diag(s_ref):
        k_top = k_ref[pl.ds(d0, h2), :]
        k_bot = k_ref[pl.ds(pl.multiple_of(d0 + h2, h2), h2), :]
        for mp in range(2):
            s_ref[mp, 0:h2, :] = _dot_nt(k_top, qs[mp])
            s_ref[mp, h2:, h2:] = _dot_nt(k_bot, qs[mp][h2:, :])

    def absorb_diag(s_ref, carry):
        vt_top = vt_ref[:, pl.ds(d0, h2)]
        vt_bot = vt_ref[:, pl.ds(pl.multiple_of(d0 + h2, h2), h2)]
        out = []
        for mp in range(2):
            m_old, l_old = carry[mp]
            top = s_ref[mp, 0:h2, :] + bt_ref[0, 0:h2, :]
            bot = s_ref[mp, h2:, h2:] + bt_ref[0, h2:, h2:]
            m_top = jnp.maximum(m_old, jnp.max(top, axis=0, keepdims=True))
            m_new = jnp.concatenate(
                [m_top[:, :h2], jnp.maximum(m_top[:, h2:], jnp.max(bot, axis=0, keepdims=True))], axis=1)
            alpha = jnp.exp2(m_old - m_new)
            p_top = jnp.exp2(top - m_new)
            p_bot = jnp.exp2(bot - m_new[:, h2:])
            l_bot = jnp.sum(p_bot, axis=0, keepdims=True)
            l_new = (alpha * l_old + jnp.sum(p_top, axis=0, keepdims=True)
                     + jnp.concatenate([jnp.zeros_like(l_bot), l_bot], axis=1))
            acc_ref[mp] = alpha * acc_ref[mp] + _dot(vt_top, p_top.astype(BF16))
            acc_ref[mp, :, h2:] += _dot(vt_bot, p_bot.astype(BF16))
            out.append((m_new, l_new))
        return tuple(out)

    init1 = (jnp.full((1, tq), -jnp.inf, F32), jnp.zeros((1, tq), F32))
    carry = (init1, init1)
    n_far = jnp.maximum(qi - 1, 0)
    odd = n_far % 2

    @pl.when(qi == 0)
    def _():
        scores_diag(sb_ref)

    @pl.when(jnp.logical_and(qi > 0, odd == 0))
    def _():
        scores(0, sa_ref)

    def odd_step(_, c):
        scores(0, sb_ref)
        scores(1, sa_ref)
        return absorb(0, sb_ref, None, c)

    carry = lax.fori_loop(0, odd, odd_step, carry)

    def pair_step(jp, c):
        j = odd + 2 * jp
        scores(j + 1, sb_ref)
        c = absorb(j, sa_ref, None, c)
        scores(j + 2, sa_ref)
        return absorb(j + 1, sb_ref, None, c)

    carry = lax.fori_loop(0, n_far // 2, pair_step, carry)

    def tail_step(j, c):
        scores_diag(sb_ref)
        c = absorb_prev(j, sa_ref, c)
        return absorb_diag(sb_ref, c)

    carry = lax.fori_loop(n_far, qi, tail_step, carry)
    carry = lax.fori_loop(0, jnp.where(qi == 0, 1, 0), lambda _, c: absorb_diag(sb_ref, c), carry)

    (_, l0), (_, l1) = carry
    ot = acc_ref[0] / l0 - sc_ref[0] * (acc_ref[1] / l1)
    ms = jnp.mean(ot * ot, axis=0, keepdims=True)
    ot = ot * lax.rsqrt(ms + RMS_EPS) * w_ref[...] * sc_ref[1]
    o_ref[...] = ot.T.astype(BF16)


def _attn_bias_tiles(rel_bias, seq, tq):
    rb = rel_bias.astype(F32)
    far = rb[_t5_causal_bucket(jnp.asarray(seq - 1, jnp.int32))]
    kj = jnp.arange(tq, dtype=jnp.int32)[:, None]
    qi = jnp.arange(tq, dtype=jnp.int32)[None, :]
    dist = jnp.stack([qi - kj, qi + tq - kj])
    onehot = (_t5_causal_bucket(jnp.maximum(dist, 0))[..., None]
              == jnp.arange(REL_BUCKETS, dtype=jnp.int32)).astype(F32)
    vals = (jnp.einsum('ntqb,bh->ntqh', onehot, rb, precision=lax.Precision.HIGHEST) - far) * LOG2E
    vals = jnp.where((dist >= 0)[..., None], vals, -jnp.inf)
    return jnp.transpose(vals, (3, 0, 1, 2))


def _diff_attention(qd, kd, vdt, btiles, subln_w, scalars, bsz, seq):
    t = qd.shape[0]
    tq = min(TQ_DF, seq)
    nq = seq // tq
    sbuf = pltpu.VMEM((2, tq, tq), F32)
    return pl.pallas_call(
        _df_kernel,
        grid_spec=pltpu.PrefetchScalarGridSpec(
            num_scalar_prefetch=1,
            grid=(bsz, DF_HEADS, nq),
            in_specs=[pl.BlockSpec((tq, 2 * DF_D), lambda b, h, i, sc: (b * nq + i, h)),
                      pl.BlockSpec((seq, 2 * DF_D), lambda b, h, i, sc: (b, h)),
                      pl.BlockSpec((None, 2 * DF_D, seq), lambda b, h, i, sc: (b, h, 0)),
                      pl.BlockSpec((None, 2, tq, tq), lambda b, h, i, sc: (h, 0, 0, 0)),
                      pl.BlockSpec((2 * DF_D, 1), lambda b, h, i, sc: (0, 0))],
            out_specs=pl.BlockSpec((tq, 2 * DF_D), lambda b, h, i, sc: (b * nq + i, h)),
            scratch_shapes=[sbuf, sbuf, pltpu.VMEM((2, 2 * DF_D, tq), F32)],
        ),
        out_shape=jax.ShapeDtypeStruct((t, DF_W), BF16),
        compiler_params=_cparams(("parallel", "parallel", "arbitrary")),
        name="diffattn",
    )(scalars, qd, kd, vdt, btiles, subln_w)


def _outproj_kernel(alpha, x_ref, a_ref, b_ref, w_ref, g_ref, bb_ref, o_ref):
    mix = _dot(a_ref[...], w_ref[0:DN_W, :]) + _dot(b_ref[...], w_ref[DN_W:, :])
    o_ref[...] = _layer_norm(alpha * x_ref[...] + mix, g_ref[...], bb_ref[...])


def _outproj_ln(x2, o_dn, o_df, w_out, g, b, alpha):
    t, d = x2.shape
    tm = min(TM_PROJ, t)
    row = lambda i: (i, 0)
    const = lambda i: (0, 0)
    return pl.pallas_call(
        functools.partial(_outproj_kernel, alpha),
        grid=(t // tm,),
        in_specs=[pl.BlockSpec((tm, d), row), pl.BlockSpec((tm, DN_W), row), pl.BlockSpec((tm, DF_W), row),
                  pl.BlockSpec(w_out.shape, const, pipeline_mode=pl.Buffered(1)),
                  pl.BlockSpec((1, d), const), pl.BlockSpec((1, d), const)],
        out_specs=pl.BlockSpec((tm, d), row),
        out_shape=jax.ShapeDtypeStruct((t, d), F32),
        compiler_params=_cparams(("parallel",)),
        name="outproj_ln",
    )(x2, o_dn, o_df, w_out, g, b)


def _swiglu_acc(xparts, wg_ref, wu_ref, wd_ref, acc_ref):
    d_ff = wg_ref.shape[-1]
    for c0 in range(0, d_ff, FF_CHUNK):
        cs = slice(c0, c0 + FF_CHUNK)
        hg = sum(_dot(xp, wg_ref[k0:k0 + xp.shape[1], cs]) for xp, k0 in xparts)
        hu = sum(_dot(xp, wu_ref[k0:k0 + xp.shape[1], cs]) for xp, k0 in xparts)
        hh = (hg * jax.nn.sigmoid(hg) * hu).astype(BF16)
        contrib = _dot(hh, wd_ref[cs, :])
        if c0 == 0:
            acc_ref[...] = contrib
        else:
            acc_ref[...] += contrib


def _ffn_kernel(alpha, x_ref, wg_ref, wu_ref, wd_ref, g_ref, b_ref, o_ref, acc_ref):
    x = x_ref[...]
    _swiglu_acc([(x.astype(BF16), 0)], wg_ref, wu_ref, wd_ref, acc_ref)
    o_ref[...] = _layer_norm(alpha * x + acc_ref[...], g_ref[...], b_ref[...])


def _ffn_ln(x2, wg, wu, wd, g, b, alpha):
    t, d = x2.shape
    tm = min(TM_FFN, t)
    row = lambda i: (i, 0)
    const = lambda i: (0, 0)
    wspec = lambda a: pl.BlockSpec(a.shape, const, pipeline_mode=pl.Buffered(1))
    return pl.pallas_call(
        functools.partial(_ffn_kernel, alpha),
        grid=(t // tm,),
        in_specs=[pl.BlockSpec((tm, d), row), wspec(wg), wspec(wu), wspec(wd),
                  pl.BlockSpec((1, d), const), pl.BlockSpec((1, d), const)],
        out_specs=pl.BlockSpec((tm, d), row),
        out_shape=jax.ShapeDtypeStruct((t, d), F32),
        scratch_shapes=[pltpu.VMEM((tm, d), F32)],
        compiler_params=_cparams(("parallel",)),
        name="ffn_ln",
    )(x2, wg, wu, wd, g, b)


def _pack_halves(x):
    h = x.shape[1] // 2
    hi = lax.bitcast_convert_type(x[:, :h].astype(BF16).astype(F32), jnp.uint32)
    lo = lax.bitcast_convert_type(x[:, h:].astype(BF16).astype(F32), jnp.uint32)
    return hi | (lo >> 16)


def _unpack_halves(p):
    a = lax.bitcast_convert_type(p & jnp.uint32(0xFFFF0000), F32)
    b = lax.bitcast_convert_type(p << 16, F32)
    return a, b


def _router_kernel(x_ref, wr_ref, o_ref, xp_ref):
    x = x_ref[...]
    xp_ref[...] = _pack_halves(x)
    logits = jnp.dot(x, wr_ref[...], preferred_element_type=F32, precision=lax.Precision.HIGHEST)
    lane = lax.broadcasted_iota(jnp.int32, logits.shape, 1)
    lg = jnp.where(lane < N_EXPERTS, logits, -jnp.inf)
    m1 = jnp.max(lg, axis=-1, keepdims=True)
    i1 = jnp.min(jnp.where(lg == m1, lane, LANES), axis=-1, keepdims=True)
    lg2 = jnp.where(lane == i1, -jnp.inf, lg)
    m2 = jnp.max(lg2, axis=-1, keepdims=True)
    i2 = jnp.min(jnp.where(lg2 == m2, lane, LANES), axis=-1, keepdims=True)
    e = jnp.exp(m2 - m1)
    g1 = 1.0 / (1.0 + e)
    g2 = e / (1.0 + e)
    out = jnp.where(lane == 0, i1.astype(F32), 0.0)
    out = jnp.where(lane == 1, i2.astype(F32), out)
    out = jnp.where(lane == 2, g1, out)
    o_ref[...] = jnp.where(lane == 3, g2, out)


def _router(x2, wr):
    t, d = x2.shape
    tm = min(TM_PROJ, t)
    row = lambda i: (i, 0)
    return pl.pallas_call(
        _router_kernel,
        grid=(t // tm,),
        in_specs=[pl.BlockSpec((tm, d), row), pl.BlockSpec((d, LANES), lambda i: (0, 0))],
        out_specs=[pl.BlockSpec((tm, LANES), row), pl.BlockSpec((tm, d // 2), row)],
        out_shape=[jax.ShapeDtypeStruct((t, LANES), F32), jax.ShapeDtypeStruct((t, d // 2), jnp.uint32)],
        compiler_params=_cparams(("parallel",)),
        name="router",
    )(x2, wr)


def _sc_mesh():
    return plsc.VectorSubcoreMesh(core_axis_name="core", subcore_axis_name="subcore")


def _sc_index_rows(idx):
    win = idx.reshape(-1, SC_WIN)
    return jnp.concatenate([win, jnp.zeros((win.shape[0], SC_IDX_TILE - SC_WIN), idx.dtype)], axis=1)


def _sc_dispatch(xp, slot0, slot1, n_slot):
    t, c = xp.shape
    half = t // SC_WIN // 2
    idx_spec = pl.BlockSpec((1, SC_IDX_TILE), lambda cc, i: (cc * half + i, 0))

    @pl.kernel(out_type=jax.ShapeDtypeStruct((n_slot, c), xp.dtype), mesh=_sc_mesh(), scratch_types=[],
               name="moe_dispatch")
    def run(x_hbm, i0_hbm, i1_hbm, o_hbm):
        def body(x_vmem, i0_vmem, i1_vmem):
            pltpu.sync_copy(x_vmem, o_hbm.at[i0_vmem.at[0, pl.ds(0, SC_WIN)]])
            pltpu.sync_copy(x_vmem, o_hbm.at[i1_vmem.at[0, pl.ds(0, SC_WIN)]])

        pltpu.emit_pipeline(
            body,
            grid=(2, half),
            in_specs=[pl.BlockSpec((SC_WIN, c), lambda cc, i: (cc * half + i, 0)), idx_spec, idx_spec],
            out_specs=[],
            core_axis_name=("core", "subcore"),
            dimension_semantics=(pltpu.PARALLEL, pltpu.PARALLEL),
        )(x_hbm, i0_hbm, i1_hbm)

    return run(xp, _sc_index_rows(slot0), _sc_index_rows(slot1))


def _sc_gather(yp, idx):
    t = idx.shape[0]
    c = yp.shape[1]
    half = t // SC_WIN // 2

    @pl.kernel(out_type=jax.ShapeDtypeStruct((t, c), yp.dtype), mesh=_sc_mesh(), scratch_types=[],
               name="moe_gather")
    def run(y_hbm, i_hbm, o_hbm):
        def body(i_vmem, o_vmem):
            pltpu.sync_copy(y_hbm.at[i_vmem.at[0, pl.ds(0, SC_WIN)]], o_vmem)

        pltpu.emit_pipeline(
            body,
            grid=(2, half),
            in_specs=[pl.BlockSpec((1, SC_IDX_TILE), lambda cc, i: (cc * half + i, 0))],
            out_specs=[pl.BlockSpec((SC_WIN, c), lambda cc, i: (cc * half + i, 0))],
            core_axis_name=("core", "subcore"),
            dimension_semantics=(pltpu.PARALLEL, pltpu.PARALLEL),
        )(i_hbm, o_hbm)

    return run(yp, _sc_index_rows(idx))


def _expert_kernel(be_ref, nv_ref, x_ref, wg_ref, wu_ref, wd_ref, o_ref, acc_ref):
    i = pl.program_id(0)
    n_valid = nv_ref[i]

    @pl.when(n_valid > 0)
    def _():
        row = lax.broadcasted_iota(jnp.int32, x_ref.shape, 0)
        xa, xb = _unpack_halves(jnp.where(row < n_valid, x_ref[...], jnp.uint32(0)))
        h = xa.shape[1]
        _swiglu_acc([(xa.astype(BF16), 0), (xb.astype(BF16), h)], wg_ref, wu_ref, wd_ref, acc_ref)
        o_ref[...] = _pack_halves(acc_ref[...])

    @pl.when(n_valid <= 0)
    def _():
        o_ref[...] = jnp.zeros_like(o_ref)


def _experts(xbp, blk_e, n_valid, wg, wu, wd):
    n_slot, dh = xbp.shape
    d = 2 * dh
    n_blk = n_slot // MOE_BLK
    dff = wg.shape[-1]
    row = lambda i, be, nv: (i, 0)
    wmap = lambda i, be, nv: (be[i], 0, 0)
    return pl.pallas_call(
        _expert_kernel,
        grid_spec=pltpu.PrefetchScalarGridSpec(
            num_scalar_prefetch=2,
            grid=(n_blk,),
            in_specs=[pl.BlockSpec((MOE_BLK, dh), row),
                      pl.BlockSpec((None, d, dff), wmap, pipeline_mode=pl.Buffered(1)),
                      pl.BlockSpec((None, d, dff), wmap, pipeline_mode=pl.Buffered(1)),
                      pl.BlockSpec((None, dff, d), wmap, pipeline_mode=pl.Buffered(1))],
            out_specs=pl.BlockSpec((MOE_BLK, dh), row),
            scratch_shapes=[pltpu.VMEM((MOE_BLK, d), F32)],
        ),
        out_shape=jax.ShapeDtypeStruct((n_slot, dh), jnp.uint32),
        compiler_params=_cparams(("arbitrary",)),
        name="experts",
    )(blk_e, n_valid, xbp, wg, wu, wd)


def _combine_kernel(alpha, x_ref, y0_ref, y1_ref, r_ref, g_ref, b_ref, o_ref):
    r = r_ref[...]
    a0, b0 = _unpack_halves(y0_ref[...])
    a1, b1 = _unpack_halves(y1_ref[...])
    g0 = r[:, 2:3]
    g1 = r[:, 3:4]
    f = jnp.concatenate([g0 * a0 + g1 * a1, g0 * b0 + g1 * b1], axis=1)
    o_ref[...] = _layer_norm(alpha * x_ref[...] + f, g_ref[...], b_ref[...])


def _combine_ln(x2, y0, y1, route, g, b, alpha):
    t, d = x2.shape
    tm = min(TM_PROJ, t)
    row = lambda i: (i, 0)
    const = lambda i: (0, 0)
    return pl.pallas_call(
        functools.partial(_combine_kernel, alpha),
        grid=(t // tm,),
        in_specs=[pl.BlockSpec((tm, d), row), pl.BlockSpec((tm, d // 2), row), pl.BlockSpec((tm, d // 2), row),
                  pl.BlockSpec((tm, LANES), row), pl.BlockSpec((1, d), const), pl.BlockSpec((1, d), const)],
        out_specs=pl.BlockSpec((tm, d), row),
        out_shape=jax.ShapeDtypeStruct((t, d), F32),
        compiler_params=_cparams(("parallel",)),
        name="combine_ln",
    )(x2, y0, y1, route, g, b)


def _moe(x2, wr, wg, wu, wd, g, b, alpha):
    t, d = x2.shape
    route, xp = _router(x2, wr)
    top_idx = route[:, 0:2].astype(jnp.int32)
    n_asg = t * TOP_K
    flat_e = top_idx.reshape(n_asg)
    onehot = (flat_e[:, None] == jnp.arange(N_EXPERTS, dtype=jnp.int32)[None, :]).astype(jnp.int32)
    csum = jnp.cumsum(onehot, axis=0)
    rank = jnp.sum((csum - onehot) * onehot, axis=1)
    counts = csum[-1]
    padded = (counts + MOE_BLK - 1) // MOE_BLK * MOE_BLK
    pad_end = jnp.cumsum(padded)
    pad_start = pad_end - padded
    slot = (pad_start[flat_e] + rank).reshape(t, TOP_K)
    n_slot = -(-n_asg // MOE_BLK) * MOE_BLK + N_EXPERTS * MOE_BLK
    n_blk = n_slot // MOE_BLK
    blk_start = jnp.arange(n_blk, dtype=jnp.int32) * MOE_BLK
    blk_e = jnp.minimum(jnp.searchsorted(pad_end, blk_start, side='right'), N_EXPERTS - 1).astype(jnp.int32)
    n_valid = jnp.clip(pad_start[blk_e] + counts[blk_e] - blk_start, 0, MOE_BLK).astype(jnp.int32)
    slot0 = slot[:, 0]
    slot1 = slot[:, 1]
    xbp = _sc_dispatch(xp, slot0, slot1, n_slot)
    ybp = _experts(xbp, blk_e, n_valid, wg, wu, wd)
    return _combine_ln(x2, _sc_gather(ybp, slot0), _sc_gather(ybp, slot1), route, g, b, alpha)


def _t5_causal_bucket(dist):
    max_exact = REL_BUCKETS // 2
    d = jnp.maximum(dist, 1).astype(F32)
    large = max_exact + (jnp.log(d / max_exact) / math.log(REL_MAX_DIST / max_exact)
                         * (REL_BUCKETS - max_exact)).astype(jnp.int32)
    large = jnp.minimum(large, REL_BUCKETS - 1)
    return jnp.where(dist < max_exact, dist, large)


def _dn_constants(ts):
    lanes = np.arange(DN_W)
    ea = np.zeros((LANES, DN_W), np.float32)
    eb = np.zeros((LANES, DN_W), np.float32)
    ea[lanes // DN_D, lanes] = 1.0
    eb[DN_HEADS + lanes // DN_D, lanes] = 1.0
    seg = (lanes[:, None] // DN_D == lanes[None, :] // DN_D).astype(np.float32)
    r = np.arange(ts)
    same = r[:, None] // DN_CHUNK == r[None, :] // DN_CHUNK
    tri = (same & (r[:, None] >= r[None, :])).astype(np.float32)
    ones = same.astype(np.float32)
    return tuple(jnp.asarray(a, BF16) for a in (ea, eb, seg, tri, ones))


def kernel(x, w_in, w_out, conv_w, dn_a_log, dn_dt_bias, dn_norm_w, df_lambda, df_subln_w, rel_bias,
           ln1_g, ln1_b, ln2_g, ln2_b, ffn_w_gate, ffn_w_up, ffn_w_down, moe_router, moe_w_gate,
           moe_w_up, moe_w_down):
    bsz, seq, d = x.shape
    depth = w_in.shape[0]
    alpha = (2 * depth) ** 0.25
    t = bsz * seq
    tq = min(TQ_DF, seq)
    assert REL_MAX_DIST <= LANES and tq % (2 * LANES) == 0 and seq % tq == 0 and seq % min(TS_DN, seq) == 0
    btiles = _attn_bias_tiles(rel_bias, seq, tq)

    dn_consts = _dn_constants(min(TS_DN, seq))
    c_dn = 3 * DN_W
    x2 = x.reshape(t, d)
    for layer in range(depth):
        lambda_init = 0.8 - 0.6 * math.exp(-0.3 * layer)
        wl = w_in[layer]
        n1 = c_dn + DN_W + 2 * DN_HEADS
        w1 = jnp.concatenate([wl[:, :n1], jnp.zeros((d, LANES - 2 * DN_HEADS), F32)], axis=1).astype(BF16)
        w2 = wl[:, n1:n1 + 2 * DF_W].astype(BF16)
        wvt = wl[:, n1 + 2 * DF_W:].T.astype(BF16)
        qkv, z, ab, qd, kd, vdt = _inproj(x2, w1, w2, wvt, bsz, seq)

        convw = jnp.concatenate([conv_w[layer], jnp.zeros((HALO - DN_CONV, c_dn), F32)], axis=0)
        par = jnp.zeros((HALO, DN_W), F32)
        par = par.at[0, 0:DN_HEADS].set(dn_a_log[layer])
        par = par.at[1, 0:DN_HEADS].set(dn_dt_bias[layer])
        par = par.at[2].set(jnp.tile(dn_norm_w[layer], DN_HEADS))
        o_dn = _deltanet(qkv, ab, z, convw, par, dn_consts, bsz, seq)

        lf = df_lambda[layer].astype(F32)
        lam = jnp.exp(jnp.sum(lf[0] * lf[1])) - jnp.exp(jnp.sum(lf[2] * lf[3])) + lambda_init
        scalars = jnp.stack([lam, jnp.asarray(1.0 - lambda_init, F32)]).astype(F32)
        o_df = _diff_attention(qd, kd, vdt, btiles, df_subln_w[layer].reshape(2 * DF_D, 1), scalars, bsz, seq)

        i = layer // 2
        g1 = ln1_g[layer].reshape(1, d)
        b1 = ln1_b[layer].reshape(1, d)
        g2 = ln2_g[layer].reshape(1, d)
        b2 = ln2_b[layer].reshape(1, d)
        x2 = _outproj_ln(x2, o_dn, o_df, w_out[layer].astype(BF16), g1, b1, alpha)
        if layer % 2 == 0:
            x2 = _ffn_ln(x2, ffn_w_gate[i].astype(BF16), ffn_w_up[i].astype(BF16), ffn_w_down[i].astype(BF16),
                         g2, b2, alpha)
        else:
            wr = jnp.concatenate([moe_router[i], jnp.zeros((d, LANES - N_EXPERTS), F32)], axis=1)
            x2 = _moe(x2, wr, moe_w_gate[i].astype(BF16), moe_w_up[i].astype(BF16), moe_w_down[i].astype(BF16),
                      g2, b2, alpha)
    return x2.reshape(bsz, seq, d)
```

```python
import functools
import math

import jax
import jax.numpy as jnp
import numpy as np
from jax import lax
from jax.experimental import pallas as pl
from jax.experimental.pallas import tpu as pltpu
from jax.experimental.pallas import tpu_sc as plsc

F32 = jnp.float32
BF16 = jnp.bfloat16

DN_HEADS = 8
DN_D = 64
DN_CONV = 4
DN_CHUNK = 64
DN_W = DN_HEADS * DN_D
DF_HEADS = 4
DF_D = 64
DF_W = DF_HEADS * 2 * DF_D
REL_BUCKETS = 32
REL_MAX_DIST = 128
N_EXPERTS = 8
TOP_K = 2
MOE_BLK = 512
LN_EPS = 1e-5
RMS_EPS = 1e-6
LOG2E = math.log2(math.e)

LANES = 128
HALO = 8
PAIR = 2 * DN_D
N_PAIRS = DN_HEADS // 2
VMEM_LIMIT = 56 * 1024 * 1024

TM_PROJ = 512
TS_DN = 256
DN_UNROLL = 4
TQ_DF = 1024
TM_FFN = 512
FF_CHUNK = 256
SC_WIN = 64
SC_IDX_TILE = 128


def _cparams(sem):
    return pltpu.CompilerParams(dimension_semantics=sem, vmem_limit_bytes=VMEM_LIMIT)


def _dot(a, b):
    return jnp.dot(a, b, preferred_element_type=F32)


def _dot_nt(a, b):
    return lax.dot_general(a, b, (((1,), (1,)), ((), ())), preferred_element_type=F32)


def _dot_tn(a, b):
    return lax.dot_general(a, b, (((0,), (0,)), ((), ())), preferred_element_type=F32)


def _split(x):
    hi = x.astype(BF16)
    lo = (x - hi.astype(F32)).astype(BF16)
    return hi, lo


def _mm_xc(x, c):
    hi, lo = _split(x)
    return _dot(hi, c) + _dot(lo, c)


def _mm_cx(c, x):
    hi, lo = _split(x)
    return _dot(c, hi) + _dot(c, lo)


def _mm3(a, b):
    ah, al = _split(a)
    bh, bl = _split(b)
    return _dot(ah, bh) + _dot(ah, bl) + _dot(al, bh)


def _layer_norm(v, g, b):
    mu = jnp.mean(v, axis=-1, keepdims=True)
    d = v - mu
    var = jnp.mean(d * d, axis=-1, keepdims=True)
    return d * lax.rsqrt(var + LN_EPS) * g + b


def _inproj_kernel(x_ref, w1_ref, w2_ref, wvt_ref, qkv_ref, z_ref, ab_ref, qd_ref, kd_ref, vd_ref):
    xb = x_ref[...].astype(BF16)
    c = 3 * DN_W
    qkv_ref[...] = _dot(xb, w1_ref[:, 0:c])
    z_ref[...] = _dot(xb, w1_ref[:, c:c + DN_W])
    ab_ref[...] = _dot(xb, w1_ref[:, c + DN_W:c + DN_W + LANES])
    qd_ref[...] = (_dot(xb, w2_ref[:, 0:DF_W]) * (DF_D ** -0.5 * LOG2E)).astype(BF16)
    kd_ref[...] = _dot(xb, w2_ref[:, DF_W:2 * DF_W]).astype(BF16)
    vd_ref[...] = _dot_nt(wvt_ref[...], xb).astype(BF16)


def _inproj(x2, w1, w2, wvt, bsz, seq):
    t, d = x2.shape
    tm = min(TM_PROJ, seq)
    nt = seq // tm
    n1, n2 = w1.shape[1], w2.shape[1]
    row = lambda i: (i, 0)
    const = lambda i: (0, 0)
    return pl.pallas_call(
        _inproj_kernel,
        grid=(t // tm,),
        in_specs=[pl.BlockSpec((tm, d), row),
                  pl.BlockSpec((d, n1), const, pipeline_mode=pl.Buffered(1)),
                  pl.BlockSpec((d, n2), const, pipeline_mode=pl.Buffered(1)),
                  pl.BlockSpec((DF_W, d), const, pipeline_mode=pl.Buffered(1))],
        out_specs=[pl.BlockSpec((tm, 3 * DN_W), row), pl.BlockSpec((tm, DN_W), row),
                   pl.BlockSpec((tm, LANES), row), pl.BlockSpec((tm, DF_W), row),
                   pl.BlockSpec((tm, DF_W), row),
                   pl.BlockSpec((None, DF_W, tm), lambda i: (i // nt, 0, i % nt))],
        out_shape=[jax.ShapeDtypeStruct((t, 3 * DN_W), F32), jax.ShapeDtypeStruct((t, DN_W), F32),
                   jax.ShapeDtypeStruct((t, LANES), F32), jax.ShapeDtypeStruct((t, DF_W), BF16),
                   jax.ShapeDtypeStruct((t, DF_W), BF16), jax.ShapeDtypeStruct((bsz, DF_W, seq), BF16)],
        compiler_params=_cparams(("parallel",)),
        name="inproj",
    )(x2, w1, w2, wvt)


def _dn_kernel(qkv_ref, halo_ref, ab_ref, z_ref, convw_ref, par_ref, ea_ref, eb_ref, seg_ref,
               tri_ref, ones_ref, o_ref,
               xe_s, q_s, k_s, kb_s, vb_s, gc_s, eg_s, kdec_s, gl_s, od_s, qe_s, m_s, n_s, state_s):
    i = pl.program_id(1)
    ts = qkv_ref.shape[0]
    n_chunks = ts // DN_CHUNK

    @pl.when(i == 0)
    def _():
        state_s[...] = jnp.zeros_like(state_s)

    halo = halo_ref[...]
    xe_s[0:HALO, :] = jnp.where(i > 0, halo, jnp.zeros_like(halo))
    xe_s[HALO:, :] = qkv_ref[...]
    xe = xe_s[...]
    y = convw_ref[DN_CONV - 1:DN_CONV, :] * xe[HALO:, :]
    for tap in range(DN_CONV - 1):
        y += convw_ref[tap:tap + 1, :] * pltpu.roll(xe, DN_CONV - 1 - tap, axis=0)[HALO:, :]
    y = y * jax.nn.sigmoid(y)
    q = y[:, 0:DN_W]
    k = y[:, DN_W:2 * DN_W]
    v = y[:, 2 * DN_W:3 * DN_W]

    seg = seg_ref[...]
    q = q * lax.rsqrt(_dot((q * q).astype(BF16), seg) + 1e-6) * (DN_D ** -0.5)
    k = k * lax.rsqrt(_dot((k * k).astype(BF16), seg) + 1e-6)

    ab = ab_ref[...]
    xa = ab + par_ref[1:2, 0:LANES]
    softplus = jnp.maximum(xa, 0.0) + jnp.log(1.0 + jnp.exp(-jnp.abs(xa)))
    g = _mm_xc(-jnp.exp(par_ref[0:1, 0:LANES]) * softplus, ea_ref[...])
    beta = _mm_xc(jax.nn.sigmoid(ab), eb_ref[...])
    gc = _mm_cx(tri_ref[...], g)
    gl = _mm_cx(ones_ref[...], g)
    eg = jnp.exp(gc)
    kb = k * beta
    q_s[...] = q
    k_s[...] = k
    kb_s[...] = kb
    vb_s[...] = v * beta
    gc_s[...] = gc
    eg_s[...] = eg
    kdec_s[...] = k * jnp.exp(gl - gc)
    gl_s[...] = jnp.exp(gl)

    lane = lax.broadcasted_iota(jnp.int32, (DN_CHUNK, PAIR), 1)
    rowi = lax.broadcasted_iota(jnp.int32, (DN_CHUNK, PAIR), 0)
    colj = jnp.where(lane >= DN_D, lane - DN_D, lane)
    even = lane < DN_D
    eye2 = rowi == colj
    lower = rowi >= colj
    lane_b = lax.broadcasted_iota(jnp.int32, (PAIR, PAIR), 1)
    row_b = lax.broadcasted_iota(jnp.int32, (PAIR, PAIR), 0)
    bdmask = (lane_b < DN_D) == (row_b < DN_D)

    def bd(xm):
        z0 = jnp.zeros_like(xm)
        return jnp.concatenate([jnp.where(even, xm, z0), jnp.where(even, z0, xm)], axis=0)

    eye_f = jnp.where(eye2, 1.0, 0.0)
    sub_masks = []
    s = 1
    while s < DN_CHUNK:
        same = (rowi // (2 * s)) == (colj // (2 * s))
        sub_masks.append(same & ((rowi // s) % 2 == 1) & ((colj // s) % 2 == 0))
        s *= 2

    def local_body(cc, carry):
        chains = [(cc * DN_UNROLL + dc, p) for dc in range(DN_UNROLL) for p in range(N_PAIRS)]
        n = len(chains)
        rows = [pl.ds(pl.multiple_of(c * DN_CHUNK, DN_CHUNK), DN_CHUNK) for c, _ in chains]
        cols = [slice(p * PAIR, (p + 1) * PAIR) for _, p in chains]
        ld = lambda ref, i: ref[rows[i], cols[i]]

        aq = [_dot_nt(jnp.concatenate([ld(kb_s, i), ld(q_s, i)], axis=0).astype(BF16),
                      bd(ld(k_s, i).astype(BF16))) for i in range(n)]
        a_qk, l_m = [], []
        for i in range(n):
            gcc = ld(gc_s, i)
            gcj = jnp.sum(jnp.where(eye2, gcc, 0.0), axis=0, keepdims=True)
            dec = jnp.where(lower, jnp.exp(jnp.minimum(gcc - gcj, 0.0)), 0.0)
            a_qk.append((aq[i][DN_CHUNK:, :] * dec).astype(BF16))
            l_m.append(jnp.where(eye2, 0.0, aq[i][:DN_CHUNK, :] * dec))

        lhl = [_split(m) for m in l_m]
        x0 = [eye_f - jnp.where(sub_masks[0], m, 0.0) for m in l_m]
        for lvl in range(1, len(sub_masks)):
            dc = [_dot(x0[i].astype(BF16), bd(jnp.where(sub_masks[lvl], lhl[i][0], jnp.zeros_like(lhl[i][0]))))
                  for i in range(n)]
            x0 = [x0[i] - _dot(dc[i].astype(BF16), bd(x0[i].astype(BF16))) for i in range(n)]
        xhl = [_split(m) for m in x0]
        res = []
        for i in range(n):
            bxh = bd(xhl[i][0])
            lx = _dot(lhl[i][0], bxh) + _dot(lhl[i][0], bd(xhl[i][1])) + _dot(lhl[i][1], bxh)
            res.append((eye_f - x0[i] - lx).astype(BF16))
        t_m = [x0[i] + _dot(xhl[i][0], bd(res[i])) for i in range(n)]

        uw = []
        for i in range(n):
            th, tl = _split(t_m[i])
            kbg = (ld(kb_s, i) * ld(eg_s, i)).astype(BF16)
            rhs = jnp.concatenate([bd(ld(vb_s, i).astype(BF16)), bd(kbg)], axis=1)
            uw.append((_dot(th, rhs) + _dot(tl, rhs)).astype(BF16))
        qo = [_dot(a_qk[i], jnp.concatenate([bd(uw[i][:, PAIR:]), bd(uw[i][:, :PAIR])], axis=1))
              for i in range(n)]
        mn = [_dot_tn(ld(kdec_s, i).astype(BF16), jnp.concatenate([uw[i][:, PAIR:], uw[i][:, :PAIR]], axis=1))
              for i in range(n)]
        for i, (c, p) in enumerate(chains):
            qe_s[rows[i], cols[i]] = (ld(q_s, i) * ld(eg_s, i) - qo[i][:, :PAIR]).astype(BF16)
            od_s[rows[i], cols[i]] = qo[i][:, PAIR:]
            m_s[c, p] = jnp.where(bdmask, mn[i][:, :PAIR], 0.0).astype(BF16)
            n_s[c, p] = jnp.where(bdmask, mn[i][:, PAIR:], 0.0)
        return carry

    lax.fori_loop(0, n_chunks // DN_UNROLL, local_body, 0)

    for c in range(n_chunks):
        rows = slice(c * DN_CHUNK, (c + 1) * DN_CHUNK)
        for p in range(N_PAIRS):
            cols = slice(p * PAIR, (p + 1) * PAIR)
            st = state_s[p]
            r = _dot(jnp.concatenate([qe_s[rows, cols], m_s[c, p]], axis=0), st.astype(BF16))
            od_s[rows, cols] += r[:DN_CHUNK, :]
            state_s[p] = gl_s[c * DN_CHUNK:c * DN_CHUNK + 1, cols] * st - r[DN_CHUNK:, :] + n_s[c, p]

    od = od_s[...]
    ms = _dot((od * od).astype(BF16), seg) * (1.0 / DN_D)
    zz = z_ref[...]
    o_ref[...] = (od * lax.rsqrt(ms + RMS_EPS) * par_ref[2:3, :] * (zz * jax.nn.sigmoid(zz))).astype(BF16)


def _deltanet(qkv, ab, z, convw, par, consts, bsz, seq):
    t = qkv.shape[0]
    ts = min(TS_DN, seq)
    nt = seq // ts
    hb = ts // HALO
    ea, eb, seg, tri, ones = consts
    row = lambda b, i: (b * nt + i, 0)
    const = lambda b, i: (0, 0)
    halo_map = lambda b, i: (jnp.maximum((b * nt + i) * hb - 1, 0), 0)
    cspec = lambda a: pl.BlockSpec(a.shape, const, pipeline_mode=pl.Buffered(1))
    big = lambda: pltpu.VMEM((ts, DN_W), F32)
    return pl.pallas_call(
        _dn_kernel,
        grid=(bsz, nt),
        in_specs=[pl.BlockSpec((ts, 3 * DN_W), row), pl.BlockSpec((HALO, 3 * DN_W), halo_map),
                  pl.BlockSpec((ts, LANES), row), pl.BlockSpec((ts, DN_W), row),
                  cspec(convw), cspec(par), cspec(ea), cspec(eb), cspec(seg), cspec(tri), cspec(ones)],
        out_specs=pl.BlockSpec((ts, DN_W), row),
        out_shape=jax.ShapeDtypeStruct((t, DN_W), BF16),
        scratch_shapes=[pltpu.VMEM((ts + HALO, 3 * DN_W), F32)] + [big() for _ in range(9)]
                       + [pltpu.VMEM((ts, DN_W), BF16),
                          pltpu.VMEM((ts // DN_CHUNK, N_PAIRS, PAIR, PAIR), BF16),
                          pltpu.VMEM((ts // DN_CHUNK, N_PAIRS, PAIR, PAIR), F32),
                          pltpu.VMEM((N_PAIRS, PAIR, PAIR), F32)],
        compiler_params=_cparams(("parallel", "arbitrary")),
        name="deltanet",
    )(qkv, qkv, ab, z, convw, par, ea, eb, seg, tri, ones)


def _df_kernel(sc_ref, q_ref, k_ref, vt_ref, bt_ref, w_ref, o_ref, sa_ref, sb_ref, acc_ref):
    qi = pl.program_id(2)
    tq = q_ref.shape[0]
    lane = lax.broadcasted_iota(jnp.int32, (tq, 2 * DF_D), 1)
    q = q_ref[...]
    zq = jnp.zeros_like(q)
    qs = (jnp.where(lane < DF_D, q, zq), jnp.where(lane < DF_D, zq, q))
    acc_ref[...] = jnp.zeros_like(acc_ref)

    def scores(j, s_ref):
        kk = k_ref[pl.ds(pl.multiple_of(j * tq, tq), tq), :]
        for mp in range(2):
            s_ref[mp] = _dot_nt(kk, qs[mp])

    def absorb(j, s_ref, bias, carry):
        vt = vt_ref[:, pl.ds(pl.multiple_of(j * tq, tq), tq)]
        out = []
        for mp in range(2):
            m_old, l_old = carry[mp]
            st = s_ref[mp]
            if bias is not None:
                st = st + bias
            m_new = jnp.maximum(m_old, jnp.max(st, axis=0, keepdims=True))
            alpha = jnp.exp2(m_old - m_new)
            pr = jnp.exp2(st - m_new)
            l_new = alpha * l_old + jnp.sum(pr, axis=0, keepdims=True)
            acc_ref[mp] = alpha * acc_ref[mp] + _dot(vt, pr.astype(BF16))
            out.append((m_new, l_new))
        return tuple(out)

    def absorb_prev(j, s_ref, carry):
        c0 = tq - LANES
        for mp in range(2):
            s_ref[mp, c0:, 0:LANES] += bt_ref[1, c0:, 0:LANES]
        return absorb(j, s_ref, None, carry)

    h2 = tq // 2
    d0 = pl.multiple_of(qi * tq, tq)

    def scores_diag(s_ref):
        k_top = k_ref[pl.ds(d0, h2), :]
        k_bot = k_ref[pl.ds(pl.multiple_of(d0 + h2, h2), h2), :]
        for mp in range(2):
            s_ref[mp, 0:h2, :] = _dot_nt(k_top, qs[mp])
            s_ref[mp, h2:, h2:] = _dot_nt(k_bot, qs[mp][h2:, :])

    def absorb_diag(s_ref, carry):
        vt_top = vt_ref[:, pl.ds(d0, h2)]
        vt_bot = vt_ref[:, pl.ds(pl.multiple_of(d0 + h2, h2), h2)]
        out = []
        for mp in range(2):
            m_old, l_old = carry[mp]
            top = s_ref[mp, 0:h2, :] + bt_ref[0, 0:h2, :]
            bot = s_ref[mp, h2:, h2:] + bt_ref[0, h2:, h2:]
            m_top = jnp.maximum(m_old, jnp.max(top, axis=0, keepdims=True))
            m_new = jnp.concatenate(
                [m_top[:, :h2], jnp.maximum(m_top[:, h2:], jnp.max(bot, axis=0, keepdims=True))], axis=1)
            alpha = jnp.exp2(m_old - m_new)
            p_top = jnp.exp2(top - m_new)
            p_bot = jnp.exp2(bot - m_new[:, h2:])
            l_bot = jnp.sum(p_bot, axis=0, keepdims=True)
            l_new = (alpha * l_old + jnp.sum(p_top, axis=0, keepdims=True)
                     + jnp.concatenate([jnp.zeros_like(l_bot), l_bot], axis=1))
            acc_ref[mp] = alpha * acc_ref[mp] + _dot(vt_top, p_top.astype(BF16))
            acc_ref[mp, :, h2:] += _dot(vt_bot, p_bot.astype(BF16))
            out.append((m_new, l_new))
        return tuple(out)

    init1 = (jnp.full((1, tq), -jnp.inf, F32), jnp.zeros((1, tq), F32))
    carry = (init1, init1)
    n_far = jnp.maximum(qi - 1, 0)
    odd = n_far % 2

    @pl.when(qi == 0)
    def _():
        scores_diag(sb_ref)

    @pl.when(jnp.logical_and(qi > 0, odd == 0))
    def _():
        scores(0, sa_ref)

    def odd_step(_, c):
        scores(0, sb_ref)
        scores(1, sa_ref)
        return absorb(0, sb_ref, None, c)

    carry = lax.fori_loop(0, odd, odd_step, carry)

    def pair_step(jp, c):
        j = odd + 2 * jp
        scores(j + 1, sb_ref)
        c = absorb(j, sa_ref, None, c)
        scores(j + 2, sa_ref)
        return absorb(j + 1, sb_ref, None, c)

    carry = lax.fori_loop(0, n_far // 2, pair_step, carry)

    def tail_step(j, c):
        scores_diag(sb_ref)
        c = absorb_prev(j, sa_ref, c)
        return absorb_diag(sb_ref, c)

    carry = lax.fori_loop(n_far, qi, tail_step, carry)
    carry = lax.fori_loop(0, jnp.where(qi == 0, 1, 0), lambda _, c: absorb_diag(sb_ref, c), carry)

    (_, l0), (_, l1) = carry
    ot = acc_ref[0] / l0 - sc_ref[0] * (acc_ref[1] / l1)
    ms = jnp.mean(ot * ot, axis=0, keepdims=True)
    ot = ot * lax.rsqrt(ms + RMS_EPS) * w_ref[...] * sc_ref[1]
    o_ref[...] = ot.T.astype(BF16)


def _attn_bias_tiles(rel_bias, seq, tq):
    rb = rel_bias.astype(F32)
    far = rb[_t5_causal_bucket(jnp.asarray(seq - 1, jnp.int32))]
    kj = jnp.arange(tq, dtype=jnp.int32)[:, None]
    qi = jnp.arange(tq, dtype=jnp.int32)[None, :]
    dist = jnp.stack([qi - kj, qi + tq - kj])
    onehot = (_t5_causal_bucket(jnp.maximum(dist, 0))[..., None]
              == jnp.arange(REL_BUCKETS, dtype=jnp.int32)).astype(F32)
    vals = (jnp.einsum('ntqb,bh->ntqh', onehot, rb, precision=lax.Precision.HIGHEST) - far) * LOG2E
    vals = jnp.where((dist >= 0)[..., None], vals, -jnp.inf)
    return jnp.transpose(vals, (3, 0, 1, 2))


def _diff_attention(qd, kd, vdt, btiles, subln_w, scalars, bsz, seq):
    t = qd.shape[0]
    tq = min(TQ_DF, seq)
    nq = seq // tq
    sbuf = pltpu.VMEM((2, tq, tq), F32)
    return pl.pallas_call(
        _df_kernel,
        grid_spec=pltpu.PrefetchScalarGridSpec(
            num_scalar_prefetch=1,
            grid=(bsz, DF_HEADS, nq),
            in_specs=[pl.BlockSpec((tq, 2 * DF_D), lambda b, h, i, sc: (b * nq + i, h)),
                      pl.BlockSpec((seq, 2 * DF_D), lambda b, h, i, sc: (b, h)),
                      pl.BlockSpec((None, 2 * DF_D, seq), lambda b, h, i, sc: (b, h, 0)),
                      pl.BlockSpec((None, 2, tq, tq), lambda b, h, i, sc: (h, 0, 0, 0)),
                      pl.BlockSpec((2 * DF_D, 1), lambda b, h, i, sc: (0, 0))],
            out_specs=pl.BlockSpec((tq, 2 * DF_D), lambda b, h, i, sc: (b * nq + i, h)),
            scratch_shapes=[sbuf, sbuf, pltpu.VMEM((2, 2 * DF_D, tq), F32)],
        ),
        out_shape=jax.ShapeDtypeStruct((t, DF_W), BF16),
        compiler_params=_cparams(("parallel", "parallel", "arbitrary")),
        name="diffattn",
    )(scalars, qd, kd, vdt, btiles, subln_w)


def _outproj_kernel(alpha, x_ref, a_ref, b_ref, w_ref, g_ref, bb_ref, o_ref):
    mix = _dot(a_ref[...], w_ref[0:DN_W, :]) + _dot(b_ref[...], w_ref[DN_W:, :])
    o_ref[...] = _layer_norm(alpha * x_ref[...] + mix, g_ref[...], bb_ref[...])


def _outproj_ln(x2, o_dn, o_df, w_out, g, b, alpha):
    t, d = x2.shape
    tm = min(TM_PROJ, t)
    row = lambda i: (i, 0)
    const = lambda i: (0, 0)
    return pl.pallas_call(
        functools.partial(_outproj_kernel, alpha),
        grid=(t // tm,),
        in_specs=[pl.BlockSpec((tm, d), row), pl.BlockSpec((tm, DN_W), row), pl.BlockSpec((tm, DF_W), row),
                  pl.BlockSpec(w_out.shape, const, pipeline_mode=pl.Buffered(1)),
                  pl.BlockSpec((1, d), const), pl.BlockSpec((1, d), const)],
        out_specs=pl.BlockSpec((tm, d), row),
        out_shape=jax.ShapeDtypeStruct((t, d), F32),
        compiler_params=_cparams(("parallel",)),
        name="outproj_ln",
    )(x2, o_dn, o_df, w_out, g, b)


def _swiglu_acc(xparts, wg_ref, wu_ref, wd_ref, acc_ref):
    d_ff = wg_ref.shape[-1]
    for c0 in range(0, d_ff, FF_CHUNK):
        cs = slice(c0, c0 + FF_CHUNK)
        hg = sum(_dot(xp, wg_ref[k0:k0 + xp.shape[1], cs]) for xp, k0 in xparts)
        hu = sum(_dot(xp, wu_ref[k0:k0 + xp.shape[1], cs]) for xp, k0 in xparts)
        hh = (hg * jax.nn.sigmoid(hg) * hu).astype(BF16)
        contrib = _dot(hh, wd_ref[cs, :])
        if c0 == 0:
            acc_ref[...] = contrib
        else:
            acc_ref[...] += contrib


def _ffn_kernel(alpha, x_ref, wg_ref, wu_ref, wd_ref, g_ref, b_ref, o_ref, acc_ref):
    x = x_ref[...]
    _swiglu_acc([(x.astype(BF16), 0)], wg_ref, wu_ref, wd_ref, acc_ref)
    o_ref[...] = _layer_norm(alpha * x + acc_ref[...], g_ref[...], b_ref[...])


def _ffn_ln(x2, wg, wu, wd, g, b, alpha):
    t, d = x2.shape
    tm = min(TM_FFN, t)
    row = lambda i: (i, 0)
    const = lambda i: (0, 0)
    wspec = lambda a: pl.BlockSpec(a.shape, const, pipeline_mode=pl.Buffered(1))
    return pl.pallas_call(
        functools.partial(_ffn_kernel, alpha),
        grid=(t // tm,),
        in_specs=[pl.BlockSpec((tm, d), row), wspec(wg), wspec(wu), wspec(wd),
                  pl.BlockSpec((1, d), const), pl.BlockSpec((1, d), const)],
        out_specs=pl.BlockSpec((tm, d), row),
        out_shape=jax.ShapeDtypeStruct((t, d), F32),
        scratch_shapes=[pltpu.VMEM((tm, d), F32)],
        compiler_params=_cparams(("parallel",)),
        name="ffn_ln",
    )(x2, wg, wu, wd, g, b)


def _pack_halves(x):
    h = x.shape[1] // 2
    hi = lax.bitcast_convert_type(x[:, :h].astype(BF16).astype(F32), jnp.uint32)
    lo = lax.bitcast_convert_type(x[:, h:].astype(BF16).astype(F32), jnp.uint32)
    return hi | (lo >> 16)


def _unpack_halves(p):
    a = lax.bitcast_convert_type(p & jnp.uint32(0xFFFF0000), F32)
    b = lax.bitcast_convert_type(p << 16, F32)
    return a, b


def _router_kernel(x_ref, wr_ref, o_ref, xp_ref):
    x = x_ref[...]
    xp_ref[...] = _pack_halves(x)
    logits = jnp.dot(x, wr_ref[...], preferred_element_type=F32, precision=lax.Precision.HIGHEST)
    lane = lax.broadcasted_iota(jnp.int32, logits.shape, 1)
    lg = jnp.where(lane < N_EXPERTS, logits, -jnp.inf)
    m1 = jnp.max(lg, axis=-1, keepdims=True)
    i1 = jnp.min(jnp.where(lg == m1, lane, LANES), axis=-1, keepdims=True)
    lg2 = jnp.where(lane == i1, -jnp.inf, lg)
    m2 = jnp.max(lg2, axis=-1, keepdims=True)
    i2 = jnp.min(jnp.where(lg2 == m2, lane, LANES), axis=-1, keepdims=True)
    e = jnp.exp(m2 - m1)
    g1 = 1.0 / (1.0 + e)
    g2 = e / (1.0 + e)
    out = jnp.where(lane == 0, i1.astype(F32), 0.0)
    out = jnp.where(lane == 1, i2.astype(F32), out)
    out = jnp.where(lane == 2, g1, out)
    o_ref[...] = jnp.where(lane == 3, g2, out)


def _router(x2, wr):
    t, d = x2.shape
    tm = min(TM_PROJ, t)
    row = lambda i: (i, 0)
    return pl.pallas_call(
        _router_kernel,
        grid=(t // tm,),
        in_specs=[pl.BlockSpec((tm, d), row), pl.BlockSpec((d, LANES), lambda i: (0, 0))],
        out_specs=[pl.BlockSpec((tm, LANES), row), pl.BlockSpec((tm, d // 2), row)],
        out_shape=[jax.ShapeDtypeStruct((t, LANES), F32), jax.ShapeDtypeStruct((t, d // 2), jnp.uint32)],
        compiler_params=_cparams(("parallel",)),
        name="router",
    )(x2, wr)


def _sc_mesh():
    return plsc.VectorSubcoreMesh(core_axis_name="core", subcore_axis_name="subcore")


def _sc_index_rows(idx):
    win = idx.reshape(-1, SC_WIN)
    return jnp.concatenate([win, jnp.zeros((win.shape[0], SC_IDX_TILE - SC_WIN), idx.dtype)], axis=1)


def _sc_dispatch(xp, slot0, slot1, n_slot):
    t, c = xp.shape
    half = t // SC_WIN // 2
    idx_spec = pl.BlockSpec((1, SC_IDX_TILE), lambda cc, i: (cc * half + i, 0))

    @pl.kernel(out_type=jax.ShapeDtypeStruct((n_slot, c), xp.dtype), mesh=_sc_mesh(), scratch_types=[],
               name="moe_dispatch")
    def run(x_hbm, i0_hbm, i1_hbm, o_hbm):
        def body(x_vmem, i0_vmem, i1_vmem):
            pltpu.sync_copy(x_vmem, o_hbm.at[i0_vmem.at[0, pl.ds(0, SC_WIN)]])
            pltpu.sync_copy(x_vmem, o_hbm.at[i1_vmem.at[0, pl.ds(0, SC_WIN)]])

        pltpu.emit_pipeline(
            body,
            grid=(2, half),
            in_specs=[pl.BlockSpec((SC_WIN, c), lambda cc, i: (cc * half + i, 0)), idx_spec, idx_spec],
            out_specs=[],
            core_axis_name=("core", "subcore"),
            dimension_semantics=(pltpu.PARALLEL, pltpu.PARALLEL),
        )(x_hbm, i0_hbm, i1_hbm)

    return run(xp, _sc_index_rows(slot0), _sc_index_rows(slot1))


def _sc_gather(yp, idx):
    t = idx.shape[0]
    c = yp.shape[1]
    half = t // SC_WIN // 2

    @pl.kernel(out_type=jax.ShapeDtypeStruct((t, c), yp.dtype), mesh=_sc_mesh(), scratch_types=[],
               name="moe_gather")
    def run(y_hbm, i_hbm, o_hbm):
        def body(i_vmem, o_vmem):
            pltpu.sync_copy(y_hbm.at[i_vmem.at[0, pl.ds(0, SC_WIN)]], o_vmem)

        pltpu.emit_pipeline(
            body,
            grid=(2, half),
            in_specs=[pl.BlockSpec((1, SC_IDX_TILE), lambda cc, i: (cc * half + i, 0))],
            out_specs=[pl.BlockSpec((SC_WIN, c), lambda cc, i: (cc * half + i, 0))],
            core_axis_name=("core", "subcore"),
            dimension_semantics=(pltpu.PARALLEL, pltpu.PARALLEL),
        )(i_hbm, o_hbm)

    return run(yp, _sc_index_rows(idx))


def _expert_kernel(be_ref, nv_ref, x_ref, wg_ref, wu_ref, wd_ref, o_ref, acc_ref):
    i = pl.program_id(0)
    n_valid = nv_ref[i]

    @pl.when(n_valid > 0)
    def _():
        row = lax.broadcasted_iota(jnp.int32, x_ref.shape, 0)
        xa, xb = _unpack_halves(jnp.where(row < n_valid, x_ref[...], jnp.uint32(0)))
        h = xa.shape[1]
        _swiglu_acc([(xa.astype(BF16), 0), (xb.astype(BF16), h)], wg_ref, wu_ref, wd_ref, acc_ref)
        o_ref[...] = _pack_halves(acc_ref[...])

    @pl.when(n_valid <= 0)
    def _():
        o_ref[...] = jnp.zeros_like(o_ref)


def _experts(xbp, blk_e, n_valid, wg, wu, wd):
    n_slot, dh = xbp.shape
    d = 2 * dh
    n_blk = n_slot // MOE_BLK
    dff = wg.shape[-1]
    row = lambda i, be, nv: (i, 0)
    wmap = lambda i, be, nv: (be[i], 0, 0)
    return pl.pallas_call(
        _expert_kernel,
        grid_spec=pltpu.PrefetchScalarGridSpec(
            num_scalar_prefetch=2,
            grid=(n_blk,),
            in_specs=[pl.BlockSpec((MOE_BLK, dh), row),
                      pl.BlockSpec((None, d, dff), wmap, pipeline_mode=pl.Buffered(1)),
                      pl.BlockSpec((None, d, dff), wmap, pipeline_mode=pl.Buffered(1)),
                      pl.BlockSpec((None, dff, d), wmap, pipeline_mode=pl.Buffered(1))],
            out_specs=pl.BlockSpec((MOE_BLK, dh), row),
            scratch_shapes=[pltpu.VMEM((MOE_BLK, d), F32)],
        ),
        out_shape=jax.ShapeDtypeStruct((n_slot, dh), jnp.uint32),
        compiler_params=_cparams(("arbitrary",)),
        name="experts",
    )(blk_e, n_valid, xbp, wg, wu, wd)


def _combine_kernel(alpha, x_ref, y0_ref, y1_ref, r_ref, g_ref, b_ref, o_ref):
    r = r_ref[...]
    a0, b0 = _unpack_halves(y0_ref[...])
    a1, b1 = _unpack_halves(y1_ref[...])
    g0 = r[:, 2:3]
    g1 = r[:, 3:4]
    f = jnp.concatenate([g0 * a0 + g1 * a1, g0 * b0 + g1 * b1], axis=1)
    o_ref[...] = _layer_norm(alpha * x_ref[...] + f, g_ref[...], b_ref[...])


def _combine_ln(x2, y0, y1, route, g, b, alpha):
    t, d = x2.shape
    tm = min(TM_PROJ, t)
    row = lambda i: (i, 0)
    const = lambda i: (0, 0)
    return pl.pallas_call(
        functools.partial(_combine_kernel, alpha),
        grid=(t // tm,),
        in_specs=[pl.BlockSpec((tm, d), row), pl.BlockSpec((tm, d // 2), row), pl.BlockSpec((tm, d // 2), row),
                  pl.BlockSpec((tm, LANES), row), pl.BlockSpec((1, d), const), pl.BlockSpec((1, d), const)],
        out_specs=pl.BlockSpec((tm, d), row),
        out_shape=jax.ShapeDtypeStruct((t, d), F32),
        compiler_params=_cparams(("parallel",)),
        name="combine_ln",
    )(x2, y0, y1, route, g, b)


def _moe(x2, wr, wg, wu, wd, g, b, alpha):
    t, d = x2.shape
    route, xp = _router(x2, wr)
    top_idx = route[:, 0:2].astype(jnp.int32)
    n_asg = t * TOP_K
    flat_e = top_idx.reshape(n_asg)
    onehot = (flat_e[:, None] == jnp.arange(N_EXPERTS, dtype=jnp.int32)[None, :]).astype(jnp.int32)
    csum = jnp.cumsum(onehot, axis=0)
    rank = jnp.sum((csum - onehot) * onehot, axis=1)
    counts = csum[-1]
    padded = (counts + MOE_BLK - 1) // MOE_BLK * MOE_BLK
    pad_end = jnp.cumsum(padded)
    pad_start = pad_end - padded
    slot = (pad_start[flat_e] + rank).reshape(t, TOP_K)
    n_slot = -(-n_asg // MOE_BLK) * MOE_BLK + N_EXPERTS * MOE_BLK
    n_blk = n_slot // MOE_BLK
    blk_start = jnp.arange(n_blk, dtype=jnp.int32) * MOE_BLK
    blk_e = jnp.minimum(jnp.searchsorted(pad_end, blk_start, side='right'), N_EXPERTS - 1).astype(jnp.int32)
    n_valid = jnp.clip(pad_start[blk_e] + counts[blk_e] - blk_start, 0, MOE_BLK).astype(jnp.int32)
    slot0 = slot[:, 0]
    slot1 = slot[:, 1]
    xbp = _sc_dispatch(xp, slot0, slot1, n_slot)
    ybp = _experts(xbp, blk_e, n_valid, wg, wu, wd)
    return _combine_ln(x2, _sc_gather(ybp, slot0), _sc_gather(ybp, slot1), route, g, b, alpha)


def _t5_causal_bucket(dist):
    max_exact = REL_BUCKETS // 2
    d = jnp.maximum(dist, 1).astype(F32)
    large = max_exact + (jnp.log(d / max_exact) / math.log(REL_MAX_DIST / max_exact)
                         * (REL_BUCKETS - max_exact)).astype(jnp.int32)
    large = jnp.minimum(large, REL_BUCKETS - 1)
    return jnp.where(dist < max_exact, dist, large)


def _dn_constants(ts):
    lanes = np.arange(DN_W)
    ea = np.zeros((LANES, DN_W), np.float32)
    eb = np.zeros((LANES, DN_W), np.float32)
    ea[lanes // DN_D, lanes] = 1.0
    eb[DN_HEADS + lanes // DN_D, lanes] = 1.0
    seg = (lanes[:, None] // DN_D == lanes[None, :] // DN_D).astype(np.float32)
    r = np.arange(ts)
    same = r[:, None] // DN_CHUNK == r[None, :] // DN_CHUNK
    tri = (same & (r[:, None] >= r[None, :])).astype(np.float32)
    ones = same.astype(np.float32)
    return tuple(jnp.asarray(a, BF16) for a in (ea, eb, seg, tri, ones))


def kernel(x, w_in, w_out, conv_w, dn_a_log, dn_dt_bias, dn_norm_w, df_lambda, df_subln_w, rel_bias,
           ln1_g, ln1_b, ln2_g, ln2_b, ffn_w_gate, ffn_w_up, ffn_w_down, moe_router, moe_w_gate,
           moe_w_up, moe_w_down):
    bsz, seq, d = x.shape
    depth = w_in.shape[0]
    alpha = (2 * depth) ** 0.25
    t = bsz * seq
    tq = min(TQ_DF, seq)
    assert REL_MAX_DIST <= LANES and tq % (2 * LANES) == 0 and seq % tq == 0 and seq % min(TS_DN, seq) == 0
    btiles = _attn_bias_tiles(rel_bias, seq, tq)

    dn_consts = _dn_constants(min(TS_DN, seq))
    c_dn = 3 * DN_W
    x2 = x.reshape(t, d)
    for layer in range(depth):
        lambda_init = 0.8 - 0.6 * math.exp(-0.3 * layer)
        wl = w_in[layer]
        n1 = c_dn + DN_W + 2 * DN_HEADS
        w1 = jnp.concatenate([wl[:, :n1], jnp.zeros((d, LANES - 2 * DN_HEADS), F32)], axis=1).astype(BF16)
        w2 = wl[:, n1:n1 + 2 * DF_W].astype(BF16)
        wvt = wl[:, n1 + 2 * DF_W:].T.astype(BF16)
        qkv, z, ab, qd, kd, vdt = _inproj(x2, w1, w2, wvt, bsz, seq)

        convw = jnp.concatenate([conv_w[layer], jnp.zeros((HALO - DN_CONV, c_dn), F32)], axis=0)
        par = jnp.zeros((HALO, DN_W), F32)
        par = par.at[0, 0:DN_HEADS].set(dn_a_log[layer])
        par = par.at[1, 0:DN_HEADS].set(dn_dt_bias[layer])
        par = par.at[2].set(jnp.tile(dn_norm_w[layer], DN_HEADS))
        o_dn = _deltanet(qkv, ab, z, convw, par, dn_consts, bsz, seq)

        lf = df_lambda[layer].astype(F32)
        lam = jnp.exp(jnp.sum(lf[0] * lf[1])) - jnp.exp(jnp.sum(lf[2] * lf[3])) + lambda_init
        scalars = jnp.stack([lam, jnp.asarray(1.0 - lambda_init, F32)]).astype(F32)
        o_df = _diff_attention(qd, kd, vdt, btiles, df_subln_w[layer].reshape(2 * DF_D, 1), scalars, bsz, seq)

        i = layer // 2
        g1 = ln1_g[layer].reshape(1, d)
        b1 = ln1_b[layer].reshape(1, d)
        g2 = ln2_g[layer].reshape(1, d)
        b2 = ln2_b[layer].reshape(1, d)
        x2 = _outproj_ln(x2, o_dn, o_df, w_out[layer].astype(BF16), g1, b1, alpha)
        if layer % 2 == 0:
            x2 = _ffn_ln(x2, ffn_w_gate[i].astype(BF16), ffn_w_up[i].astype(BF16), ffn_w_down[i].astype(BF16),
                         g2, b2, alpha)
        else:
            wr = jnp.concatenate([moe_router[i], jnp.zeros((d, LANES - N_EXPERTS), F32)], axis=1)
            x2 = _moe(x2, wr, moe_w_gate[i].astype(BF16), moe_w_up[i].astype(BF16), moe_w_down[i].astype(BF16),
                      g2, b2, alpha)
    return x2.reshape(bsz, seq, d)
```

```python
import functools
import math

import jax
import jax.numpy as jnp
import numpy as np
from jax import lax
from jax.experimental import pallas as pl
from jax.experimental.pallas import tpu as pltpu
from jax.experimental.pallas import tpu_sc as plsc

F32 = jnp.float32
BF16 = jnp.bfloat16

DN_HEADS = 8
DN_D = 64
DN_CONV = 4
DN_CHUNK = 64
DN_W = DN_HEADS * DN_D
DF_HEADS = 4
DF_D = 64
DF_W = DF_HEADS * 2 * DF_D
REL_BUCKETS = 32
REL_MAX_DIST = 128
N_EXPERTS = 8
TOP_K = 2
MOE_BLK = 512
LN_EPS = 1e-5
RMS_EPS = 1e-6
LOG2E = math.log2(math.e)

LANES = 128
HALO = 8
PAIR = 2 * DN_D
N_PAIRS = DN_HEADS // 2
VMEM_LIMIT = 56 * 1024 * 1024

TM_PROJ = 512
TS_DN = 256
DN_UNROLL = 4
TQ_DF = 1024
DF_DIAG_BANDS = 4
TM_FFN = 512
FF_CHUNK = 256
SC_WIN = 64
SC_IDX_TILE = 128


def _cparams(sem):
    return pltpu.CompilerParams(dimension_semantics=sem, vmem_limit_bytes=VMEM_LIMIT)


def _dot(a, b):
    return jnp.dot(a, b, preferred_element_type=F32)


def _dot_nt(a, b):
    return lax.dot_general(a, b, (((1,), (1,)), ((), ())), preferred_element_type=F32)


def _dot_tn(a, b):
    return lax.dot_general(a, b, (((0,), (0,)), ((), ())), preferred_element_type=F32)


def _split(x):
    hi = x.astype(BF16)
    lo = (x - hi.astype(F32)).astype(BF16)
    return hi, lo


def _mm_xc(x, c):
    hi, lo = _split(x)
    return _dot(hi, c) + _dot(lo, c)


def _mm_cx(c, x):
    hi, lo = _split(x)
    return _dot(c, hi) + _dot(c, lo)


def _mm3(a, b):
    ah, al = _split(a)
    bh, bl = _split(b)
    return _dot(ah, bh) + _dot(ah, bl) + _dot(al, bh)


def _layer_norm(v, g, b):
    mu = jnp.mean(v, axis=-1, keepdims=True)
    d = v - mu
    var = jnp.mean(d * d, axis=-1, keepdims=True)
    return d * lax.rsqrt(var + LN_EPS) * g + b


def _inproj_kernel(x_ref, w1_ref, w2_ref, wvt_ref, qkv_ref, z_ref, ab_ref, qd_ref, kd_ref, vd_ref):
    xb = x_ref[...].astype(BF16)
    c = 3 * DN_W
    qkv_ref[...] = _dot(xb, w1_ref[:, 0:c])
    z_ref[...] = _dot(xb, w1_ref[:, c:c + DN_W])
    ab_ref[...] = _dot(xb, w1_ref[:, c + DN_W:c + DN_W + LANES])
    qd_ref[...] = (_dot(xb, w2_ref[:, 0:DF_W]) * (DF_D ** -0.5 * LOG2E)).astype(BF16)
    kd_ref[...] = _dot(xb, w2_ref[:, DF_W:2 * DF_W]).astype(BF16)
    vd_ref[...] = _dot_nt(wvt_ref[...], xb).astype(BF16)


def _inproj(x2, w1, w2, wvt, bsz, seq):
    t, d = x2.shape
    tm = min(TM_PROJ, seq)
    nt = seq // tm
    n1, n2 = w1.shape[1], w2.shape[1]
    row = lambda i: (i, 0)
    const = lambda i: (0, 0)
    return pl.pallas_call(
        _inproj_kernel,
        grid=(t // tm,),
        in_specs=[pl.BlockSpec((tm, d), row),
                  pl.BlockSpec((d, n1), const, pipeline_mode=pl.Buffered(1)),
                  pl.BlockSpec((d, n2), const, pipeline_mode=pl.Buffered(1)),
                  pl.BlockSpec((DF_W, d), const, pipeline_mode=pl.Buffered(1))],
        out_specs=[pl.BlockSpec((tm, 3 * DN_W), row), pl.BlockSpec((tm, DN_W), row),
                   pl.BlockSpec((tm, LANES), row), pl.BlockSpec((tm, DF_W), row),
                   pl.BlockSpec((tm, DF_W), row),
                   pl.BlockSpec((None, DF_W, tm), lambda i: (i // nt, 0, i % nt))],
        out_shape=[jax.ShapeDtypeStruct((t, 3 * DN_W), F32), jax.ShapeDtypeStruct((t, DN_W), F32),
                   jax.ShapeDtypeStruct((t, LANES), F32), jax.ShapeDtypeStruct((t, DF_W), BF16),
                   jax.ShapeDtypeStruct((t, DF_W), BF16), jax.ShapeDtypeStruct((bsz, DF_W, seq), BF16)],
        compiler_params=_cparams(("parallel",)),
        name="inproj",
    )(x2, w1, w2, wvt)


def _dn_kernel(qkv_ref, halo_ref, ab_ref, z_ref, convw_ref, par_ref, ea_ref, eb_ref, seg_ref,
               tri_ref, ones_ref, o_ref,
               xe_s, q_s, k_s, kb_s, vb_s, gc_s, eg_s, kdec_s, gl_s, od_s, qe_s, m_s, n_s, state_s):
    i = pl.program_id(1)
    ts = qkv_ref.shape[0]
    n_chunks = ts // DN_CHUNK

    @pl.when(i == 0)
    def _():
        state_s[...] = jnp.zeros_like(state_s)

    halo = halo_ref[...]
    xe_s[0:HALO, :] = jnp.where(i > 0, halo, jnp.zeros_like(halo))
    xe_s[HALO:, :] = qkv_ref[...]
    xe = xe_s[...]
    y = convw_ref[DN_CONV - 1:DN_CONV, :] * xe[HALO:, :]
    for tap in range(DN_CONV - 1):
        y += convw_ref[tap:tap + 1, :] * pltpu.roll(xe, DN_CONV - 1 - tap, axis=0)[HALO:, :]
    y = y * jax.nn.sigmoid(y)
    q = y[:, 0:DN_W]
    k = y[:, DN_W:2 * DN_W]
    v = y[:, 2 * DN_W:3 * DN_W]

    seg = seg_ref[...]
    q = q * lax.rsqrt(_dot((q * q).astype(BF16), seg) + 1e-6) * (DN_D ** -0.5)
    k = k * lax.rsqrt(_dot((k * k).astype(BF16), seg) + 1e-6)

    ab = ab_ref[...]
    xa = ab + par_ref[1:2, 0:LANES]
    softplus = jnp.maximum(xa, 0.0) + jnp.log(1.0 + jnp.exp(-jnp.abs(xa)))
    g = _mm_xc(-jnp.exp(par_ref[0:1, 0:LANES]) * softplus, ea_ref[...])
    beta = _mm_xc(jax.nn.sigmoid(ab), eb_ref[...])
    gc = _mm_cx(tri_ref[...], g)
    gl = _mm_cx(ones_ref[...], g)
    eg = jnp.exp(gc)
    kb = k * beta
    q_s[...] = q
    k_s[...] = k
    kb_s[...] = kb
    vb_s[...] = v * beta
    gc_s[...] = gc
    eg_s[...] = eg
    kdec_s[...] = k * jnp.exp(gl - gc)
    gl_s[...] = jnp.exp(gl)

    lane = lax.broadcasted_iota(jnp.int32, (DN_CHUNK, PAIR), 1)
    rowi = lax.broadcasted_iota(jnp.int32, (DN_CHUNK, PAIR), 0)
    colj = jnp.where(lane >= DN_D, lane - DN_D, lane)
    even = lane < DN_D
    eye2 = rowi == colj
    lower = rowi >= colj
    lane_b = lax.broadcasted_iota(jnp.int32, (PAIR, PAIR), 1)
    row_b = lax.broadcasted_iota(jnp.int32, (PAIR, PAIR), 0)
    bdmask = (lane_b < DN_D) == (row_b < DN_D)

    def bd(xm):
        z0 = jnp.zeros_like(xm)
        return jnp.concatenate([jnp.where(even, xm, z0), jnp.where(even, z0, xm)], axis=0)

    eye_f = jnp.where(eye2, 1.0, 0.0)
    sub_masks = []
    s = 1
    while s < DN_CHUNK:
        same = (rowi // (2 * s)) == (colj // (2 * s))
        sub_masks.append(same & ((rowi // s) % 2 == 1) & ((colj // s) % 2 == 0))
        s *= 2

    def local_body(cc, carry):
        chains = [(cc * DN_UNROLL + dc, p) for dc in range(DN_UNROLL) for p in range(N_PAIRS)]
        n = len(chains)
        rows = [pl.ds(pl.multiple_of(c * DN_CHUNK, DN_CHUNK), DN_CHUNK) for c, _ in chains]
        cols = [slice(p * PAIR, (p + 1) * PAIR) for _, p in chains]
        ld = lambda ref, i: ref[rows[i], cols[i]]

        aq = [_dot_nt(jnp.concatenate([ld(kb_s, i), ld(q_s, i)], axis=0).astype(BF16),
                      bd(ld(k_s, i).astype(BF16))) for i in range(n)]
        a_qk, l_m = [], []
        for i in range(n):
            gcc = ld(gc_s, i)
            gcj = jnp.sum(jnp.where(eye2, gcc, 0.0), axis=0, keepdims=True)
            dec = jnp.where(lower, jnp.exp(jnp.minimum(gcc - gcj, 0.0)), 0.0)
            a_qk.append((aq[i][DN_CHUNK:, :] * dec).astype(BF16))
            l_m.append(jnp.where(eye2, 0.0, aq[i][:DN_CHUNK, :] * dec))

        lhl = [_split(m) for m in l_m]
        x0 = [eye_f - jnp.where(sub_masks[0], m, 0.0) for m in l_m]
        for lvl in range(1, len(sub_masks)):
            dc = [_dot(x0[i].astype(BF16), bd(jnp.where(sub_masks[lvl], lhl[i][0], jnp.zeros_like(lhl[i][0]))))
                  for i in range(n)]
            x0 = [x0[i] - _dot(dc[i].astype(BF16), bd(x0[i].astype(BF16))) for i in range(n)]
        xhl = [_split(m) for m in x0]
        res = []
        for i in range(n):
            bxh = bd(xhl[i][0])
            lx = _dot(lhl[i][0], bxh) + _dot(lhl[i][0], bd(xhl[i][1])) + _dot(lhl[i][1], bxh)
            res.append((eye_f - x0[i] - lx).astype(BF16))
        t_m = [x0[i] + _dot(xhl[i][0], bd(res[i])) for i in range(n)]

        uw = []
        for i in range(n):
            th, tl = _split(t_m[i])
            kbg = (ld(kb_s, i) * ld(eg_s, i)).astype(BF16)
            rhs = jnp.concatenate([bd(ld(vb_s, i).astype(BF16)), bd(kbg)], axis=1)
            uw.append((_dot(th, rhs) + _dot(tl, rhs)).astype(BF16))
        qo = [_dot(a_qk[i], jnp.concatenate([bd(uw[i][:, PAIR:]), bd(uw[i][:, :PAIR])], axis=1))
              for i in range(n)]
        mn = [_dot_tn(ld(kdec_s, i).astype(BF16), jnp.concatenate([uw[i][:, PAIR:], uw[i][:, :PAIR]], axis=1))
              for i in range(n)]
        for i, (c, p) in enumerate(chains):
            qe_s[rows[i], cols[i]] = (ld(q_s, i) * ld(eg_s, i) - qo[i][:, :PAIR]).astype(BF16)
            od_s[rows[i], cols[i]] = qo[i][:, PAIR:]
            m_s[c, p] = jnp.where(bdmask, mn[i][:, :PAIR], 0.0).astype(BF16)
            n_s[c, p] = jnp.where(bdmask, mn[i][:, PAIR:], 0.0)
        return carry

    lax.fori_loop(0, n_chunks // DN_UNROLL, local_body, 0)

    for c in range(n_chunks):
        rows = slice(c * DN_CHUNK, (c + 1) * DN_CHUNK)
        for p in range(N_PAIRS):
            cols = slice(p * PAIR, (p + 1) * PAIR)
            st = state_s[p]
            r = _dot(jnp.concatenate([qe_s[rows, cols], m_s[c, p]], axis=0), st.astype(BF16))
            od_s[rows, cols] += r[:DN_CHUNK, :]
            state_s[p] = gl_s[c * DN_CHUNK:c * DN_CHUNK + 1, cols] * st - r[DN_CHUNK:, :] + n_s[c, p]

    od = od_s[...]
    ms = _dot((od * od).astype(BF16), seg) * (1.0 / DN_D)
    zz = z_ref[...]
    o_ref[...] = (od * lax.rsqrt(ms + RMS_EPS) * par_ref[2:3, :] * (zz * jax.nn.sigmoid(zz))).astype(BF16)


def _deltanet(qkv, ab, z, convw, par, consts, bsz, seq):
    t = qkv.shape[0]
    ts = min(TS_DN, seq)
    nt = seq // ts
    hb = ts // HALO
    ea, eb, seg, tri, ones = consts
    row = lambda b, i: (b * nt + i, 0)
    const = lambda b, i: (0, 0)
    halo_map = lambda b, i: (jnp.maximum((b * nt + i) * hb - 1, 0), 0)
    cspec = lambda a: pl.BlockSpec(a.shape, const, pipeline_mode=pl.Buffered(1))
    big = lambda: pltpu.VMEM((ts, DN_W), F32)
    return pl.pallas_call(
        _dn_kernel,
        grid=(bsz, nt),
        in_specs=[pl.BlockSpec((ts, 3 * DN_W), row), pl.BlockSpec((HALO, 3 * DN_W), halo_map),
                  pl.BlockSpec((ts, LANES), row), pl.BlockSpec((ts, DN_W), row),
                  cspec(convw), cspec(par), cspec(ea), cspec(eb), cspec(seg), cspec(tri), cspec(ones)],
        out_specs=pl.BlockSpec((ts, DN_W), row),
        out_shape=jax.ShapeDtypeStruct((t, DN_W), BF16),
        scratch_shapes=[pltpu.VMEM((ts + HALO, 3 * DN_W), F32)] + [big() for _ in range(9)]
                       + [pltpu.VMEM((ts, DN_W), BF16),
                          pltpu.VMEM((ts // DN_CHUNK, N_PAIRS, PAIR, PAIR), BF16),
                          pltpu.VMEM((ts // DN_CHUNK, N_PAIRS, PAIR, PAIR), F32),
                          pltpu.VMEM((N_PAIRS, PAIR, PAIR), F32)],
        compiler_params=_cparams(("parallel", "arbitrary")),
        name="deltanet",
    )(qkv, qkv, ab, z, convw, par, ea, eb, seg, tri, ones)


def _df_kernel(sc_ref, q_ref, k_ref, vt_ref, bt_ref, w_ref, o_ref, sa_ref, sb_ref, acc_ref):
    qi = pl.program_id(2)
    tq = q_ref.shape[0]
    lane = lax.broadcasted_iota(jnp.int32, (tq, 2 * DF_D), 1)
    q = q_ref[...]
    zq = jnp.zeros_like(q)
    qs = (jnp.where(lane < DF_D, q, zq), jnp.where(lane < DF_D, zq, q))
    acc_ref[...] = jnp.zeros_like(acc_ref)

    def scores(j, s_ref):
        kk = k_ref[pl.ds(pl.multiple_of(j * tq, tq), tq), :]
        for mp in range(2):
            s_ref[mp] = _dot_nt(kk, qs[mp])

    def absorb(j, s_ref, carry):
        vt = vt_ref[:, pl.ds(pl.multiple_of(j * tq, tq), tq)]
        out = []
        for mp in range(2):
            m_old, l_old = carry[mp]
            st = s_ref[mp]
            m_new = jnp.maximum(m_old, jnp.max(st, axis=0, keepdims=True))
            alpha = jnp.exp2(m_old - m_new)
            pr = jnp.exp2(st - m_new)
            l_new = alpha * l_old + jnp.sum(pr, axis=0, keepdims=True)
            acc_ref[mp] = alpha * acc_ref[mp] + _dot(vt, pr.astype(BF16))
            out.append((m_new, l_new))
        return tuple(out)

    def absorb_prev(j, s_ref, carry):
        c0 = tq - LANES
        for mp in range(2):
            s_ref[mp, c0:, 0:LANES] += bt_ref[1, c0:, 0:LANES]
        return absorb(j, s_ref, carry)

    bw = tq // DF_DIAG_BANDS
    d0 = pl.multiple_of(qi * tq, tq)
    band_keys = [pl.ds(pl.multiple_of(d0 + r * bw, bw), bw) for r in range(DF_DIAG_BANDS)]

    def scores_diag(s_ref):
        for r in range(DF_DIAG_BANDS):
            c0 = r * bw
            kk = k_ref[band_keys[r], :]
            for mp in range(2):
                s_ref[mp, c0:c0 + bw, c0:] = _dot_nt(kk, qs[mp][c0:, :])

    def absorb_diag(s_ref, carry):
        out = []
        for mp in range(2):
            m_old, l_old = carry[mp]
            bands = []
            m_new = m_old
            for r in range(DF_DIAG_BANDS):
                c0 = r * bw
                band = s_ref[mp, c0:c0 + bw, c0:] + bt_ref[0, c0:c0 + bw, c0:]
                bands.append(band)
                m_hi = jnp.maximum(m_new[:, c0:], jnp.max(band, axis=0, keepdims=True))
                m_new = m_hi if c0 == 0 else jnp.concatenate([m_new[:, :c0], m_hi], axis=1)
            alpha = jnp.exp2(m_old - m_new)
            acc_ref[mp] = alpha * acc_ref[mp]
            l_new = alpha * l_old
            for r in range(DF_DIAG_BANDS):
                c0 = r * bw
                pr = jnp.exp2(bands[r] - m_new[:, c0:])
                l_band = jnp.sum(pr, axis=0, keepdims=True)
                l_new = l_new + (l_band if c0 == 0 else
                                 jnp.concatenate([jnp.zeros((1, c0), F32), l_band], axis=1))
                acc_ref[mp, :, c0:] += _dot(vt_ref[:, band_keys[r]], pr.astype(BF16))
            out.append((m_new, l_new))
        return tuple(out)

    init1 = (jnp.full((1, tq), -jnp.inf, F32), jnp.zeros((1, tq), F32))
    carry = (init1, init1)
    n_far = jnp.maximum(qi - 1, 0)
    odd = n_far % 2

    @pl.when(qi == 0)
    def _():
        scores_diag(sb_ref)

    @pl.when(jnp.logical_and(qi > 0, odd == 0))
    def _():
        scores(0, sa_ref)

    def odd_step(_, c):
        scores(0, sb_ref)
        scores(1, sa_ref)
        return absorb(0, sb_ref, c)

    carry = lax.fori_loop(0, odd, odd_step, carry)

    def pair_step(jp, c):
        j = odd + 2 * jp
        scores(j + 1, sb_ref)
        c = absorb(j, sa_ref, c)
        scores(j + 2, sa_ref)
        return absorb(j + 1, sb_ref, c)

    carry = lax.fori_loop(0, n_far // 2, pair_step, carry)

    def tail_step(j, c):
        scores_diag(sb_ref)
        c = absorb_prev(j, sa_ref, c)
        return absorb_diag(sb_ref, c)

    carry = lax.fori_loop(n_far, qi, tail_step, carry)
    carry = lax.fori_loop(0, jnp.where(qi == 0, 1, 0), lambda _, c: absorb_diag(sb_ref, c), carry)

    (_, l0), (_, l1) = carry
    ot = acc_ref[0] / l0 - sc_ref[0] * (acc_ref[1] / l1)
    ms = jnp.mean(ot * ot, axis=0, keepdims=True)
    ot = ot * lax.rsqrt(ms + RMS_EPS) * w_ref[...] * sc_ref[1]
    o_ref[...] = ot.T.astype(BF16)


def _attn_bias_tiles(rel_bias, seq, tq):
    rb = rel_bias.astype(F32)
    far = rb[_t5_causal_bucket(jnp.asarray(seq - 1, jnp.int32))]
    kj = jnp.arange(tq, dtype=jnp.int32)[:, None]
    qi = jnp.arange(tq, dtype=jnp.int32)[None, :]
    dist = jnp.stack([qi - kj, qi + tq - kj])
    onehot = (_t5_causal_bucket(jnp.maximum(dist, 0))[..., None]
              == jnp.arange(REL_BUCKETS, dtype=jnp.int32)).astype(F32)
    vals = (jnp.einsum('ntqb,bh->ntqh', onehot, rb, precision=lax.Precision.HIGHEST) - far) * LOG2E
    vals = jnp.where((dist >= 0)[..., None], vals, -jnp.inf)
    return jnp.transpose(vals, (3, 0, 1, 2))


def _diff_attention(qd, kd, vdt, btiles, subln_w, scalars, bsz, seq):
    t = qd.shape[0]
    tq = min(TQ_DF, seq)
    nq = seq // tq
    sbuf = pltpu.VMEM((2, tq, tq), F32)
    return pl.pallas_call(
        _df_kernel,
        grid_spec=pltpu.PrefetchScalarGridSpec(
            num_scalar_prefetch=1,
            grid=(bsz, DF_HEADS, nq),
            in_specs=[pl.BlockSpec((tq, 2 * DF_D), lambda b, h, i, sc: (b * nq + i, h)),
                      pl.BlockSpec((seq, 2 * DF_D), lambda b, h, i, sc: (b, h)),
                      pl.BlockSpec((None, 2 * DF_D, seq), lambda b, h, i, sc: (b, h, 0)),
                      pl.BlockSpec((None, 2, tq, tq), lambda b, h, i, sc: (h, 0, 0, 0)),
                      pl.BlockSpec((2 * DF_D, 1), lambda b, h, i, sc: (0, 0))],
            out_specs=pl.BlockSpec((tq, 2 * DF_D), lambda b, h, i, sc: (b * nq + i, h)),
            scratch_shapes=[sbuf, sbuf, pltpu.VMEM((2, 2 * DF_D, tq), F32)],
        ),
        out_shape=jax.ShapeDtypeStruct((t, DF_W), BF16),
        compiler_params=_cparams(("parallel", "parallel", "arbitrary")),
        name="diffattn",
    )(scalars, qd, kd, vdt, btiles, subln_w)


def _outproj_kernel(alpha, x_ref, a_ref, b_ref, w_ref, g_ref, bb_ref, o_ref):
    mix = _dot(a_ref[...], w_ref[0:DN_W, :]) + _dot(b_ref[...], w_ref[DN_W:, :])
    o_ref[...] = _layer_norm(alpha * x_ref[...] + mix, g_ref[...], bb_ref[...])


def _outproj_ln(x2, o_dn, o_df, w_out, g, b, alpha):
    t, d = x2.shape
    tm = min(TM_PROJ, t)
    row = lambda i: (i, 0)
    const = lambda i: (0, 0)
    return pl.pallas_call(
        functools.partial(_outproj_kernel, alpha),
        grid=(t // tm,),
        in_specs=[pl.BlockSpec((tm, d), row), pl.BlockSpec((tm, DN_W), row), pl.BlockSpec((tm, DF_W), row),
                  pl.BlockSpec(w_out.shape, const, pipeline_mode=pl.Buffered(1)),
                  pl.BlockSpec((1, d), const), pl.BlockSpec((1, d), const)],
        out_specs=pl.BlockSpec((tm, d), row),
        out_shape=jax.ShapeDtypeStruct((t, d), F32),
        compiler_params=_cparams(("parallel",)),
        name="outproj_ln",
    )(x2, o_dn, o_df, w_out, g, b)


def _swiglu_acc(xparts, wg_ref, wu_ref, wd_ref, acc_ref):
    d_ff = wg_ref.shape[-1]
    for c0 in range(0, d_ff, FF_CHUNK):
        cs = slice(c0, c0 + FF_CHUNK)
        hg = sum(_dot(xp, wg_ref[k0:k0 + xp.shape[1], cs]) for xp, k0 in xparts)
        hu = sum(_dot(xp, wu_ref[k0:k0 + xp.shape[1], cs]) for xp, k0 in xparts)
        hh = (hg * jax.nn.sigmoid(hg) * hu).astype(BF16)
        contrib = _dot(hh, wd_ref[cs, :])
        if c0 == 0:
            acc_ref[...] = contrib
        else:
            acc_ref[...] += contrib


def _ffn_kernel(alpha, x_ref, wg_ref, wu_ref, wd_ref, g_ref, b_ref, o_ref, acc_ref):
    x = x_ref[...]
    _swiglu_acc([(x.astype(BF16), 0)], wg_ref, wu_ref, wd_ref, acc_ref)
    o_ref[...] = _layer_norm(alpha * x + acc_ref[...], g_ref[...], b_ref[...])


def _ffn_ln(x2, wg, wu, wd, g, b, alpha):
    t, d = x2.shape
    tm = min(TM_FFN, t)
    row = lambda i: (i, 0)
    const = lambda i: (0, 0)
    wspec = lambda a: pl.BlockSpec(a.shape, const, pipeline_mode=pl.Buffered(1))
    return pl.pallas_call(
        functools.partial(_ffn_kernel, alpha),
        grid=(t // tm,),
        in_specs=[pl.BlockSpec((tm, d), row), wspec(wg), wspec(wu), wspec(wd),
                  pl.BlockSpec((1, d), const), pl.BlockSpec((1, d), const)],
        out_specs=pl.BlockSpec((tm, d), row),
        out_shape=jax.ShapeDtypeStruct((t, d), F32),
        scratch_shapes=[pltpu.VMEM((tm, d), F32)],
        compiler_params=_cparams(("parallel",)),
        name="ffn_ln",
    )(x2, wg, wu, wd, g, b)


def _pack_halves(x):
    h = x.shape[1] // 2
    hi = lax.bitcast_convert_type(x[:, :h].astype(BF16).astype(F32), jnp.uint32)
    lo = lax.bitcast_convert_type(x[:, h:].astype(BF16).astype(F32), jnp.uint32)
    return hi | (lo >> 16)


def _unpack_halves(p):
    a = lax.bitcast_convert_type(p & jnp.uint32(0xFFFF0000), F32)
    b = lax.bitcast_convert_type(p << 16, F32)
    return a, b


def _router_kernel(x_ref, wr_ref, o_ref, xp_ref):
    x = x_ref[...]
    xp_ref[...] = _pack_halves(x)
    logits = _mm3(x, wr_ref[...])
    lane = lax.broadcasted_iota(jnp.int32, logits.shape, 1)
    lg = jnp.where(lane < N_EXPERTS, logits, -jnp.inf)
    m1 = jnp.max(lg, axis=-1, keepdims=True)
    i1 = jnp.min(jnp.where(lg == m1, lane, LANES), axis=-1, keepdims=True)
    lg2 = jnp.where(lane == i1, -jnp.inf, lg)
    m2 = jnp.max(lg2, axis=-1, keepdims=True)
    i2 = jnp.min(jnp.where(lg2 == m2, lane, LANES), axis=-1, keepdims=True)
    e = jnp.exp(m2 - m1)
    g1 = 1.0 / (1.0 + e)
    g2 = e / (1.0 + e)
    out = jnp.where(lane == 0, i1.astype(F32), 0.0)
    out = jnp.where(lane == 1, i2.astype(F32), out)
    out = jnp.where(lane == 2, g1, out)
    o_ref[...] = jnp.where(lane == 3, g2, out)


def _router(x2, wr):
    t, d = x2.shape
    tm = min(TM_PROJ, t)
    row = lambda i: (i, 0)
    return pl.pallas_call(
        _router_kernel,
        grid=(t // tm,),
        in_specs=[pl.BlockSpec((tm, d), row), pl.BlockSpec((d, LANES), lambda i: (0, 0))],
        out_specs=[pl.BlockSpec((tm, LANES), row), pl.BlockSpec((tm, d // 2), row)],
        out_shape=[jax.ShapeDtypeStruct((t, LANES), F32), jax.ShapeDtypeStruct((t, d // 2), jnp.uint32)],
        compiler_params=_cparams(("parallel",)),
        name="router",
    )(x2, wr)


def _sc_mesh():
    return plsc.VectorSubcoreMesh(core_axis_name="core", subcore_axis_name="subcore")


def _sc_index_rows(idx):
    win = idx.reshape(-1, SC_WIN)
    return jnp.concatenate([win, jnp.zeros((win.shape[0], SC_IDX_TILE - SC_WIN), idx.dtype)], axis=1)


def _sc_dispatch(xp, slot0, slot1, n_slot):
    t, c = xp.shape
    half = t // SC_WIN // 2
    idx_spec = pl.BlockSpec((1, SC_IDX_TILE), lambda cc, i: (cc * half + i, 0))

    @pl.kernel(out_type=jax.ShapeDtypeStruct((n_slot, c), xp.dtype), mesh=_sc_mesh(), scratch_types=[],
               name="moe_dispatch")
    def run(x_hbm, i0_hbm, i1_hbm, o_hbm):
        def body(x_vmem, i0_vmem, i1_vmem):
            pltpu.sync_copy(x_vmem, o_hbm.at[i0_vmem.at[0, pl.ds(0, SC_WIN)]])
            pltpu.sync_copy(x_vmem, o_hbm.at[i1_vmem.at[0, pl.ds(0, SC_WIN)]])

        pltpu.emit_pipeline(
            body,
            grid=(2, half),
            in_specs=[pl.BlockSpec((SC_WIN, c), lambda cc, i: (cc * half + i, 0)), idx_spec, idx_spec],
            out_specs=[],
            core_axis_name=("core", "subcore"),
            dimension_semantics=(pltpu.PARALLEL, pltpu.PARALLEL),
        )(x_hbm, i0_hbm, i1_hbm)

    return run(xp, _sc_index_rows(slot0), _sc_index_rows(slot1))


def _sc_gather(yp, idx):
    t = idx.shape[0]
    c = yp.shape[1]
    half = t // SC_WIN // 2

    @pl.kernel(out_type=jax.ShapeDtypeStruct((t, c), yp.dtype), mesh=_sc_mesh(), scratch_types=[],
               name="moe_gather")
    def run(y_hbm, i_hbm, o_hbm):
        def body(i_vmem, o_vmem):
            pltpu.sync_copy(y_hbm.at[i_vmem.at[0, pl.ds(0, SC_WIN)]], o_vmem)

        pltpu.emit_pipeline(
            body,
            grid=(2, half),
            in_specs=[pl.BlockSpec((1, SC_IDX_TILE), lambda cc, i: (cc * half + i, 0))],
            out_specs=[pl.BlockSpec((SC_WIN, c), lambda cc, i: (cc * half + i, 0))],
            core_axis_name=("core", "subcore"),
            dimension_semantics=(pltpu.PARALLEL, pltpu.PARALLEL),
        )(i_hbm, o_hbm)

    return run(yp, _sc_index_rows(idx))


def _expert_kernel(be_ref, nv_ref, x_ref, wg_ref, wu_ref, wd_ref, o_ref, acc_ref):
    i = pl.program_id(0)
    n_valid = nv_ref[i]

    @pl.when(n_valid > 0)
    def _():
        row = lax.broadcasted_iota(jnp.int32, x_ref.shape, 0)
        xa, xb = _unpack_halves(jnp.where(row < n_valid, x_ref[...], jnp.uint32(0)))
        h = xa.shape[1]
        _swiglu_acc([(xa.astype(BF16), 0), (xb.astype(BF16), h)], wg_ref, wu_ref, wd_ref, acc_ref)
        o_ref[...] = _pack_halves(acc_ref[...])

    @pl.when(n_valid <= 0)
    def _():
        o_ref[...] = jnp.zeros_like(o_ref)


def _experts(xbp, blk_e, n_valid, wg, wu, wd):
    n_slot, dh = xbp.shape
    d = 2 * dh
    n_blk = n_slot // MOE_BLK
    dff = wg.shape[-1]
    row = lambda i, be, nv: (i, 0)
    wmap = lambda i, be, nv: (be[i], 0, 0)
    return pl.pallas_call(
        _expert_kernel,
        grid_spec=pltpu.PrefetchScalarGridSpec(
            num_scalar_prefetch=2,
            grid=(n_blk,),
            in_specs=[pl.BlockSpec((MOE_BLK, dh), row),
                      pl.BlockSpec((None, d, dff), wmap, pipeline_mode=pl.Buffered(1)),
                      pl.BlockSpec((None, d, dff), wmap, pipeline_mode=pl.Buffered(1)),
                      pl.BlockSpec((None, dff, d), wmap, pipeline_mode=pl.Buffered(1))],
            out_specs=pl.BlockSpec((MOE_BLK, dh), row),
            scratch_shapes=[pltpu.VMEM((MOE_BLK, d), F32)],
        ),
        out_shape=jax.ShapeDtypeStruct((n_slot, dh), jnp.uint32),
        compiler_params=_cparams(("arbitrary",)),
        name="experts",
    )(blk_e, n_valid, xbp, wg, wu, wd)


def _combine_kernel(alpha, x_ref, y0_ref, y1_ref, r_ref, g_ref, b_ref, o_ref):
    r = r_ref[...]
    a0, b0 = _unpack_halves(y0_ref[...])
    a1, b1 = _unpack_halves(y1_ref[...])
    g0 = r[:, 2:3]
    g1 = r[:, 3:4]
    f = jnp.concatenate([g0 * a0 + g1 * a1, g0 * b0 + g1 * b1], axis=1)
    o_ref[...] = _layer_norm(alpha * x_ref[...] + f, g_ref[...], b_ref[...])


def _combine_ln(x2, y0, y1, route, g, b, alpha):
    t, d = x2.shape
    tm = min(TM_PROJ, t)
    row = lambda i: (i, 0)
    const = lambda i: (0, 0)
    return pl.pallas_call(
        functools.partial(_combine_kernel, alpha),
        grid=(t // tm,),
        in_specs=[pl.BlockSpec((tm, d), row), pl.BlockSpec((tm, d // 2), row), pl.BlockSpec((tm, d // 2), row),
                  pl.BlockSpec((tm, LANES), row), pl.BlockSpec((1, d), const), pl.BlockSpec((1, d), const)],
        out_specs=pl.BlockSpec((tm, d), row),
        out_shape=jax.ShapeDtypeStruct((t, d), F32),
        compiler_params=_cparams(("parallel",)),
        name="combine_ln",
    )(x2, y0, y1, route, g, b)


def _moe(x2, wr, wg, wu, wd, g, b, alpha):
    t, d = x2.shape
    route, xp = _router(x2, wr)
    top_idx = route[:, 0:2].astype(jnp.int32)
    n_asg = t * TOP_K
    flat_e = top_idx.reshape(n_asg)
    onehot = (flat_e[:, None] == jnp.arange(N_EXPERTS, dtype=jnp.int32)[None, :]).astype(jnp.int32)
    csum = jnp.cumsum(onehot, axis=0)
    rank = jnp.sum((csum - onehot) * onehot, axis=1)
    counts = csum[-1]
    padded = (counts + MOE_BLK - 1) // MOE_BLK * MOE_BLK
    pad_end = jnp.cumsum(padded)
    pad_start = pad_end - padded
    slot = (pad_start[flat_e] + rank).reshape(t, TOP_K)
    n_slot = -(-n_asg // MOE_BLK) * MOE_BLK + N_EXPERTS * MOE_BLK
    n_blk = n_slot // MOE_BLK
    blk_start = jnp.arange(n_blk, dtype=jnp.int32) * MOE_BLK
    blk_e = jnp.minimum(jnp.searchsorted(pad_end, blk_start, side='right'), N_EXPERTS - 1).astype(jnp.int32)
    n_valid = jnp.clip(pad_start[blk_e] + counts[blk_e] - blk_start, 0, MOE_BLK).astype(jnp.int32)
    slot0 = slot[:, 0]
    slot1 = slot[:, 1]
    xbp = _sc_dispatch(xp, slot0, slot1, n_slot)
    ybp = _experts(xbp, blk_e, n_valid, wg, wu, wd)
    return _combine_ln(x2, _sc_gather(ybp, slot0), _sc_gather(ybp, slot1), route, g, b, alpha)


def _t5_causal_bucket(dist):
    max_exact = REL_BUCKETS // 2
    d = jnp.maximum(dist, 1).astype(F32)
    large = max_exact + (jnp.log(d / max_exact) / math.log(REL_MAX_DIST / max_exact)
                         * (REL_BUCKETS - max_exact)).astype(jnp.int32)
    large = jnp.minimum(large, REL_BUCKETS - 1)
    return jnp.where(dist < max_exact, dist, large)


def _dn_constants(ts):
    lanes = np.arange(DN_W)
    ea = np.zeros((LANES, DN_W), np.float32)
    eb = np.zeros((LANES, DN_W), np.float32)
    ea[lanes // DN_D, lanes] = 1.0
    eb[DN_HEADS + lanes // DN_D, lanes] = 1.0
    seg = (lanes[:, None] // DN_D == lanes[None, :] // DN_D).astype(np.float32)
    r = np.arange(ts)
    same = r[:, None] // DN_CHUNK == r[None, :] // DN_CHUNK
    tri = (same & (r[:, None] >= r[None, :])).astype(np.float32)
    ones = same.astype(np.float32)
    return tuple(jnp.asarray(a, BF16) for a in (ea, eb, seg, tri, ones))


def kernel(x, w_in, w_out, conv_w, dn_a_log, dn_dt_bias, dn_norm_w, df_lambda, df_subln_w, rel_bias,
           ln1_g, ln1_b, ln2_g, ln2_b, ffn_w_gate, ffn_w_up, ffn_w_down, moe_router, moe_w_gate,
           moe_w_up, moe_w_down):
    bsz, seq, d = x.shape
    depth = w_in.shape[0]
    alpha = (2 * depth) ** 0.25
    t = bsz * seq
    tq = min(TQ_DF, seq)
    assert REL_MAX_DIST <= LANES and tq % (DF_DIAG_BANDS * LANES) == 0
    assert seq % tq == 0 and seq % min(TS_DN, seq) == 0
    btiles = _attn_bias_tiles(rel_bias, seq, tq)

    dn_consts = _dn_constants(min(TS_DN, seq))
    c_dn = 3 * DN_W
    x2 = x.reshape(t, d)
    for layer in range(depth):
        lambda_init = 0.8 - 0.6 * math.exp(-0.3 * layer)
        wl = w_in[layer]
        n1 = c_dn + DN_W + 2 * DN_HEADS
        w1 = jnp.concatenate([wl[:, :n1], jnp.zeros((d, LANES - 2 * DN_HEADS), F32)], axis=1).astype(BF16)
        w2 = wl[:, n1:n1 + 2 * DF_W].astype(BF16)
        wvt = wl[:, n1 + 2 * DF_W:].T.astype(BF16)
        qkv, z, ab, qd, kd, vdt = _inproj(x2, w1, w2, wvt, bsz, seq)

        convw = jnp.concatenate([conv_w[layer], jnp.zeros((HALO - DN_CONV, c_dn), F32)], axis=0)
        par = jnp.zeros((HALO, DN_W), F32)
        par = par.at[0, 0:DN_HEADS].set(dn_a_log[layer])
        par = par.at[1, 0:DN_HEADS].set(dn_dt_bias[layer])
        par = par.at[2].set(jnp.tile(dn_norm_w[layer], DN_HEADS))
        o_dn = _deltanet(qkv, ab, z, convw, par, dn_consts, bsz, seq)

        lf = df_lambda[layer].astype(F32)
        lam = jnp.exp(jnp.sum(lf[0] * lf[1])) - jnp.exp(jnp.sum(lf[2] * lf[3])) + lambda_init
        scalars = jnp.stack([lam, jnp.asarray(1.0 - lambda_init, F32)]).astype(F32)
        o_df = _diff_attention(qd, kd, vdt, btiles, df_subln_w[layer].reshape(2 * DF_D, 1), scalars, bsz, seq)

        i = layer // 2
        g1 = ln1_g[layer].reshape(1, d)
        b1 = ln1_b[layer].reshape(1, d)
        g2 = ln2_g[layer].reshape(1, d)
        b2 = ln2_b[layer].reshape(1, d)
        x2 = _outproj_ln(x2, o_dn, o_df, w_out[layer].astype(BF16), g1, b1, alpha)
        if layer % 2 == 0:
            x2 = _ffn_ln(x2, ffn_w_gate[i].astype(BF16), ffn_w_up[i].astype(BF16), ffn_w_down[i].astype(BF16),
                         g2, b2, alpha)
        else:
            wr = jnp.concatenate([moe_router[i], jnp.zeros((d, LANES - N_EXPERTS), F32)], axis=1)
            x2 = _moe(x2, wr, moe_w_gate[i].astype(BF16), moe_w_up[i].astype(BF16), moe_w_down[i].astype(BF16),
                      g2, b2, alpha)
    return x2.reshape(bsz, seq, d)
```

```python
import functools
import math

import jax
import jax.numpy as jnp
import numpy as np
from jax import lax
from jax.experimental import pallas as pl
from jax.experimental.pallas import tpu as pltpu
from jax.experimental.pallas import tpu_sc as plsc

F32 = jnp.float32
BF16 = jnp.bfloat16

DN_HEADS = 8
DN_D = 64
DN_CONV = 4
DN_CHUNK = 64
DN_W = DN_HEADS * DN_D
DF_HEADS = 4
DF_D = 64
DF_W = DF_HEADS * 2 * DF_D
REL_BUCKETS = 32
REL_MAX_DIST = 128
N_EXPERTS = 8
TOP_K = 2
MOE_BLK = 512
LN_EPS = 1e-5
RMS_EPS = 1e-6
LOG2E = math.log2(math.e)

LANES = 128
HALO = 8
PAIR = 2 * DN_D
N_PAIRS = DN_HEADS // 2
VMEM_LIMIT = 56 * 1024 * 1024

TM_PROJ = 512
TS_DN = 256
DN_UNROLL = 4
TQ_DF = 1024
DF_DIAG_BANDS = 4
TM_FFN = 512
FF_CHUNK = 256
SC_WIN = 64
SC_IDX_TILE = 128


def _cparams(sem):
    return pltpu.CompilerParams(dimension_semantics=sem, vmem_limit_bytes=VMEM_LIMIT)


def _dot(a, b):
    return jnp.dot(a, b, preferred_element_type=F32)


def _dot_nt(a, b):
    return lax.dot_general(a, b, (((1,), (1,)), ((), ())), preferred_element_type=F32)


def _dot_tn(a, b):
    return lax.dot_general(a, b, (((0,), (0,)), ((), ())), preferred_element_type=F32)


def _split(x):
    hi = x.astype(BF16)
    lo = (x - hi.astype(F32)).astype(BF16)
    return hi, lo


def _mm_xc(x, c):
    hi, lo = _split(x)
    return _dot(hi, c) + _dot(lo, c)


def _mm_cx(c, x):
    hi, lo = _split(x)
    return _dot(c, hi) + _dot(c, lo)


def _mm3(a, b):
    ah, al = _split(a)
    bh, bl = _split(b)
    return _dot(ah, bh) + _dot(ah, bl) + _dot(al, bh)


def _layer_norm(v, g, b):
    mu = jnp.mean(v, axis=-1, keepdims=True)
    d = v - mu
    var = jnp.mean(d * d, axis=-1, keepdims=True)
    return d * lax.rsqrt(var + LN_EPS) * g + b


def _inproj_kernel(x_ref, w1_ref, w2_ref, wvt_ref, qkv_ref, z_ref, ab_ref, qd_ref, kd_ref, vd_ref):
    xb = x_ref[...].astype(BF16)
    c = 3 * DN_W
    qkv_ref[...] = _dot(xb, w1_ref[:, 0:c])
    z_ref[...] = _dot(xb, w1_ref[:, c:c + DN_W])
    ab_ref[...] = _dot(xb, w1_ref[:, c + DN_W:c + DN_W + LANES])
    qd_ref[...] = (_dot(xb, w2_ref[:, 0:DF_W]) * (DF_D ** -0.5 * LOG2E)).astype(BF16)
    kd_ref[...] = _dot(xb, w2_ref[:, DF_W:2 * DF_W]).astype(BF16)
    vd_ref[...] = _dot_nt(wvt_ref[...], xb).astype(BF16)


def _inproj(x2, w1, w2, wvt, bsz, seq):
    t, d = x2.shape
    tm = min(TM_PROJ, seq)
    nt = seq // tm
    n1, n2 = w1.shape[1], w2.shape[1]
    row = lambda i: (i, 0)
    const = lambda i: (0, 0)
    return pl.pallas_call(
        _inproj_kernel,
        grid=(t // tm,),
        in_specs=[pl.BlockSpec((tm, d), row),
                  pl.BlockSpec((d, n1), const, pipeline_mode=pl.Buffered(1)),
                  pl.BlockSpec((d, n2), const, pipeline_mode=pl.Buffered(1)),
                  pl.BlockSpec((DF_W, d), const, pipeline_mode=pl.Buffered(1))],
        out_specs=[pl.BlockSpec((tm, 3 * DN_W), row), pl.BlockSpec((tm, DN_W), row),
                   pl.BlockSpec((tm, LANES), row), pl.BlockSpec((tm, DF_W), row),
                   pl.BlockSpec((tm, DF_W), row),
                   pl.BlockSpec((None, DF_W, tm), lambda i: (i // nt, 0, i % nt))],
        out_shape=[jax.ShapeDtypeStruct((t, 3 * DN_W), F32), jax.ShapeDtypeStruct((t, DN_W), F32),
                   jax.ShapeDtypeStruct((t, LANES), F32), jax.ShapeDtypeStruct((t, DF_W), BF16),
                   jax.ShapeDtypeStruct((t, DF_W), BF16), jax.ShapeDtypeStruct((bsz, DF_W, seq), BF16)],
        compiler_params=_cparams(("parallel",)),
        name="inproj",
    )(x2, w1, w2, wvt)


def _dn_kernel(qkv_ref, halo_ref, ab_ref, z_ref, convw_ref, par_ref, ea_ref, eb_ref, seg_ref,
               tri_ref, ones_ref, o_ref,
               xe_s, q_s, k_s, kb_s, vb_s, gc_s, eg_s, kdec_s, gl_s, od_s, qe_s, m_s, n_s, state_s):
    i = pl.program_id(1)
    ts = qkv_ref.shape[0]
    n_chunks = ts // DN_CHUNK

    @pl.when(i == 0)
    def _():
        state_s[...] = jnp.zeros_like(state_s)

    halo = halo_ref[...]
    xe_s[0:HALO, :] = jnp.where(i > 0, halo, jnp.zeros_like(halo))
    xe_s[HALO:, :] = qkv_ref[...]
    xe = xe_s[...]
    y = convw_ref[DN_CONV - 1:DN_CONV, :] * xe[HALO:, :]
    for tap in range(DN_CONV - 1):
        y += convw_ref[tap:tap + 1, :] * pltpu.roll(xe, DN_CONV - 1 - tap, axis=0)[HALO:, :]
    y = y * jax.nn.sigmoid(y)
    q = y[:, 0:DN_W]
    k = y[:, DN_W:2 * DN_W]
    v = y[:, 2 * DN_W:3 * DN_W]

    seg = seg_ref[...]
    q = q * lax.rsqrt(_dot((q * q).astype(BF16), seg) + 1e-6) * (DN_D ** -0.5)
    k = k * lax.rsqrt(_dot((k * k).astype(BF16), seg) + 1e-6)

    ab = ab_ref[...]
    xa = ab + par_ref[1:2, 0:LANES]
    softplus = jnp.maximum(xa, 0.0) + jnp.log(1.0 + jnp.exp(-jnp.abs(xa)))
    g = _mm_xc(-jnp.exp(par_ref[0:1, 0:LANES]) * softplus, ea_ref[...])
    beta = _mm_xc(jax.nn.sigmoid(ab), eb_ref[...])
    gc = _mm_cx(tri_ref[...], g)
    gl = _mm_cx(ones_ref[...], g)
    eg = jnp.exp(gc)
    kb = k * beta
    q_s[...] = q
    k_s[...] = k
    kb_s[...] = kb
    vb_s[...] = v * beta
    gc_s[...] = gc
    eg_s[...] = eg
    kdec_s[...] = k * jnp.exp(gl - gc)
    gl_s[...] = jnp.exp(gl)

    lane = lax.broadcasted_iota(jnp.int32, (DN_CHUNK, PAIR), 1)
    rowi = lax.broadcasted_iota(jnp.int32, (DN_CHUNK, PAIR), 0)
    colj = jnp.where(lane >= DN_D, lane - DN_D, lane)
    even = lane < DN_D
    eye2 = rowi == colj
    lower = rowi >= colj
    lane_b = lax.broadcasted_iota(jnp.int32, (PAIR, PAIR), 1)
    row_b = lax.broadcasted_iota(jnp.int32, (PAIR, PAIR), 0)
    bdmask = (lane_b < DN_D) == (row_b < DN_D)

    def bd(xm):
        z0 = jnp.zeros_like(xm)
        return jnp.concatenate([jnp.where(even, xm, z0), jnp.where(even, z0, xm)], axis=0)

    eye_f = jnp.where(eye2, 1.0, 0.0)
    sub_masks = []
    s = 1
    while s < DN_CHUNK:
        same = (rowi // (2 * s)) == (colj // (2 * s))
        sub_masks.append(same & ((rowi // s) % 2 == 1) & ((colj // s) % 2 == 0))
        s *= 2

    def local_body(cc, carry):
        chains = [(cc * DN_UNROLL + dc, p) for dc in range(DN_UNROLL) for p in range(N_PAIRS)]
        n = len(chains)
        rows = [pl.ds(pl.multiple_of(c * DN_CHUNK, DN_CHUNK), DN_CHUNK) for c, _ in chains]
        cols = [slice(p * PAIR, (p + 1) * PAIR) for _, p in chains]
        ld = lambda ref, i: ref[rows[i], cols[i]]

        aq = [_dot_nt(jnp.concatenate([ld(kb_s, i), ld(q_s, i)], axis=0).astype(BF16),
                      bd(ld(k_s, i).astype(BF16))) for i in range(n)]
        a_qk, l_m = [], []
        for i in range(n):
            gcc = ld(gc_s, i)
            gcj = jnp.sum(jnp.where(eye2, gcc, 0.0), axis=0, keepdims=True)
            dec = jnp.where(lower, jnp.exp(jnp.minimum(gcc - gcj, 0.0)), 0.0)
            a_qk.append((aq[i][DN_CHUNK:, :] * dec).astype(BF16))
            l_m.append(jnp.where(eye2, 0.0, aq[i][:DN_CHUNK, :] * dec))

        lb = [m.astype(BF16) for m in l_m]
        t_m = [eye_f - jnp.where(sub_masks[0], m, 0.0) for m in l_m]
        for lvl in range(1, len(sub_masks)):
            dc = [_dot(t_m[i].astype(BF16), bd(jnp.where(sub_masks[lvl], lb[i], jnp.zeros_like(lb[i]))))
                  for i in range(n)]
            t_m = [t_m[i] - _dot(dc[i].astype(BF16), bd(t_m[i].astype(BF16))) for i in range(n)]

        uw = []
        for i in range(n):
            kbg = (ld(kb_s, i) * ld(eg_s, i)).astype(BF16)
            rhs = jnp.concatenate([bd(ld(vb_s, i).astype(BF16)), bd(kbg)], axis=1)
            uw.append(_dot(t_m[i].astype(BF16), rhs).astype(BF16))
        qo = [_dot(a_qk[i], jnp.concatenate([bd(uw[i][:, PAIR:]), bd(uw[i][:, :PAIR])], axis=1))
              for i in range(n)]
        mn = [_dot_tn(ld(kdec_s, i).astype(BF16), jnp.concatenate([uw[i][:, PAIR:], uw[i][:, :PAIR]], axis=1))
              for i in range(n)]
        for i, (c, p) in enumerate(chains):
            qe_s[rows[i], cols[i]] = (ld(q_s, i) * ld(eg_s, i) - qo[i][:, :PAIR]).astype(BF16)
            od_s[rows[i], cols[i]] = qo[i][:, PAIR:]
            m_s[c, p] = jnp.where(bdmask, mn[i][:, :PAIR], 0.0).astype(BF16)
            n_s[c, p] = jnp.where(bdmask, mn[i][:, PAIR:], 0.0)
        return carry

    lax.fori_loop(0, n_chunks // DN_UNROLL, local_body, 0)

    for c in range(n_chunks):
        rows = slice(c * DN_CHUNK, (c + 1) * DN_CHUNK)
        for p in range(N_PAIRS):
            cols = slice(p * PAIR, (p + 1) * PAIR)
            st = state_s[p]
            r = _dot(jnp.concatenate([qe_s[rows, cols], m_s[c, p]], axis=0), st.astype(BF16))
            od_s[rows, cols] += r[:DN_CHUNK, :]
            state_s[p] = gl_s[c * DN_CHUNK:c * DN_CHUNK + 1, cols] * st - r[DN_CHUNK:, :] + n_s[c, p]

    od = od_s[...]
    ms = _dot((od * od).astype(BF16), seg) * (1.0 / DN_D)
    zz = z_ref[...]
    o_ref[...] = (od * lax.rsqrt(ms + RMS_EPS) * par_ref[2:3, :] * (zz * jax.nn.sigmoid(zz))).astype(BF16)


def _deltanet(qkv, ab, z, convw, par, consts, bsz, seq):
    t = qkv.shape[0]
    ts = min(TS_DN, seq)
    nt = seq // ts
    hb = ts // HALO
    ea, eb, seg, tri, ones = consts
    row = lambda b, i: (b * nt + i, 0)
    const = lambda b, i: (0, 0)
    halo_map = lambda b, i: (jnp.maximum((b * nt + i) * hb - 1, 0), 0)
    cspec = lambda a: pl.BlockSpec(a.shape, const, pipeline_mode=pl.Buffered(1))
    big = lambda: pltpu.VMEM((ts, DN_W), F32)
    return pl.pallas_call(
        _dn_kernel,
        grid=(bsz, nt),
        in_specs=[pl.BlockSpec((ts, 3 * DN_W), row), pl.BlockSpec((HALO, 3 * DN_W), halo_map),
                  pl.BlockSpec((ts, LANES), row), pl.BlockSpec((ts, DN_W), row),
                  cspec(convw), cspec(par), cspec(ea), cspec(eb), cspec(seg), cspec(tri), cspec(ones)],
        out_specs=pl.BlockSpec((ts, DN_W), row),
        out_shape=jax.ShapeDtypeStruct((t, DN_W), BF16),
        scratch_shapes=[pltpu.VMEM((ts + HALO, 3 * DN_W), F32)] + [big() for _ in range(9)]
                       + [pltpu.VMEM((ts, DN_W), BF16),
                          pltpu.VMEM((ts // DN_CHUNK, N_PAIRS, PAIR, PAIR), BF16),
                          pltpu.VMEM((ts // DN_CHUNK, N_PAIRS, PAIR, PAIR), F32),
                          pltpu.VMEM((N_PAIRS, PAIR, PAIR), F32)],
        compiler_params=_cparams(("parallel", "arbitrary")),
        name="deltanet",
    )(qkv, qkv, ab, z, convw, par, ea, eb, seg, tri, ones)


def _df_kernel(sc_ref, q_ref, k_ref, vt_ref, bt_ref, w_ref, o_ref, sa_ref, sb_ref, acc_ref):
    qi = pl.program_id(2)
    tq = q_ref.shape[0]
    lane = lax.broadcasted_iota(jnp.int32, (tq, 2 * DF_D), 1)
    q = q_ref[...]
    zq = jnp.zeros_like(q)
    qs = (jnp.where(lane < DF_D, q, zq), jnp.where(lane < DF_D, zq, q))
    acc_ref[...] = jnp.zeros_like(acc_ref)

    def scores(j, s_ref):
        kk = k_ref[pl.ds(pl.multiple_of(j * tq, tq), tq), :]
        for mp in range(2):
            s_ref[mp] = _dot_nt(kk, qs[mp])

    def absorb(j, s_ref, carry):
        vt = vt_ref[:, pl.ds(pl.multiple_of(j * tq, tq), tq)]
        out = []
        for mp in range(2):
            m_old, l_old = carry[mp]
            st = s_ref[mp]
            m_new = jnp.maximum(m_old, jnp.max(st, axis=0, keepdims=True))
            alpha = jnp.exp2(m_old - m_new)
            pr = jnp.exp2(st - m_new)
            l_new = alpha * l_old + jnp.sum(pr, axis=0, keepdims=True)
            acc_ref[mp] = alpha * acc_ref[mp] + _dot(vt, pr.astype(BF16))
            out.append((m_new, l_new))
        return tuple(out)

    def absorb_prev(j, s_ref, carry):
        c0 = tq - LANES
        for mp in range(2):
            s_ref[mp, c0:, 0:LANES] += bt_ref[1, c0:, 0:LANES]
        return absorb(j, s_ref, carry)

    bw = tq // DF_DIAG_BANDS
    d0 = pl.multiple_of(qi * tq, tq)
    band_keys = [pl.ds(pl.multiple_of(d0 + r * bw, bw), bw) for r in range(DF_DIAG_BANDS)]

    def scores_diag(s_ref):
        for r in range(DF_DIAG_BANDS):
            c0 = r * bw
            kk = k_ref[band_keys[r], :]
            for mp in range(2):
                s_ref[mp, c0:c0 + bw, c0:] = _dot_nt(kk, qs[mp][c0:, :])

    def absorb_diag(s_ref, carry):
        out = []
        for mp in range(2):
            m_old, l_old = carry[mp]
            bands = []
            m_new = m_old
            for r in range(DF_DIAG_BANDS):
                c0 = r * bw
                band = s_ref[mp, c0:c0 + bw, c0:] + bt_ref[0, c0:c0 + bw, c0:]
                bands.append(band)
                m_hi = jnp.maximum(m_new[:, c0:], jnp.max(band, axis=0, keepdims=True))
                m_new = m_hi if c0 == 0 else jnp.concatenate([m_new[:, :c0], m_hi], axis=1)
            alpha = jnp.exp2(m_old - m_new)
            acc_ref[mp] = alpha * acc_ref[mp]
            l_new = alpha * l_old
            for r in range(DF_DIAG_BANDS):
                c0 = r * bw
                pr = jnp.exp2(bands[r] - m_new[:, c0:])
                l_band = jnp.sum(pr, axis=0, keepdims=True)
                l_new = l_new + (l_band if c0 == 0 else
                                 jnp.concatenate([jnp.zeros((1, c0), F32), l_band], axis=1))
                acc_ref[mp, :, c0:] += _dot(vt_ref[:, band_keys[r]], pr.astype(BF16))
            out.append((m_new, l_new))
        return tuple(out)

    init1 = (jnp.full((1, tq), -jnp.inf, F32), jnp.zeros((1, tq), F32))
    carry = (init1, init1)
    n_far = jnp.maximum(qi - 1, 0)
    odd = n_far % 2

    @pl.when(qi == 0)
    def _():
        scores_diag(sb_ref)

    @pl.when(jnp.logical_and(qi > 0, odd == 0))
    def _():
        scores(0, sa_ref)

    def odd_step(_, c):
        scores(0, sb_ref)
        scores(1, sa_ref)
        return absorb(0, sb_ref, c)

    carry = lax.fori_loop(0, odd, odd_step, carry)

    def pair_step(jp, c):
        j = odd + 2 * jp
        scores(j + 1, sb_ref)
        c = absorb(j, sa_ref, c)
        scores(j + 2, sa_ref)
        return absorb(j + 1, sb_ref, c)

    carry = lax.fori_loop(0, n_far // 2, pair_step, carry)

    def tail_step(j, c):
        scores_diag(sb_ref)
        c = absorb_prev(j, sa_ref, c)
        return absorb_diag(sb_ref, c)

    carry = lax.fori_loop(n_far, qi, tail_step, carry)
    carry = lax.fori_loop(0, jnp.where(qi == 0, 1, 0), lambda _, c: absorb_diag(sb_ref, c), carry)

    (_, l0), (_, l1) = carry
    ot = acc_ref[0] / l0 - sc_ref[0] * (acc_ref[1] / l1)
    ms = jnp.mean(ot * ot, axis=0, keepdims=True)
    ot = ot * lax.rsqrt(ms + RMS_EPS) * w_ref[...] * sc_ref[1]
    o_ref[...] = ot.T.astype(BF16)


def _attn_bias_tiles(rel_bias, seq, tq):
    rb = rel_bias.astype(F32)
    far = rb[_t5_causal_bucket(jnp.asarray(seq - 1, jnp.int32))]
    kj = jnp.arange(tq, dtype=jnp.int32)[:, None]
    qi = jnp.arange(tq, dtype=jnp.int32)[None, :]
    dist = jnp.stack([qi - kj, qi + tq - kj])
    onehot = (_t5_causal_bucket(jnp.maximum(dist, 0))[..., None]
              == jnp.arange(REL_BUCKETS, dtype=jnp.int32)).astype(F32)
    vals = (jnp.einsum('ntqb,bh->ntqh', onehot, rb, precision=lax.Precision.HIGHEST) - far) * LOG2E
    vals = jnp.where((dist >= 0)[..., None], vals, -jnp.inf)
    return jnp.transpose(vals, (3, 0, 1, 2))


def _diff_attention(qd, kd, vdt, btiles, subln_w, scalars, bsz, seq):
    t = qd.shape[0]
    tq = min(TQ_DF, seq)
    nq = seq // tq
    sbuf = pltpu.VMEM((2, tq, tq), F32)
    return pl.pallas_call(
        _df_kernel,
        grid_spec=pltpu.PrefetchScalarGridSpec(
            num_scalar_prefetch=1,
            grid=(bsz, DF_HEADS, nq),
            in_specs=[pl.BlockSpec((tq, 2 * DF_D), lambda b, h, i, sc: (b * nq + i, h)),
                      pl.BlockSpec((seq, 2 * DF_D), lambda b, h, i, sc: (b, h)),
                      pl.BlockSpec((None, 2 * DF_D, seq), lambda b, h, i, sc: (b, h, 0)),
                      pl.BlockSpec((None, 2, tq, tq), lambda b, h, i, sc: (h, 0, 0, 0)),
                      pl.BlockSpec((2 * DF_D, 1), lambda b, h, i, sc: (0, 0))],
            out_specs=pl.BlockSpec((tq, 2 * DF_D), lambda b, h, i, sc: (b * nq + i, h)),
            scratch_shapes=[sbuf, sbuf, pltpu.VMEM((2, 2 * DF_D, tq), F32)],
        ),
        out_shape=jax.ShapeDtypeStruct((t, DF_W), BF16),
        compiler_params=_cparams(("parallel", "parallel", "arbitrary")),
        name="diffattn",
    )(scalars, qd, kd, vdt, btiles, subln_w)


def _outproj_kernel(alpha, x_ref, a_ref, b_ref, w_ref, g_ref, bb_ref, o_ref):
    mix = _dot(a_ref[...], w_ref[0:DN_W, :]) + _dot(b_ref[...], w_ref[DN_W:, :])
    o_ref[...] = _layer_norm(alpha * x_ref[...] + mix, g_ref[...], bb_ref[...])


def _outproj_ln(x2, o_dn, o_df, w_out, g, b, alpha):
    t, d = x2.shape
    tm = min(TM_PROJ, t)
    row = lambda i: (i, 0)
    const = lambda i: (0, 0)
    return pl.pallas_call(
        functools.partial(_outproj_kernel, alpha),
        grid=(t // tm,),
        in_specs=[pl.BlockSpec((tm, d), row), pl.BlockSpec((tm, DN_W), row), pl.BlockSpec((tm, DF_W), row),
                  pl.BlockSpec(w_out.shape, const, pipeline_mode=pl.Buffered(1)),
                  pl.BlockSpec((1, d), const), pl.BlockSpec((1, d), const)],
        out_specs=pl.BlockSpec((tm, d), row),
        out_shape=jax.ShapeDtypeStruct((t, d), F32),
        compiler_params=_cparams(("parallel",)),
        name="outproj_ln",
    )(x2, o_dn, o_df, w_out, g, b)


def _swiglu_acc(xparts, wg_ref, wu_ref, wd_ref, acc_ref):
    d_ff = wg_ref.shape[-1]
    for c0 in range(0, d_ff, FF_CHUNK):
        cs = slice(c0, c0 + FF_CHUNK)
        hg = sum(_dot(xp, wg_ref[k0:k0 + xp.shape[1], cs]) for xp, k0 in xparts)
        hu = sum(_dot(xp, wu_ref[k0:k0 + xp.shape[1], cs]) for xp, k0 in xparts)
        hh = (hg * jax.nn.sigmoid(hg) * hu).astype(BF16)
        contrib = _dot(hh, wd_ref[cs, :])
        if c0 == 0:
            acc_ref[...] = contrib
        else:
            acc_ref[...] += contrib


def _ffn_kernel(alpha, x_ref, wg_ref, wu_ref, wd_ref, g_ref, b_ref, o_ref, acc_ref):
    x = x_ref[...]
    _swiglu_acc([(x.astype(BF16), 0)], wg_ref, wu_ref, wd_ref, acc_ref)
    o_ref[...] = _layer_norm(alpha * x + acc_ref[...], g_ref[...], b_ref[...])


def _ffn_ln(x2, wg, wu, wd, g, b, alpha):
    t, d = x2.shape
    tm = min(TM_FFN, t)
    row = lambda i: (i, 0)
    const = lambda i: (0, 0)
    wspec = lambda a: pl.BlockSpec(a.shape, const, pipeline_mode=pl.Buffered(1))
    return pl.pallas_call(
        functools.partial(_ffn_kernel, alpha),
        grid=(t // tm,),
        in_specs=[pl.BlockSpec((tm, d), row), wspec(wg), wspec(wu), wspec(wd),
                  pl.BlockSpec((1, d), const), pl.BlockSpec((1, d), const)],
        out_specs=pl.BlockSpec((tm, d), row),
        out_shape=jax.ShapeDtypeStruct((t, d), F32),
        scratch_shapes=[pltpu.VMEM((tm, d), F32)],
        compiler_params=_cparams(("parallel",)),
        name="ffn_ln",
    )(x2, wg, wu, wd, g, b)


def _pack_halves(x):
    h = x.shape[1] // 2
    hi = lax.bitcast_convert_type(x[:, :h].astype(BF16).astype(F32), jnp.uint32)
    lo = lax.bitcast_convert_type(x[:, h:].astype(BF16).astype(F32), jnp.uint32)
    return hi | (lo >> 16)


def _unpack_halves(p):
    a = lax.bitcast_convert_type(p & jnp.uint32(0xFFFF0000), F32)
    b = lax.bitcast_convert_type(p << 16, F32)
    return a, b


def _router_kernel(x_ref, wr_ref, o_ref, xp_ref):
    x = x_ref[...]
    xp_ref[...] = _pack_halves(x)
    logits = _mm3(x, wr_ref[...])
    lane = lax.broadcasted_iota(jnp.int32, logits.shape, 1)
    lg = jnp.where(lane < N_EXPERTS, logits, -jnp.inf)
    m1 = jnp.max(lg, axis=-1, keepdims=True)
    i1 = jnp.min(jnp.where(lg == m1, lane, LANES), axis=-1, keepdims=True)
    lg2 = jnp.where(lane == i1, -jnp.inf, lg)
    m2 = jnp.max(lg2, axis=-1, keepdims=True)
    i2 = jnp.min(jnp.where(lg2 == m2, lane, LANES), axis=-1, keepdims=True)
    e = jnp.exp(m2 - m1)
    g1 = 1.0 / (1.0 + e)
    g2 = e / (1.0 + e)
    out = jnp.where(lane == 0, i1.astype(F32), 0.0)
    out = jnp.where(lane == 1, i2.astype(F32), out)
    out = jnp.where(lane == 2, g1, out)
    o_ref[...] = jnp.where(lane == 3, g2, out)


def _router(x2, wr):
    t, d = x2.shape
    tm = min(TM_PROJ, t)
    row = lambda i: (i, 0)
    return pl.pallas_call(
        _router_kernel,
        grid=(t // tm,),
        in_specs=[pl.BlockSpec((tm, d), row), pl.BlockSpec((d, LANES), lambda i: (0, 0))],
        out_specs=[pl.BlockSpec((tm, LANES), row), pl.BlockSpec((tm, d // 2), row)],
        out_shape=[jax.ShapeDtypeStruct((t, LANES), F32), jax.ShapeDtypeStruct((t, d // 2), jnp.uint32)],
        compiler_params=_cparams(("parallel",)),
        name="router",
    )(x2, wr)


def _sc_mesh():
    return plsc.VectorSubcoreMesh(core_axis_name="core", subcore_axis_name="subcore")


def _sc_index_rows(idx):
    win = idx.reshape(-1, SC_WIN)
    return jnp.concatenate([win, jnp.zeros((win.shape[0], SC_IDX_TILE - SC_WIN), idx.dtype)], axis=1)


def _sc_dispatch(xp, slot0, slot1, n_slot):
    t, c = xp.shape
    half = t // SC_WIN // 2
    idx_spec = pl.BlockSpec((1, SC_IDX_TILE), lambda cc, i: (cc * half + i, 0))

    @pl.kernel(out_type=jax.ShapeDtypeStruct((n_slot, c), xp.dtype), mesh=_sc_mesh(), scratch_types=[],
               name="moe_dispatch")
    def run(x_hbm, i0_hbm, i1_hbm, o_hbm):
        def body(x_vmem, i0_vmem, i1_vmem):
            pltpu.sync_copy(x_vmem, o_hbm.at[i0_vmem.at[0, pl.ds(0, SC_WIN)]])
            pltpu.sync_copy(x_vmem, o_hbm.at[i1_vmem.at[0, pl.ds(0, SC_WIN)]])

        pltpu.emit_pipeline(
            body,
            grid=(2, half),
            in_specs=[pl.BlockSpec((SC_WIN, c), lambda cc, i: (cc * half + i, 0)), idx_spec, idx_spec],
            out_specs=[],
            core_axis_name=("core", "subcore"),
            dimension_semantics=(pltpu.PARALLEL, pltpu.PARALLEL),
        )(x_hbm, i0_hbm, i1_hbm)

    return run(xp, _sc_index_rows(slot0), _sc_index_rows(slot1))


def _sc_gather(yp, idx):
    t = idx.shape[0]
    c = yp.shape[1]
    half = t // SC_WIN // 2

    @pl.kernel(out_type=jax.ShapeDtypeStruct((t, c), yp.dtype), mesh=_sc_mesh(), scratch_types=[],
               name="moe_gather")
    def run(y_hbm, i_hbm, o_hbm):
        def body(i_vmem, o_vmem):
            pltpu.sync_copy(y_hbm.at[i_vmem.at[0, pl.ds(0, SC_WIN)]], o_vmem)

        pltpu.emit_pipeline(
            body,
            grid=(2, half),
            in_specs=[pl.BlockSpec((1, SC_IDX_TILE), lambda cc, i: (cc * half + i, 0))],
            out_specs=[pl.BlockSpec((SC_WIN, c), lambda cc, i: (cc * half + i, 0))],
            core_axis_name=("core", "subcore"),
            dimension_semantics=(pltpu.PARALLEL, pltpu.PARALLEL),
        )(i_hbm, o_hbm)

    return run(yp, _sc_index_rows(idx))


def _expert_kernel(be_ref, nv_ref, x_ref, wg_ref, wu_ref, wd_ref, o_ref, acc_ref):
    i = pl.program_id(0)
    n_valid = nv_ref[i]

    @pl.when(n_valid > 0)
    def _():
        row = lax.broadcasted_iota(jnp.int32, x_ref.shape, 0)
        xa, xb = _unpack_halves(jnp.where(row < n_valid, x_ref[...], jnp.uint32(0)))
        h = xa.shape[1]
        _swiglu_acc([(xa.astype(BF16), 0), (xb.astype(BF16), h)], wg_ref, wu_ref, wd_ref, acc_ref)
        o_ref[...] = _pack_halves(acc_ref[...])

    @pl.when(n_valid <= 0)
    def _():
        o_ref[...] = jnp.zeros_like(o_ref)


def _experts(xbp, blk_e, n_valid, wg, wu, wd):
    n_slot, dh = xbp.shape
    d = 2 * dh
    n_blk = n_slot // MOE_BLK
    dff = wg.shape[-1]
    row = lambda i, be, nv: (i, 0)
    wmap = lambda i, be, nv: (be[i], 0, 0)
    return pl.pallas_call(
        _expert_kernel,
        grid_spec=pltpu.PrefetchScalarGridSpec(
            num_scalar_prefetch=2,
            grid=(n_blk,),
            in_specs=[pl.BlockSpec((MOE_BLK, dh), row),
                      pl.BlockSpec((None, d, dff), wmap, pipeline_mode=pl.Buffered(1)),
                      pl.BlockSpec((None, d, dff), wmap, pipeline_mode=pl.Buffered(1)),
                      pl.BlockSpec((None, dff, d), wmap, pipeline_mode=pl.Buffered(1))],
            out_specs=pl.BlockSpec((MOE_BLK, dh), row),
            scratch_shapes=[pltpu.VMEM((MOE_BLK, d), F32)],
        ),
        out_shape=jax.ShapeDtypeStruct((n_slot, dh), jnp.uint32),
        compiler_params=_cparams(("arbitrary",)),
        name="experts",
    )(blk_e, n_valid, xbp, wg, wu, wd)


def _combine_kernel(alpha, x_ref, y0_ref, y1_ref, r_ref, g_ref, b_ref, o_ref):
    r = r_ref[...]
    a0, b0 = _unpack_halves(y0_ref[...])
    a1, b1 = _unpack_halves(y1_ref[...])
    g0 = r[:, 2:3]
    g1 = r[:, 3:4]
    f = jnp.concatenate([g0 * a0 + g1 * a1, g0 * b0 + g1 * b1], axis=1)
    o_ref[...] = _layer_norm(alpha * x_ref[...] + f, g_ref[...], b_ref[...])


def _combine_ln(x2, y0, y1, route, g, b, alpha):
    t, d = x2.shape
    tm = min(TM_PROJ, t)
    row = lambda i: (i, 0)
    const = lambda i: (0, 0)
    return pl.pallas_call(
        functools.partial(_combine_kernel, alpha),
        grid=(t // tm,),
        in_specs=[pl.BlockSpec((tm, d), row), pl.BlockSpec((tm, d // 2), row), pl.BlockSpec((tm, d // 2), row),
                  pl.BlockSpec((tm, LANES), row), pl.BlockSpec((1, d), const), pl.BlockSpec((1, d), const)],
        out_specs=pl.BlockSpec((tm, d), row),
        out_shape=jax.ShapeDtypeStruct((t, d), F32),
        compiler_params=_cparams(("parallel",)),
        name="combine_ln",
    )(x2, y0, y1, route, g, b)


def _moe(x2, wr, wg, wu, wd, g, b, alpha):
    t, d = x2.shape
    route, xp = _router(x2, wr)
    top_idx = route[:, 0:2].astype(jnp.int32)
    n_asg = t * TOP_K
    flat_e = top_idx.reshape(n_asg)
    onehot = (flat_e[:, None] == jnp.arange(N_EXPERTS, dtype=jnp.int32)[None, :]).astype(jnp.int32)
    csum = jnp.cumsum(onehot, axis=0)
    rank = jnp.sum((csum - onehot) * onehot, axis=1)
    counts = csum[-1]
    padded = (counts + MOE_BLK - 1) // MOE_BLK * MOE_BLK
    pad_end = jnp.cumsum(padded)
    pad_start = pad_end - padded
    slot = (pad_start[flat_e] + rank).reshape(t, TOP_K)
    n_slot = -(-n_asg // MOE_BLK) * MOE_BLK + N_EXPERTS * MOE_BLK
    n_blk = n_slot // MOE_BLK
    blk_start = jnp.arange(n_blk, dtype=jnp.int32) * MOE_BLK
    blk_e = jnp.minimum(jnp.searchsorted(pad_end, blk_start, side='right'), N_EXPERTS - 1).astype(jnp.int32)
    n_valid = jnp.clip(pad_start[blk_e] + counts[blk_e] - blk_start, 0, MOE_BLK).astype(jnp.int32)
    slot0 = slot[:, 0]
    slot1 = slot[:, 1]
    xbp = _sc_dispatch(xp, slot0, slot1, n_slot)
    ybp = _experts(xbp, blk_e, n_valid, wg, wu, wd)
    return _combine_ln(x2, _sc_gather(ybp, slot0), _sc_gather(ybp, slot1), route, g, b, alpha)


def _t5_causal_bucket(dist):
    max_exact = REL_BUCKETS // 2
    d = jnp.maximum(dist, 1).astype(F32)
    large = max_exact + (jnp.log(d / max_exact) / math.log(REL_MAX_DIST / max_exact)
                         * (REL_BUCKETS - max_exact)).astype(jnp.int32)
    large = jnp.minimum(large, REL_BUCKETS - 1)
    return jnp.where(dist < max_exact, dist, large)


def _dn_constants(ts):
    lanes = np.arange(DN_W)
    ea = np.zeros((LANES, DN_W), np.float32)
    eb = np.zeros((LANES, DN_W), np.float32)
    ea[lanes // DN_D, lanes] = 1.0
    eb[DN_HEADS + lanes // DN_D, lanes] = 1.0
    seg = (lanes[:, None] // DN_D == lanes[None, :] // DN_D).astype(np.float32)
    r = np.arange(ts)
    same = r[:, None] // DN_CHUNK == r[None, :] // DN_CHUNK
    tri = (same & (r[:, None] >= r[None, :])).astype(np.float32)
    ones = same.astype(np.float32)
    return tuple(jnp.asarray(a, BF16) for a in (ea, eb, seg, tri, ones))


def kernel(x, w_in, w_out, conv_w, dn_a_log, dn_dt_bias, dn_norm_w, df_lambda, df_subln_w, rel_bias,
           ln1_g, ln1_b, ln2_g, ln2_b, ffn_w_gate, ffn_w_up, ffn_w_down, moe_router, moe_w_gate,
           moe_w_up, moe_w_down):
    bsz, seq, d = x.shape
    depth = w_in.shape[0]
    alpha = (2 * depth) ** 0.25
    t = bsz * seq
    tq = min(TQ_DF, seq)
    assert REL_MAX_DIST <= LANES and tq % (DF_DIAG_BANDS * LANES) == 0
    assert seq % tq == 0 and seq % min(TS_DN, seq) == 0
    btiles = _attn_bias_tiles(rel_bias, seq, tq)

    dn_consts = _dn_constants(min(TS_DN, seq))
    c_dn = 3 * DN_W
    x2 = x.reshape(t, d)
    for layer in range(depth):
        lambda_init = 0.8 - 0.6 * math.exp(-0.3 * layer)
        wl = w_in[layer]
        n1 = c_dn + DN_W + 2 * DN_HEADS
        w1 = jnp.concatenate([wl[:, :n1], jnp.zeros((d, LANES - 2 * DN_HEADS), F32)], axis=1).astype(BF16)
        w2 = wl[:, n1:n1 + 2 * DF_W].astype(BF16)
        wvt = wl[:, n1 + 2 * DF_W:].T.astype(BF16)
        qkv, z, ab, qd, kd, vdt = _inproj(x2, w1, w2, wvt, bsz, seq)

        convw = jnp.concatenate([conv_w[layer], jnp.zeros((HALO - DN_CONV, c_dn), F32)], axis=0)
        par = jnp.zeros((HALO, DN_W), F32)
        par = par.at[0, 0:DN_HEADS].set(dn_a_log[layer])
        par = par.at[1, 0:DN_HEADS].set(dn_dt_bias[layer])
        par = par.at[2].set(jnp.tile(dn_norm_w[layer], DN_HEADS))
        o_dn = _deltanet(qkv, ab, z, convw, par, dn_consts, bsz, seq)

        lf = df_lambda[layer].astype(F32)
        lam = jnp.exp(jnp.sum(lf[0] * lf[1])) - jnp.exp(jnp.sum(lf[2] * lf[3])) + lambda_init
        scalars = jnp.stack([lam, jnp.asarray(1.0 - lambda_init, F32)]).astype(F32)
        o_df = _diff_attention(qd, kd, vdt, btiles, df_subln_w[layer].reshape(2 * DF_D, 1), scalars, bsz, seq)

        i = layer // 2
        g1 = ln1_g[layer].reshape(1, d)
        b1 = ln1_b[layer].reshape(1, d)
        g2 = ln2_g[layer].reshape(1, d)
        b2 = ln2_b[layer].reshape(1, d)
        x2 = _outproj_ln(x2, o_dn, o_df, w_out[layer].astype(BF16), g1, b1, alpha)
        if layer % 2 == 0:
            x2 = _ffn_ln(x2, ffn_w_gate[i].astype(BF16), ffn_w_up[i].astype(BF16), ffn_w_down[i].astype(BF16),
                         g2, b2, alpha)
        else:
            wr = jnp.concatenate([moe_router[i], jnp.zeros((d, LANES - N_EXPERTS), F32)], axis=1)
            x2 = _moe(x2, wr, moe_w_gate[i].astype(BF16), moe_w_up[i].astype(BF16), moe_w_down[i].astype(BF16),
                      g2, b2, alpha)
    return x2.reshape(bsz, seq, d)
```

```python
import functools
import math

import jax
import jax.numpy as jnp
import numpy as np
from jax import lax
from jax.experimental import pallas as pl
from jax.experimental.pallas import tpu as pltpu
from jax.experimental.pallas import tpu_sc as plsc

F32 = jnp.float32
BF16 = jnp.bfloat16

DN_HEADS = 8
DN_D = 64
DN_CONV = 4
DN_CHUNK = 64
DN_W = DN_HEADS * DN_D
DF_HEADS = 4
DF_D = 64
DF_W = DF_HEADS * 2 * DF_D
REL_BUCKETS = 32
REL_MAX_DIST = 128
N_EXPERTS = 8
TOP_K = 2
MOE_BLK = 512
LN_EPS = 1e-5
RMS_EPS = 1e-6
LOG2E = math.log2(math.e)

LANES = 128
HALO = 8
PAIR = 2 * DN_D
N_PAIRS = DN_HEADS // 2
VMEM_LIMIT = 56 * 1024 * 1024

TM_PROJ = 512
TM_LN = 1024
TS_DN = 256
DN_UNROLL = 4
TQ_DF = 1024
DF_DIAG_BANDS = 4
TM_FFN = 512
FF_CHUNK = 256
SC_WIN = 64
SC_IDX_TILE = 128


def _cparams(sem):
    return pltpu.CompilerParams(dimension_semantics=sem, vmem_limit_bytes=VMEM_LIMIT)


def _dot(a, b):
    return jnp.dot(a, b, preferred_element_type=F32)


def _dot_nt(a, b):
    return lax.dot_general(a, b, (((1,), (1,)), ((), ())), preferred_element_type=F32)


def _dot_tn(a, b):
    return lax.dot_general(a, b, (((0,), (0,)), ((), ())), preferred_element_type=F32)


def _split(x):
    hi = x.astype(BF16)
    lo = (x - hi.astype(F32)).astype(BF16)
    return hi, lo


def _mm_xc(x, c):
    hi, lo = _split(x)
    return _dot(hi, c) + _dot(lo, c)


def _mm_cx(c, x):
    hi, lo = _split(x)
    return _dot(c, hi) + _dot(c, lo)


def _mm3(a, b):
    ah, al = _split(a)
    bh, bl = _split(b)
    return _dot(ah, bh) + _dot(ah, bl) + _dot(al, bh)


def _layer_norm(v, g, b):
    mu = jnp.mean(v, axis=-1, keepdims=True)
    d = v - mu
    var = jnp.mean(d * d, axis=-1, keepdims=True)
    return d * lax.rsqrt(var + LN_EPS) * g + b


def _inproj_kernel(x_ref, w1_ref, w2_ref, wvt_ref, qkv_ref, z_ref, ab_ref, qd_ref, kd_ref, vd_ref):
    xb = x_ref[...].astype(BF16)
    c = 3 * DN_W
    qkv_ref[...] = _dot(xb, w1_ref[:, 0:c])
    z_ref[...] = _dot(xb, w1_ref[:, c:c + DN_W])
    ab_ref[...] = _dot(xb, w1_ref[:, c + DN_W:c + DN_W + LANES])
    qd_ref[...] = (_dot(xb, w2_ref[:, 0:DF_W]) * (DF_D ** -0.5 * LOG2E)).astype(BF16)
    kd_ref[...] = _dot(xb, w2_ref[:, DF_W:2 * DF_W]).astype(BF16)
    vd_ref[...] = _dot_nt(wvt_ref[...], xb).astype(BF16)


def _inproj(x2, w1, w2, wvt, bsz, seq):
    t, d = x2.shape
    tm = min(TM_PROJ, seq)
    nt = seq // tm
    n1, n2 = w1.shape[1], w2.shape[1]
    row = lambda i: (i, 0)
    const = lambda i: (0, 0)
    return pl.pallas_call(
        _inproj_kernel,
        grid=(t // tm,),
        in_specs=[pl.BlockSpec((tm, d), row),
                  pl.BlockSpec((d, n1), const, pipeline_mode=pl.Buffered(1)),
                  pl.BlockSpec((d, n2), const, pipeline_mode=pl.Buffered(1)),
                  pl.BlockSpec((DF_W, d), const, pipeline_mode=pl.Buffered(1))],
        out_specs=[pl.BlockSpec((tm, 3 * DN_W), row), pl.BlockSpec((tm, DN_W), row),
                   pl.BlockSpec((tm, LANES), row), pl.BlockSpec((tm, DF_W), row),
                   pl.BlockSpec((tm, DF_W), row),
                   pl.BlockSpec((None, DF_W, tm), lambda i: (i // nt, 0, i % nt))],
        out_shape=[jax.ShapeDtypeStruct((t, 3 * DN_W), F32), jax.ShapeDtypeStruct((t, DN_W), F32),
                   jax.ShapeDtypeStruct((t, LANES), F32), jax.ShapeDtypeStruct((t, DF_W), BF16),
                   jax.ShapeDtypeStruct((t, DF_W), BF16), jax.ShapeDtypeStruct((bsz, DF_W, seq), BF16)],
        compiler_params=_cparams(("parallel",)),
        name="inproj",
    )(x2, w1, w2, wvt)


def _dn_kernel(qkv_ref, halo_ref, ab_ref, z_ref, convw_ref, par_ref, ea_ref, eb_ref, seg_ref,
               tri_ref, ones_ref, o_ref,
               xe_s, q_s, k_s, kb_s, vb_s, gc_s, eg_s, kdec_s, gl_s, od_s, qe_s, m_s, n_s, state_s):
    i = pl.program_id(1)
    ts = qkv_ref.shape[0]
    n_chunks = ts // DN_CHUNK

    @pl.when(i == 0)
    def _():
        state_s[...] = jnp.zeros_like(state_s)

    halo = halo_ref[...]
    xe_s[0:HALO, :] = jnp.where(i > 0, halo, jnp.zeros_like(halo))
    xe_s[HALO:, :] = qkv_ref[...]
    xe = xe_s[...]
    y = convw_ref[DN_CONV - 1:DN_CONV, :] * xe[HALO:, :]
    for tap in range(DN_CONV - 1):
        y += convw_ref[tap:tap + 1, :] * pltpu.roll(xe, DN_CONV - 1 - tap, axis=0)[HALO:, :]
    y = y * jax.nn.sigmoid(y)
    q = y[:, 0:DN_W]
    k = y[:, DN_W:2 * DN_W]
    v = y[:, 2 * DN_W:3 * DN_W]

    seg = seg_ref[...]
    q = q * lax.rsqrt(_dot((q * q).astype(BF16), seg) + 1e-6) * (DN_D ** -0.5)
    k = k * lax.rsqrt(_dot((k * k).astype(BF16), seg) + 1e-6)

    ab = ab_ref[...]
    xa = ab + par_ref[1:2, 0:LANES]
    softplus = jnp.maximum(xa, 0.0) + jnp.log(1.0 + jnp.exp(-jnp.abs(xa)))
    g = _mm_xc(-jnp.exp(par_ref[0:1, 0:LANES]) * softplus, ea_ref[...])
    beta = _mm_xc(jax.nn.sigmoid(ab), eb_ref[...])
    gc = _mm_cx(tri_ref[...], g)
    gl = _mm_cx(ones_ref[...], g)
    eg = jnp.exp(gc)
    kb = k * beta
    q_s[...] = q
    k_s[...] = k
    kb_s[...] = kb
    vb_s[...] = v * beta
    gc_s[...] = gc
    eg_s[...] = eg
    kdec_s[...] = k * jnp.exp(gl - gc)
    gl_s[...] = jnp.exp(gl)

    lane = lax.broadcasted_iota(jnp.int32, (DN_CHUNK, PAIR), 1)
    rowi = lax.broadcasted_iota(jnp.int32, (DN_CHUNK, PAIR), 0)
    colj = jnp.where(lane >= DN_D, lane - DN_D, lane)
    even = lane < DN_D
    eye2 = rowi == colj
    lower = rowi >= colj
    lane_b = lax.broadcasted_iota(jnp.int32, (PAIR, PAIR), 1)
    row_b = lax.broadcasted_iota(jnp.int32, (PAIR, PAIR), 0)
    bdmask = (lane_b < DN_D) == (row_b < DN_D)

    def bd(xm):
        z0 = jnp.zeros_like(xm)
        return jnp.concatenate([jnp.where(even, xm, z0), jnp.where(even, z0, xm)], axis=0)

    eye_f = jnp.where(eye2, 1.0, 0.0)
    sub_masks = []
    s = 1
    while s < DN_CHUNK:
        same = (rowi // (2 * s)) == (colj // (2 * s))
        sub_masks.append(same & ((rowi // s) % 2 == 1) & ((colj // s) % 2 == 0))
        s *= 2

    def local_body(cc, carry):
        chains = [(cc * DN_UNROLL + dc, p) for dc in range(DN_UNROLL) for p in range(N_PAIRS)]
        n = len(chains)
        rows = [pl.ds(pl.multiple_of(c * DN_CHUNK, DN_CHUNK), DN_CHUNK) for c, _ in chains]
        cols = [slice(p * PAIR, (p + 1) * PAIR) for _, p in chains]
        ld = lambda ref, i: ref[rows[i], cols[i]]

        aq = [_dot_nt(jnp.concatenate([ld(kb_s, i), ld(q_s, i)], axis=0).astype(BF16),
                      bd(ld(k_s, i).astype(BF16))) for i in range(n)]
        a_qk, l_m = [], []
        for i in range(n):
            gcc = ld(gc_s, i)
            gcj = jnp.sum(jnp.where(eye2, gcc, 0.0), axis=0, keepdims=True)
            dec = jnp.where(lower, jnp.exp(jnp.minimum(gcc - gcj, 0.0)), 0.0)
            a_qk.append((aq[i][DN_CHUNK:, :] * dec).astype(BF16))
            l_m.append(jnp.where(eye2, 0.0, aq[i][:DN_CHUNK, :] * dec))

        lb = [m.astype(BF16) for m in l_m]
        t_m = [eye_f - jnp.where(sub_masks[0], m, 0.0) for m in l_m]
        for lvl in range(1, len(sub_masks)):
            dc = [_dot(t_m[i].astype(BF16), bd(jnp.where(sub_masks[lvl], lb[i], jnp.zeros_like(lb[i]))))
                  for i in range(n)]
            t_m = [t_m[i] - _dot(dc[i].astype(BF16), bd(t_m[i].astype(BF16))) for i in range(n)]

        uw = []
        for i in range(n):
            kbg = (ld(kb_s, i) * ld(eg_s, i)).astype(BF16)
            rhs = jnp.concatenate([bd(ld(vb_s, i).astype(BF16)), bd(kbg)], axis=1)
            uw.append(_dot(t_m[i].astype(BF16), rhs).astype(BF16))
        qo = [_dot(a_qk[i], jnp.concatenate([bd(uw[i][:, PAIR:]), bd(uw[i][:, :PAIR])], axis=1))
              for i in range(n)]
        mn = [_dot_tn(ld(kdec_s, i).astype(BF16), jnp.concatenate([uw[i][:, PAIR:], uw[i][:, :PAIR]], axis=1))
              for i in range(n)]
        for i, (c, p) in enumerate(chains):
            qe_s[rows[i], cols[i]] = (ld(q_s, i) * ld(eg_s, i) - qo[i][:, :PAIR]).astype(BF16)
            od_s[rows[i], cols[i]] = qo[i][:, PAIR:]
            m_s[c, p] = jnp.where(bdmask, mn[i][:, :PAIR], 0.0).astype(BF16)
            n_s[c, p] = jnp.where(bdmask, mn[i][:, PAIR:], 0.0)
        return carry

    lax.fori_loop(0, n_chunks // DN_UNROLL, local_body, 0)

    for c in range(n_chunks):
        rows = slice(c * DN_CHUNK, (c + 1) * DN_CHUNK)
        for p in range(N_PAIRS):
            cols = slice(p * PAIR, (p + 1) * PAIR)
            st = state_s[p]
            r = _dot(jnp.concatenate([qe_s[rows, cols], m_s[c, p]], axis=0), st.astype(BF16))
            od_s[rows, cols] += r[:DN_CHUNK, :]
            state_s[p] = gl_s[c * DN_CHUNK:c * DN_CHUNK + 1, cols] * st - r[DN_CHUNK:, :] + n_s[c, p]

    od = od_s[...]
    ms = _dot((od * od).astype(BF16), seg) * (1.0 / DN_D)
    zz = z_ref[...]
    o_ref[...] = (od * lax.rsqrt(ms + RMS_EPS) * par_ref[2:3, :] * (zz * jax.nn.sigmoid(zz))).astype(BF16)


def _deltanet(qkv, ab, z, convw, par, consts, bsz, seq):
    t = qkv.shape[0]
    ts = min(TS_DN, seq)
    nt = seq // ts
    hb = ts // HALO
    ea, eb, seg, tri, ones = consts
    row = lambda b, i: (b * nt + i, 0)
    const = lambda b, i: (0, 0)
    halo_map = lambda b, i: (jnp.maximum((b * nt + i) * hb - 1, 0), 0)
    cspec = lambda a: pl.BlockSpec(a.shape, const, pipeline_mode=pl.Buffered(1))
    big = lambda: pltpu.VMEM((ts, DN_W), F32)
    return pl.pallas_call(
        _dn_kernel,
        grid=(bsz, nt),
        in_specs=[pl.BlockSpec((ts, 3 * DN_W), row), pl.BlockSpec((HALO, 3 * DN_W), halo_map),
                  pl.BlockSpec((ts, LANES), row), pl.BlockSpec((ts, DN_W), row),
                  cspec(convw), cspec(par), cspec(ea), cspec(eb), cspec(seg), cspec(tri), cspec(ones)],
        out_specs=pl.BlockSpec((ts, DN_W), row),
        out_shape=jax.ShapeDtypeStruct((t, DN_W), BF16),
        scratch_shapes=[pltpu.VMEM((ts + HALO, 3 * DN_W), F32)] + [big() for _ in range(9)]
                       + [pltpu.VMEM((ts, DN_W), BF16),
                          pltpu.VMEM((ts // DN_CHUNK, N_PAIRS, PAIR, PAIR), BF16),
                          pltpu.VMEM((ts // DN_CHUNK, N_PAIRS, PAIR, PAIR), F32),
                          pltpu.VMEM((N_PAIRS, PAIR, PAIR), F32)],
        compiler_params=_cparams(("parallel", "arbitrary")),
        name="deltanet",
    )(qkv, qkv, ab, z, convw, par, ea, eb, seg, tri, ones)


def _df_kernel(sc_ref, q_ref, k_ref, vt_ref, bt_ref, w_ref, o_ref, sa_ref, sb_ref, acc_ref):
    qi = pl.program_id(2)
    tq = q_ref.shape[0]
    lane = lax.broadcasted_iota(jnp.int32, (tq, 2 * DF_D), 1)
    q = q_ref[...]
    zq = jnp.zeros_like(q)
    qs = (jnp.where(lane < DF_D, q, zq), jnp.where(lane < DF_D, zq, q))
    acc_ref[...] = jnp.zeros_like(acc_ref)

    def scores(j, s_ref):
        kk = k_ref[pl.ds(pl.multiple_of(j * tq, tq), tq), :]
        for mp in range(2):
            s_ref[mp] = _dot_nt(kk, qs[mp])

    def absorb(j, s_ref, carry):
        vt = vt_ref[:, pl.ds(pl.multiple_of(j * tq, tq), tq)]
        out = []
        for mp in range(2):
            m_old, l_old = carry[mp]
            st = s_ref[mp]
            m_new = jnp.maximum(m_old, jnp.max(st, axis=0, keepdims=True))
            alpha = jnp.exp2(m_old - m_new)
            pr = jnp.exp2(st - m_new)
            l_new = alpha * l_old + jnp.sum(pr, axis=0, keepdims=True)
            acc_ref[mp] = alpha * acc_ref[mp] + _dot(vt, pr.astype(BF16))
            out.append((m_new, l_new))
        return tuple(out)

    def absorb_prev(j, s_ref, carry):
        c0 = tq - LANES
        for mp in range(2):
            s_ref[mp, c0:, 0:LANES] += bt_ref[1, c0:, 0:LANES]
        return absorb(j, s_ref, carry)

    bw = tq // DF_DIAG_BANDS
    d0 = pl.multiple_of(qi * tq, tq)
    band_keys = [pl.ds(pl.multiple_of(d0 + r * bw, bw), bw) for r in range(DF_DIAG_BANDS)]

    def scores_diag(s_ref):
        for r in range(DF_DIAG_BANDS):
            c0 = r * bw
            kk = k_ref[band_keys[r], :]
            for mp in range(2):
                s_ref[mp, c0:c0 + bw, c0:] = _dot_nt(kk, qs[mp][c0:, :])

    def absorb_diag(s_ref, carry):
        out = []
        for mp in range(2):
            m_old, l_old = carry[mp]
            bands = []
            m_new = m_old
            for r in range(DF_DIAG_BANDS):
                c0 = r * bw
                band = s_ref[mp, c0:c0 + bw, c0:] + bt_ref[0, c0:c0 + bw, c0:]
                bands.append(band)
                m_hi = jnp.maximum(m_new[:, c0:], jnp.max(band, axis=0, keepdims=True))
                m_new = m_hi if c0 == 0 else jnp.concatenate([m_new[:, :c0], m_hi], axis=1)
            alpha = jnp.exp2(m_old - m_new)
            acc_ref[mp] = alpha * acc_ref[mp]
            l_new = alpha * l_old
            for r in range(DF_DIAG_BANDS):
                c0 = r * bw
                pr = jnp.exp2(bands[r] - m_new[:, c0:])
                l_band = jnp.sum(pr, axis=0, keepdims=True)
                l_new = l_new + (l_band if c0 == 0 else
                                 jnp.concatenate([jnp.zeros((1, c0), F32), l_band], axis=1))
                acc_ref[mp, :, c0:] += _dot(vt_ref[:, band_keys[r]], pr.astype(BF16))
            out.append((m_new, l_new))
        return tuple(out)

    init1 = (jnp.full((1, tq), -jnp.inf, F32), jnp.zeros((1, tq), F32))
    carry = (init1, init1)
    n_far = jnp.maximum(qi - 1, 0)
    odd = n_far % 2

    @pl.when(qi == 0)
    def _():
        scores_diag(sb_ref)

    @pl.when(jnp.logical_and(qi > 0, odd == 0))
    def _():
        scores(0, sa_ref)

    def odd_step(_, c):
        scores(0, sb_ref)
        scores(1, sa_ref)
        return absorb(0, sb_ref, c)

    carry = lax.fori_loop(0, odd, odd_step, carry)

    def pair_step(jp, c):
        j = odd + 2 * jp
        scores(j + 1, sb_ref)
        c = absorb(j, sa_ref, c)
        scores(j + 2, sa_ref)
        return absorb(j + 1, sb_ref, c)

    carry = lax.fori_loop(0, n_far // 2, pair_step, carry)

    def tail_step(j, c):
        scores_diag(sb_ref)
        c = absorb_prev(j, sa_ref, c)
        return absorb_diag(sb_ref, c)

    carry = lax.fori_loop(n_far, qi, tail_step, carry)
    carry = lax.fori_loop(0, jnp.where(qi == 0, 1, 0), lambda _, c: absorb_diag(sb_ref, c), carry)

    (_, l0), (_, l1) = carry
    ot = acc_ref[0] / l0 - sc_ref[0] * (acc_ref[1] / l1)
    ms = jnp.mean(ot * ot, axis=0, keepdims=True)
    ot = ot * lax.rsqrt(ms + RMS_EPS) * w_ref[...] * sc_ref[1]
    o_ref[...] = ot.T.astype(BF16)


def _attn_bias_tiles(rel_bias, seq, tq):
    rb = rel_bias.astype(F32)
    far = rb[_t5_causal_bucket(jnp.asarray(seq - 1, jnp.int32))]
    kj = jnp.arange(tq, dtype=jnp.int32)[:, None]
    qi = jnp.arange(tq, dtype=jnp.int32)[None, :]
    dist = jnp.stack([qi - kj, qi + tq - kj])
    onehot = (_t5_causal_bucket(jnp.maximum(dist, 0))[..., None]
              == jnp.arange(REL_BUCKETS, dtype=jnp.int32)).astype(F32)
    vals = (jnp.einsum('ntqb,bh->ntqh', onehot, rb, precision=lax.Precision.HIGHEST) - far) * LOG2E
    vals = jnp.where((dist >= 0)[..., None], vals, -jnp.inf)
    return jnp.transpose(vals, (3, 0, 1, 2))


def _diff_attention(qd, kd, vdt, btiles, subln_w, scalars, bsz, seq):
    t = qd.shape[0]
    tq = min(TQ_DF, seq)
    nq = seq // tq
    sbuf = pltpu.VMEM((2, tq, tq), F32)
    return pl.pallas_call(
        _df_kernel,
        grid_spec=pltpu.PrefetchScalarGridSpec(
            num_scalar_prefetch=1,
            grid=(bsz, DF_HEADS, nq),
            in_specs=[pl.BlockSpec((tq, 2 * DF_D), lambda b, h, i, sc: (b * nq + i, h)),
                      pl.BlockSpec((seq, 2 * DF_D), lambda b, h, i, sc: (b, h)),
                      pl.BlockSpec((None, 2 * DF_D, seq), lambda b, h, i, sc: (b, h, 0)),
                      pl.BlockSpec((None, 2, tq, tq), lambda b, h, i, sc: (h, 0, 0, 0)),
                      pl.BlockSpec((2 * DF_D, 1), lambda b, h, i, sc: (0, 0))],
            out_specs=pl.BlockSpec((tq, 2 * DF_D), lambda b, h, i, sc: (b * nq + i, h)),
            scratch_shapes=[sbuf, sbuf, pltpu.VMEM((2, 2 * DF_D, tq), F32)],
        ),
        out_shape=jax.ShapeDtypeStruct((t, DF_W), BF16),
        compiler_params=_cparams(("parallel", "parallel", "arbitrary")),
        name="diffattn",
    )(scalars, qd, kd, vdt, btiles, subln_w)


def _outproj_kernel(alpha, x_ref, a_ref, b_ref, w_ref, g_ref, bb_ref, o_ref):
    mix = _dot(a_ref[...], w_ref[0:DN_W, :]) + _dot(b_ref[...], w_ref[DN_W:, :])
    o_ref[...] = _layer_norm(alpha * x_ref[...] + mix, g_ref[...], bb_ref[...])


def _outproj_ln(x2, o_dn, o_df, w_out, g, b, alpha):
    t, d = x2.shape
    tm = min(TM_LN, t)
    row = lambda i: (i, 0)
    const = lambda i: (0, 0)
    return pl.pallas_call(
        functools.partial(_outproj_kernel, alpha),
        grid=(t // tm,),
        in_specs=[pl.BlockSpec((tm, d), row), pl.BlockSpec((tm, DN_W), row), pl.BlockSpec((tm, DF_W), row),
                  pl.BlockSpec(w_out.shape, const, pipeline_mode=pl.Buffered(1)),
                  pl.BlockSpec((1, d), const), pl.BlockSpec((1, d), const)],
        out_specs=pl.BlockSpec((tm, d), row),
        out_shape=jax.ShapeDtypeStruct((t, d), F32),
        compiler_params=_cparams(("parallel",)),
        name="outproj_ln",
    )(x2, o_dn, o_df, w_out, g, b)


def _swiglu_acc(xparts, wg_ref, wu_ref, wd_ref, acc_ref):
    d_ff = wg_ref.shape[-1]
    for c0 in range(0, d_ff, FF_CHUNK):
        cs = slice(c0, c0 + FF_CHUNK)
        hg = sum(_dot(xp, wg_ref[k0:k0 + xp.shape[1], cs]) for xp, k0 in xparts)
        hu = sum(_dot(xp, wu_ref[k0:k0 + xp.shape[1], cs]) for xp, k0 in xparts)
        hh = (hg * jax.nn.sigmoid(hg) * hu).astype(BF16)
        contrib = _dot(hh, wd_ref[cs, :])
        if c0 == 0:
            acc_ref[...] = contrib
        else:
            acc_ref[...] += contrib


def _ffn_kernel(alpha, x_ref, wg_ref, wu_ref, wd_ref, g_ref, b_ref, o_ref, acc_ref):
    x = x_ref[...]
    _swiglu_acc([(x.astype(BF16), 0)], wg_ref, wu_ref, wd_ref, acc_ref)
    o_ref[...] = _layer_norm(alpha * x + acc_ref[...], g_ref[...], b_ref[...])


def _ffn_ln(x2, wg, wu, wd, g, b, alpha):
    t, d = x2.shape
    tm = min(TM_FFN, t)
    row = lambda i: (i, 0)
    const = lambda i: (0, 0)
    wspec = lambda a: pl.BlockSpec(a.shape, const, pipeline_mode=pl.Buffered(1))
    return pl.pallas_call(
        functools.partial(_ffn_kernel, alpha),
        grid=(t // tm,),
        in_specs=[pl.BlockSpec((tm, d), row), wspec(wg), wspec(wu), wspec(wd),
                  pl.BlockSpec((1, d), const), pl.BlockSpec((1, d), const)],
        out_specs=pl.BlockSpec((tm, d), row),
        out_shape=jax.ShapeDtypeStruct((t, d), F32),
        scratch_shapes=[pltpu.VMEM((tm, d), F32)],
        compiler_params=_cparams(("parallel",)),
        name="ffn_ln",
    )(x2, wg, wu, wd, g, b)


def _pack_halves(x):
    h = x.shape[1] // 2
    hi = lax.bitcast_convert_type(x[:, :h].astype(BF16).astype(F32), jnp.uint32)
    lo = lax.bitcast_convert_type(x[:, h:].astype(BF16).astype(F32), jnp.uint32)
    return hi | (lo >> 16)


def _unpack_halves(p):
    a = lax.bitcast_convert_type(p & jnp.uint32(0xFFFF0000), F32)
    b = lax.bitcast_convert_type(p << 16, F32)
    return a, b


def _router_kernel(x_ref, wr_ref, o_ref, xp_ref):
    x = x_ref[...]
    xp_ref[...] = _pack_halves(x)
    logits = _mm3(x, wr_ref[...])
    lane = lax.broadcasted_iota(jnp.int32, logits.shape, 1)
    lg = jnp.where(lane < N_EXPERTS, logits, -jnp.inf)
    m1 = jnp.max(lg, axis=-1, keepdims=True)
    i1 = jnp.min(jnp.where(lg == m1, lane, LANES), axis=-1, keepdims=True)
    lg2 = jnp.where(lane == i1, -jnp.inf, lg)
    m2 = jnp.max(lg2, axis=-1, keepdims=True)
    i2 = jnp.min(jnp.where(lg2 == m2, lane, LANES), axis=-1, keepdims=True)
    e = jnp.exp(m2 - m1)
    g1 = 1.0 / (1.0 + e)
    g2 = e / (1.0 + e)
    out = jnp.where(lane == 0, i1.astype(F32), 0.0)
    out = jnp.where(lane == 1, i2.astype(F32), out)
    out = jnp.where(lane == 2, g1, out)
    o_ref[...] = jnp.where(lane == 3, g2, out)


def _router(x2, wr):
    t, d = x2.shape
    tm = min(TM_LN, t)
    row = lambda i: (i, 0)
    return pl.pallas_call(
        _router_kernel,
        grid=(t // tm,),
        in_specs=[pl.BlockSpec((tm, d), row), pl.BlockSpec((d, LANES), lambda i: (0, 0))],
        out_specs=[pl.BlockSpec((tm, LANES), row), pl.BlockSpec((tm, d // 2), row)],
        out_shape=[jax.ShapeDtypeStruct((t, LANES), F32), jax.ShapeDtypeStruct((t, d // 2), jnp.uint32)],
        compiler_params=_cparams(("parallel",)),
        name="router",
    )(x2, wr)


def _sc_mesh():
    return plsc.VectorSubcoreMesh(core_axis_name="core", subcore_axis_name="subcore")


def _sc_index_rows(idx):
    win = idx.reshape(-1, SC_WIN)
    return jnp.concatenate([win, jnp.zeros((win.shape[0], SC_IDX_TILE - SC_WIN), idx.dtype)], axis=1)


def _sc_dispatch(xp, slot0, slot1, n_slot):
    t, c = xp.shape
    half = t // SC_WIN // 2
    idx_spec = pl.BlockSpec((1, SC_IDX_TILE), lambda cc, i: (cc * half + i, 0))

    @pl.kernel(out_type=jax.ShapeDtypeStruct((n_slot, c), xp.dtype), mesh=_sc_mesh(), scratch_types=[],
               name="moe_dispatch")
    def run(x_hbm, i0_hbm, i1_hbm, o_hbm):
        def body(x_vmem, i0_vmem, i1_vmem):
            pltpu.sync_copy(x_vmem, o_hbm.at[i0_vmem.at[0, pl.ds(0, SC_WIN)]])
            pltpu.sync_copy(x_vmem, o_hbm.at[i1_vmem.at[0, pl.ds(0, SC_WIN)]])

        pltpu.emit_pipeline(
            body,
            grid=(2, half),
            in_specs=[pl.BlockSpec((SC_WIN, c), lambda cc, i: (cc * half + i, 0)), idx_spec, idx_spec],
            out_specs=[],
            core_axis_name=("core", "subcore"),
            dimension_semantics=(pltpu.PARALLEL, pltpu.PARALLEL),
        )(x_hbm, i0_hbm, i1_hbm)

    return run(xp, _sc_index_rows(slot0), _sc_index_rows(slot1))


def _sc_gather(yp, idx):
    t = idx.shape[0]
    c = yp.shape[1]
    half = t // SC_WIN // 2

    @pl.kernel(out_type=jax.ShapeDtypeStruct((t, c), yp.dtype), mesh=_sc_mesh(), scratch_types=[],
               name="moe_gather")
    def run(y_hbm, i_hbm, o_hbm):
        def body(i_vmem, o_vmem):
            pltpu.sync_copy(y_hbm.at[i_vmem.at[0, pl.ds(0, SC_WIN)]], o_vmem)

        pltpu.emit_pipeline(
            body,
            grid=(2, half),
            in_specs=[pl.BlockSpec((1, SC_IDX_TILE), lambda cc, i: (cc * half + i, 0))],
            out_specs=[pl.BlockSpec((SC_WIN, c), lambda cc, i: (cc * half + i, 0))],
            core_axis_name=("core", "subcore"),
            dimension_semantics=(pltpu.PARALLEL, pltpu.PARALLEL),
        )(i_hbm, o_hbm)

    return run(yp, _sc_index_rows(idx))


def _expert_kernel(be_ref, nv_ref, x_ref, wg_ref, wu_ref, wd_ref, o_ref, acc_ref):
    i = pl.program_id(0)
    n_valid = nv_ref[i]

    @pl.when(n_valid > 0)
    def _():
        row = lax.broadcasted_iota(jnp.int32, x_ref.shape, 0)
        xa, xb = _unpack_halves(jnp.where(row < n_valid, x_ref[...], jnp.uint32(0)))
        h = xa.shape[1]
        _swiglu_acc([(xa.astype(BF16), 0), (xb.astype(BF16), h)], wg_ref, wu_ref, wd_ref, acc_ref)
        o_ref[...] = _pack_halves(acc_ref[...])

    @pl.when(n_valid <= 0)
    def _():
        o_ref[...] = jnp.zeros_like(o_ref)


def _experts(xbp, blk_e, n_valid, wg, wu, wd):
    n_slot, dh = xbp.shape
    d = 2 * dh
    n_blk = n_slot // MOE_BLK
    dff = wg.shape[-1]
    row = lambda i, be, nv: (i, 0)
    wmap = lambda i, be, nv: (be[i], 0, 0)
    return pl.pallas_call(
        _expert_kernel,
        grid_spec=pltpu.PrefetchScalarGridSpec(
            num_scalar_prefetch=2,
            grid=(n_blk,),
            in_specs=[pl.BlockSpec((MOE_BLK, dh), row),
                      pl.BlockSpec((None, d, dff), wmap, pipeline_mode=pl.Buffered(1)),
                      pl.BlockSpec((None, d, dff), wmap, pipeline_mode=pl.Buffered(1)),
                      pl.BlockSpec((None, dff, d), wmap, pipeline_mode=pl.Buffered(1))],
            out_specs=pl.BlockSpec((MOE_BLK, dh), row),
            scratch_shapes=[pltpu.VMEM((MOE_BLK, d), F32)],
        ),
        out_shape=jax.ShapeDtypeStruct((n_slot, dh), jnp.uint32),
        compiler_params=_cparams(("arbitrary",)),
        name="experts",
    )(blk_e, n_valid, xbp, wg, wu, wd)


def _combine_kernel(alpha, x_ref, y0_ref, y1_ref, r_ref, g_ref, b_ref, o_ref):
    r = r_ref[...]
    a0, b0 = _unpack_halves(y0_ref[...])
    a1, b1 = _unpack_halves(y1_ref[...])
    g0 = r[:, 2:3]
    g1 = r[:, 3:4]
    f = jnp.concatenate([g0 * a0 + g1 * a1, g0 * b0 + g1 * b1], axis=1)
    o_ref[...] = _layer_norm(alpha * x_ref[...] + f, g_ref[...], b_ref[...])


def _combine_ln(x2, y0, y1, route, g, b, alpha):
    t, d = x2.shape
    tm = min(TM_LN, t)
    row = lambda i: (i, 0)
    const = lambda i: (0, 0)
    return pl.pallas_call(
        functools.partial(_combine_kernel, alpha),
        grid=(t // tm,),
        in_specs=[pl.BlockSpec((tm, d), row), pl.BlockSpec((tm, d // 2), row), pl.BlockSpec((tm, d // 2), row),
                  pl.BlockSpec((tm, LANES), row), pl.BlockSpec((1, d), const), pl.BlockSpec((1, d), const)],
        out_specs=pl.BlockSpec((tm, d), row),
        out_shape=jax.ShapeDtypeStruct((t, d), F32),
        compiler_params=_cparams(("parallel",)),
        name="combine_ln",
    )(x2, y0, y1, route, g, b)


def _moe(x2, wr, wg, wu, wd, g, b, alpha):
    t, d = x2.shape
    route, xp = _router(x2, wr)
    top_idx = route[:, 0:2].astype(jnp.int32)
    n_asg = t * TOP_K
    flat_e = top_idx.reshape(n_asg)
    onehot = (flat_e[:, None] == jnp.arange(N_EXPERTS, dtype=jnp.int32)[None, :]).astype(jnp.int32)
    csum = jnp.cumsum(onehot, axis=0)
    rank = jnp.sum((csum - onehot) * onehot, axis=1)
    counts = csum[-1]
    padded = (counts + MOE_BLK - 1) // MOE_BLK * MOE_BLK
    pad_end = jnp.cumsum(padded)
    pad_start = pad_end - padded
    slot = (pad_start[flat_e] + rank).reshape(t, TOP_K)
    n_slot = -(-n_asg // MOE_BLK) * MOE_BLK + N_EXPERTS * MOE_BLK
    n_blk = n_slot // MOE_BLK
    blk_start = jnp.arange(n_blk, dtype=jnp.int32) * MOE_BLK
    blk_e = jnp.minimum(jnp.searchsorted(pad_end, blk_start, side='right'), N_EXPERTS - 1).astype(jnp.int32)
    n_valid = jnp.clip(pad_start[blk_e] + counts[blk_e] - blk_start, 0, MOE_BLK).astype(jnp.int32)
    slot0 = slot[:, 0]
    slot1 = slot[:, 1]
    xbp = _sc_dispatch(xp, slot0, slot1, n_slot)
    ybp = _experts(xbp, blk_e, n_valid, wg, wu, wd)
    return _combine_ln(x2, _sc_gather(ybp, slot0), _sc_gather(ybp, slot1), route, g, b, alpha)


def _t5_causal_bucket(dist):
    max_exact = REL_BUCKETS // 2
    d = jnp.maximum(dist, 1).astype(F32)
    large = max_exact + (jnp.log(d / max_exact) / math.log(REL_MAX_DIST / max_exact)
                         * (REL_BUCKETS - max_exact)).astype(jnp.int32)
    large = jnp.minimum(large, REL_BUCKETS - 1)
    return jnp.where(dist < max_exact, dist, large)


def _dn_constants(ts):
    lanes = np.arange(DN_W)
    ea = np.zeros((LANES, DN_W), np.float32)
    eb = np.zeros((LANES, DN_W), np.float32)
    ea[lanes // DN_D, lanes] = 1.0
    eb[DN_HEADS + lanes // DN_D, lanes] = 1.0
    seg = (lanes[:, None] // DN_D == lanes[None, :] // DN_D).astype(np.float32)
    r = np.arange(ts)
    same = r[:, None] // DN_CHUNK == r[None, :] // DN_CHUNK
    tri = (same & (r[:, None] >= r[None, :])).astype(np.float32)
    ones = same.astype(np.float32)
    return tuple(jnp.asarray(a, BF16) for a in (ea, eb, seg, tri, ones))


def kernel(x, w_in, w_out, conv_w, dn_a_log, dn_dt_bias, dn_norm_w, df_lambda, df_subln_w, rel_bias,
           ln1_g, ln1_b, ln2_g, ln2_b, ffn_w_gate, ffn_w_up, ffn_w_down, moe_router, moe_w_gate,
           moe_w_up, moe_w_down):
    bsz, seq, d = x.shape
    depth = w_in.shape[0]
    alpha = (2 * depth) ** 0.25
    t = bsz * seq
    tq = min(TQ_DF, seq)
    assert REL_MAX_DIST <= LANES and tq % (DF_DIAG_BANDS * LANES) == 0
    assert seq % tq == 0 and seq % min(TS_DN, seq) == 0
    btiles = _attn_bias_tiles(rel_bias, seq, tq)

    dn_consts = _dn_constants(min(TS_DN, seq))
    c_dn = 3 * DN_W
    x2 = x.reshape(t, d)
    for layer in range(depth):
        lambda_init = 0.8 - 0.6 * math.exp(-0.3 * layer)
        wl = w_in[layer]
        n1 = c_dn + DN_W + 2 * DN_HEADS
        w1 = jnp.concatenate([wl[:, :n1], jnp.zeros((d, LANES - 2 * DN_HEADS), F32)], axis=1).astype(BF16)
        w2 = wl[:, n1:n1 + 2 * DF_W].astype(BF16)
        wvt = wl[:, n1 + 2 * DF_W:].T.astype(BF16)
        qkv, z, ab, qd, kd, vdt = _inproj(x2, w1, w2, wvt, bsz, seq)

        convw = jnp.concatenate([conv_w[layer], jnp.zeros((HALO - DN_CONV, c_dn), F32)], axis=0)
        par = jnp.zeros((HALO, DN_W), F32)
        par = par.at[0, 0:DN_HEADS].set(dn_a_log[layer])
        par = par.at[1, 0:DN_HEADS].set(dn_dt_bias[layer])
        par = par.at[2].set(jnp.tile(dn_norm_w[layer], DN_HEADS))
        o_dn = _deltanet(qkv, ab, z, convw, par, dn_consts, bsz, seq)

        lf = df_lambda[layer].astype(F32)
        lam = jnp.exp(jnp.sum(lf[0] * lf[1])) - jnp.exp(jnp.sum(lf[2] * lf[3])) + lambda_init
        scalars = jnp.stack([lam, jnp.asarray(1.0 - lambda_init, F32)]).astype(F32)
        o_df = _diff_attention(qd, kd, vdt, btiles, df_subln_w[layer].reshape(2 * DF_D, 1), scalars, bsz, seq)

        i = layer // 2
        g1 = ln1_g[layer].reshape(1, d)
        b1 = ln1_b[layer].reshape(1, d)
        g2 = ln2_g[layer].reshape(1, d)
        b2 = ln2_b[layer].reshape(1, d)
        x2 = _outproj_ln(x2, o_dn, o_df, w_out[layer].astype(BF16), g1, b1, alpha)
        if layer % 2 == 0:
            x2 = _ffn_ln(x2, ffn_w_gate[i].astype(BF16), ffn_w_up[i].astype(BF16), ffn_w_down[i].astype(BF16),
                         g2, b2, alpha)
        else:
            wr = jnp.concatenate([moe_router[i], jnp.zeros((d, LANES - N_EXPERTS), F32)], axis=1)
            x2 = _moe(x2, wr, moe_w_gate[i].astype(BF16), moe_w_up[i].astype(BF16), moe_w_down[i].astype(BF16),
                      g2, b2, alpha)
    return x2.reshape(bsz, seq, d)
```

```python
import functools
import math

import jax
import jax.numpy as jnp
import numpy as np
from jax import lax
from jax.experimental import pallas as pl
from jax.experimental.pallas import tpu as pltpu
from jax.experimental.pallas import tpu_sc as plsc

F32 = jnp.float32
BF16 = jnp.bfloat16

DN_HEADS = 8
DN_D = 64
DN_CONV = 4
DN_CHUNK = 64
DN_W = DN_HEADS * DN_D
DF_HEADS = 4
DF_D = 64
DF_W = DF_HEADS * 2 * DF_D
REL_BUCKETS = 32
REL_MAX_DIST = 128
N_EXPERTS = 8
TOP_K = 2
MOE_BLK = 512
LN_EPS = 1e-5
RMS_EPS = 1e-6
LOG2E = math.log2(math.e)

LANES = 128
HALO = 8
PAIR = 2 * DN_D
N_PAIRS = DN_HEADS // 2
VMEM_LIMIT = 56 * 1024 * 1024

TM_PROJ = 512
TM_LN = 1024
TS_DN = 256
DN_UNROLL = 4
TQ_DF = 1024
DF_DIAG_BANDS = 4
TM_FFN = 512
FF_CHUNK = 256
SC_WIN = 64
SC_IDX_TILE = 128


def _cparams(sem):
    return pltpu.CompilerParams(dimension_semantics=sem, vmem_limit_bytes=VMEM_LIMIT)


def _dot(a, b):
    return jnp.dot(a, b, preferred_element_type=F32)


def _dot_nt(a, b):
    return lax.dot_general(a, b, (((1,), (1,)), ((), ())), preferred_element_type=F32)


def _dot_tn(a, b):
    return lax.dot_general(a, b, (((0,), (0,)), ((), ())), preferred_element_type=F32)


def _split(x):
    hi = x.astype(BF16)
    lo = (x - hi.astype(F32)).astype(BF16)
    return hi, lo


def _mm_xc(x, c):
    hi, lo = _split(x)
    return _dot(hi, c) + _dot(lo, c)


def _mm_cx(c, x):
    hi, lo = _split(x)
    return _dot(c, hi) + _dot(c, lo)


def _mm3(a, b):
    ah, al = _split(a)
    bh, bl = _split(b)
    return _dot(ah, bh) + _dot(ah, bl) + _dot(al, bh)


def _layer_norm(v, g, b):
    mu = jnp.mean(v, axis=-1, keepdims=True)
    d = v - mu
    var = jnp.mean(d * d, axis=-1, keepdims=True)
    return d * lax.rsqrt(var + LN_EPS) * g + b


def _inproj_kernel(x_ref, w1_ref, w2_ref, wvt_ref, qkv_ref, z_ref, ab_ref, qd_ref, kd_ref, vd_ref):
    xb = x_ref[...].astype(BF16)
    c = 3 * DN_W
    qkv_ref[...] = _dot(xb, w1_ref[:, 0:c])
    z_ref[...] = _dot(xb, w1_ref[:, c:c + DN_W])
    ab_ref[...] = _dot(xb, w1_ref[:, c + DN_W:c + DN_W + LANES])
    qd_ref[...] = (_dot(xb, w2_ref[:, 0:DF_W]) * (DF_D ** -0.5 * LOG2E)).astype(BF16)
    kd_ref[...] = _dot(xb, w2_ref[:, DF_W:2 * DF_W]).astype(BF16)
    vd_ref[...] = _dot_nt(wvt_ref[...], xb).astype(BF16)


def _inproj(x2, w1, w2, wvt, bsz, seq):
    t, d = x2.shape
    tm = min(TM_PROJ, seq)
    nt = seq // tm
    n1, n2 = w1.shape[1], w2.shape[1]
    row = lambda i: (i, 0)
    const = lambda i: (0, 0)
    return pl.pallas_call(
        _inproj_kernel,
        grid=(t // tm,),
        in_specs=[pl.BlockSpec((tm, d), row),
                  pl.BlockSpec((d, n1), const, pipeline_mode=pl.Buffered(1)),
                  pl.BlockSpec((d, n2), const, pipeline_mode=pl.Buffered(1)),
                  pl.BlockSpec((DF_W, d), const, pipeline_mode=pl.Buffered(1))],
        out_specs=[pl.BlockSpec((tm, 3 * DN_W), row), pl.BlockSpec((tm, DN_W), row),
                   pl.BlockSpec((tm, LANES), row), pl.BlockSpec((tm, DF_W), row),
                   pl.BlockSpec((tm, DF_W), row),
                   pl.BlockSpec((None, DF_W, tm), lambda i: (i // nt, 0, i % nt))],
        out_shape=[jax.ShapeDtypeStruct((t, 3 * DN_W), F32), jax.ShapeDtypeStruct((t, DN_W), F32),
                   jax.ShapeDtypeStruct((t, LANES), F32), jax.ShapeDtypeStruct((t, DF_W), BF16),
                   jax.ShapeDtypeStruct((t, DF_W), BF16), jax.ShapeDtypeStruct((bsz, DF_W, seq), BF16)],
        compiler_params=_cparams(("parallel",)),
        name="inproj",
    )(x2, w1, w2, wvt)


def _dn_kernel(qkv_ref, halo_ref, ab_ref, z_ref, convw_ref, par_ref, ea_ref, eb_ref, seg_ref,
               tri_ref, ones_ref, o_ref,
               xe_s, q_s, k_s, kb_s, vb_s, gc_s, eg_s, kdec_s, gl_s, od_s, qe_s, m_s, n_s, state_s):
    i = pl.program_id(1)
    ts = qkv_ref.shape[0]
    n_chunks = ts // DN_CHUNK

    @pl.when(i == 0)
    def _():
        state_s[...] = jnp.zeros_like(state_s)

    halo = halo_ref[...]
    xe_s[0:HALO, :] = jnp.where(i > 0, halo, jnp.zeros_like(halo))
    xe_s[HALO:, :] = qkv_ref[...]
    xe = xe_s[...]
    y = convw_ref[DN_CONV - 1:DN_CONV, :] * xe[HALO:, :]
    for tap in range(DN_CONV - 1):
        y += convw_ref[tap:tap + 1, :] * pltpu.roll(xe, DN_CONV - 1 - tap, axis=0)[HALO:, :]
    y = y * jax.nn.sigmoid(y)
    q = y[:, 0:DN_W]
    k = y[:, DN_W:2 * DN_W]
    v = y[:, 2 * DN_W:3 * DN_W]

    seg = seg_ref[...]
    q = q * lax.rsqrt(_dot((q * q).astype(BF16), seg) + 1e-6) * (DN_D ** -0.5)
    k = k * lax.rsqrt(_dot((k * k).astype(BF16), seg) + 1e-6)

    ab = ab_ref[...]
    xa = ab + par_ref[1:2, 0:LANES]
    softplus = jnp.maximum(xa, 0.0) + jnp.log(1.0 + jnp.exp(-jnp.abs(xa)))
    g = _mm_xc(-jnp.exp(par_ref[0:1, 0:LANES]) * softplus, ea_ref[...])
    beta = _mm_xc(jax.nn.sigmoid(ab), eb_ref[...])
    gc = _mm_cx(tri_ref[...], g)
    gl = _mm_cx(ones_ref[...], g)
    eg = jnp.exp(gc)
    kb = k * beta
    q_s[...] = q
    k_s[...] = k
    kb_s[...] = kb
    vb_s[...] = v * beta
    gc_s[...] = gc
    eg_s[...] = eg
    kdec_s[...] = k * jnp.exp(gl - gc)
    gl_s[...] = jnp.exp(gl)

    lane = lax.broadcasted_iota(jnp.int32, (DN_CHUNK, PAIR), 1)
    rowi = lax.broadcasted_iota(jnp.int32, (DN_CHUNK, PAIR), 0)
    colj = jnp.where(lane >= DN_D, lane - DN_D, lane)
    even = lane < DN_D
    eye2 = rowi == colj
    lower = rowi >= colj
    lane_b = lax.broadcasted_iota(jnp.int32, (PAIR, PAIR), 1)
    row_b = lax.broadcasted_iota(jnp.int32, (PAIR, PAIR), 0)
    bdmask = (lane_b < DN_D) == (row_b < DN_D)

    def bd(xm):
        z0 = jnp.zeros_like(xm)
        return jnp.concatenate([jnp.where(even, xm, z0), jnp.where(even, z0, xm)], axis=0)

    eye_f = jnp.where(eye2, 1.0, 0.0)
    sub_masks = []
    s = 1
    while s < DN_CHUNK:
        same = (rowi // (2 * s)) == (colj // (2 * s))
        sub_masks.append(same & ((rowi // s) % 2 == 1) & ((colj // s) % 2 == 0))
        s *= 2

    def local_body(cc, carry):
        chains = [(cc * DN_UNROLL + dc, p) for dc in range(DN_UNROLL) for p in range(N_PAIRS)]
        n = len(chains)
        rows = [pl.ds(pl.multiple_of(c * DN_CHUNK, DN_CHUNK), DN_CHUNK) for c, _ in chains]
        cols = [slice(p * PAIR, (p + 1) * PAIR) for _, p in chains]
        ld = lambda ref, i: ref[rows[i], cols[i]]

        aq = [_dot_nt(jnp.concatenate([ld(kb_s, i), ld(q_s, i)], axis=0).astype(BF16),
                      bd(ld(k_s, i).astype(BF16))) for i in range(n)]
        a_qk, l_m = [], []
        for i in range(n):
            gcc = ld(gc_s, i)
            gcj = jnp.sum(jnp.where(eye2, gcc, 0.0), axis=0, keepdims=True)
            dec = jnp.where(lower, jnp.exp(jnp.minimum(gcc - gcj, 0.0)), 0.0)
            a_qk.append((aq[i][DN_CHUNK:, :] * dec).astype(BF16))
            l_m.append(jnp.where(eye2, 0.0, aq[i][:DN_CHUNK, :] * dec))

        lb = [m.astype(BF16) for m in l_m]
        t_m = [eye_f - jnp.where(sub_masks[0], m, 0.0) for m in l_m]
        for lvl in range(1, len(sub_masks)):
            dc = [_dot(t_m[i].astype(BF16), bd(jnp.where(sub_masks[lvl], lb[i], jnp.zeros_like(lb[i]))))
                  for i in range(n)]
            t_m = [t_m[i] - _dot(dc[i].astype(BF16), bd(t_m[i].astype(BF16))) for i in range(n)]

        uw = []
        for i in range(n):
            kbg = (ld(kb_s, i) * ld(eg_s, i)).astype(BF16)
            rhs = jnp.concatenate([bd(ld(vb_s, i).astype(BF16)), bd(kbg)], axis=1)
            uw.append(_dot(t_m[i].astype(BF16), rhs).astype(BF16))
        qo = [_dot(a_qk[i], jnp.concatenate([bd(uw[i][:, PAIR:]), bd(uw[i][:, :PAIR])], axis=1))
              for i in range(n)]
        mn = [_dot_tn(ld(kdec_s, i).astype(BF16), jnp.concatenate([uw[i][:, PAIR:], uw[i][:, :PAIR]], axis=1))
              for i in range(n)]
        for i, (c, p) in enumerate(chains):
            qe_s[rows[i], cols[i]] = (ld(q_s, i) * ld(eg_s, i) - qo[i][:, :PAIR]).astype(BF16)
            od_s[rows[i], cols[i]] = qo[i][:, PAIR:]
            m_s[c, p] = jnp.where(bdmask, mn[i][:, :PAIR], 0.0).astype(BF16)
            n_s[c, p] = jnp.where(bdmask, mn[i][:, PAIR:], 0.0)
        return carry

    lax.fori_loop(0, n_chunks // DN_UNROLL, local_body, 0)

    for c in range(n_chunks):
        rows = slice(c * DN_CHUNK, (c + 1) * DN_CHUNK)
        for p in range(N_PAIRS):
            cols = slice(p * PAIR, (p + 1) * PAIR)
            st = state_s[p]
            r = _dot(jnp.concatenate([qe_s[rows, cols], m_s[c, p]], axis=0), st.astype(BF16))
            od_s[rows, cols] += r[:DN_CHUNK, :]
            state_s[p] = gl_s[c * DN_CHUNK:c * DN_CHUNK + 1, cols] * st - r[DN_CHUNK:, :] + n_s[c, p]

    od = od_s[...]
    ms = _dot((od * od).astype(BF16), seg) * (1.0 / DN_D)
    zz = z_ref[...]
    o_ref[...] = (od * lax.rsqrt(ms + RMS_EPS) * par_ref[2:3, :] * (zz * jax.nn.sigmoid(zz))).astype(BF16)


def _deltanet(qkv, ab, z, convw, par, consts, bsz, seq):
    t = qkv.shape[0]
    ts = min(TS_DN, seq)
    nt = seq // ts
    hb = ts // HALO
    ea, eb, seg, tri, ones = consts
    row = lambda b, i: (b * nt + i, 0)
    const = lambda b, i: (0, 0)
    halo_map = lambda b, i: (jnp.maximum((b * nt + i) * hb - 1, 0), 0)
    cspec = lambda a: pl.BlockSpec(a.shape, const, pipeline_mode=pl.Buffered(1))
    big = lambda: pltpu.VMEM((ts, DN_W), F32)
    return pl.pallas_call(
        _dn_kernel,
        grid=(bsz, nt),
        in_specs=[pl.BlockSpec((ts, 3 * DN_W), row), pl.BlockSpec((HALO, 3 * DN_W), halo_map),
                  pl.BlockSpec((ts, LANES), row), pl.BlockSpec((ts, DN_W), row),
                  cspec(convw), cspec(par), cspec(ea), cspec(eb), cspec(seg), cspec(tri), cspec(ones)],
        out_specs=pl.BlockSpec((ts, DN_W), row),
        out_shape=jax.ShapeDtypeStruct((t, DN_W), BF16),
        scratch_shapes=[pltpu.VMEM((ts + HALO, 3 * DN_W), F32)] + [big() for _ in range(9)]
                       + [pltpu.VMEM((ts, DN_W), BF16),
                          pltpu.VMEM((ts // DN_CHUNK, N_PAIRS, PAIR, PAIR), BF16),
                          pltpu.VMEM((ts // DN_CHUNK, N_PAIRS, PAIR, PAIR), F32),
                          pltpu.VMEM((N_PAIRS, PAIR, PAIR), F32)],
        compiler_params=_cparams(("parallel", "arbitrary")),
        name="deltanet",
    )(qkv, qkv, ab, z, convw, par, ea, eb, seg, tri, ones)


def _df_kernel(sc_ref, q_ref, k_ref, vt_ref, bt_ref, w_ref, o_ref, sa_ref, sb_ref, acc_ref):
    qi = pl.program_id(2)
    tq = q_ref.shape[0]
    lane = lax.broadcasted_iota(jnp.int32, (tq, 2 * DF_D), 1)
    q = q_ref[...]
    zq = jnp.zeros_like(q)
    qs = (jnp.where(lane < DF_D, q, zq), jnp.where(lane < DF_D, zq, q))
    acc_ref[...] = jnp.zeros_like(acc_ref)

    def scores(j, s_ref):
        kk = k_ref[pl.ds(pl.multiple_of(j * tq, tq), tq), :]
        for mp in range(2):
            s_ref[mp] = _dot_nt(kk, qs[mp])

    def absorb(j, s_ref, carry):
        vt = vt_ref[:, pl.ds(pl.multiple_of(j * tq, tq), tq)]
        out = []
        for mp in range(2):
            m_old, l_old = carry[mp]
            st = s_ref[mp]
            m_new = jnp.maximum(m_old, jnp.max(st, axis=0, keepdims=True))
            alpha = jnp.exp2(m_old - m_new)
            pr = jnp.exp2(st - m_new)
            l_new = alpha * l_old + jnp.sum(pr, axis=0, keepdims=True)
            acc_ref[mp] = alpha * acc_ref[mp] + _dot(vt, pr.astype(BF16))
            out.append((m_new, l_new))
        return tuple(out)

    def absorb_prev(j, s_ref, carry):
        c0 = tq - LANES
        for mp in range(2):
            s_ref[mp, c0:, 0:LANES] += bt_ref[1, c0:, 0:LANES]
        return absorb(j, s_ref, carry)

    bw = tq // DF_DIAG_BANDS
    d0 = pl.multiple_of(qi * tq, tq)
    band_keys = [pl.ds(pl.multiple_of(d0 + r * bw, bw), bw) for r in range(DF_DIAG_BANDS)]

    def scores_diag(s_ref):
        for r in range(DF_DIAG_BANDS):
            c0 = r * bw
            kk = k_ref[band_keys[r], :]
            for mp in range(2):
                s_ref[mp, c0:c0 + bw, c0:] = _dot_nt(kk, qs[mp][c0:, :])

    def absorb_diag(s_ref, carry):
        out = []
        for mp in range(2):
            m_old, l_old = carry[mp]
            bands = []
            m_new = m_old
            for r in range(DF_DIAG_BANDS):
                c0 = r * bw
                band = s_ref[mp, c0:c0 + bw, c0:] + bt_ref[0, c0:c0 + bw, c0:]
                bands.append(band)
                m_hi = jnp.maximum(m_new[:, c0:], jnp.max(band, axis=0, keepdims=True))
                m_new = m_hi if c0 == 0 else jnp.concatenate([m_new[:, :c0], m_hi], axis=1)
            alpha = jnp.exp2(m_old - m_new)
            acc_ref[mp] = alpha * acc_ref[mp]
            l_new = alpha * l_old
            for r in range(DF_DIAG_BANDS):
                c0 = r * bw
                pr = jnp.exp2(bands[r] - m_new[:, c0:])
                l_band = jnp.sum(pr, axis=0, keepdims=True)
                l_new = l_new + (l_band if c0 == 0 else
                                 jnp.concatenate([jnp.zeros((1, c0), F32), l_band], axis=1))
                acc_ref[mp, :, c0:] += _dot(vt_ref[:, band_keys[r]], pr.astype(BF16))
            out.append((m_new, l_new))
        return tuple(out)

    init1 = (jnp.full((1, tq), -jnp.inf, F32), jnp.zeros((1, tq), F32))
    carry = (init1, init1)
    n_far = jnp.maximum(qi - 1, 0)
    odd = n_far % 2

    @pl.when(qi == 0)
    def _():
        scores_diag(sb_ref)

    @pl.when(jnp.logical_and(qi > 0, odd == 0))
    def _():
        scores(0, sa_ref)

    def odd_step(_, c):
        scores(0, sb_ref)
        scores(1, sa_ref)
        return absorb(0, sb_ref, c)

    carry = lax.fori_loop(0, odd, odd_step, carry)

    def pair_step(jp, c):
        j = odd + 2 * jp
        scores(j + 1, sb_ref)
        c = absorb(j, sa_ref, c)
        scores(j + 2, sa_ref)
        return absorb(j + 1, sb_ref, c)

    carry = lax.fori_loop(0, n_far // 2, pair_step, carry)

    def tail_step(j, c):
        scores_diag(sb_ref)
        c = absorb_prev(j, sa_ref, c)
        return absorb_diag(sb_ref, c)

    carry = lax.fori_loop(n_far, qi, tail_step, carry)
    carry = lax.fori_loop(0, jnp.where(qi == 0, 1, 0), lambda _, c: absorb_diag(sb_ref, c), carry)

    (_, l0), (_, l1) = carry
    ot = acc_ref[0] / l0 - sc_ref[0] * (acc_ref[1] / l1)
    ms = jnp.mean(ot * ot, axis=0, keepdims=True)
    ot = ot * lax.rsqrt(ms + RMS_EPS) * w_ref[...] * sc_ref[1]
    o_ref[...] = ot.T.astype(BF16)


def _attn_bias_tiles(rel_bias, seq, tq):
    rb = rel_bias.astype(F32)
    far = rb[_t5_causal_bucket(jnp.asarray(seq - 1, jnp.int32))]
    kj = jnp.arange(tq, dtype=jnp.int32)[:, None]
    qi = jnp.arange(tq, dtype=jnp.int32)[None, :]
    dist = jnp.stack([qi - kj, qi + tq - kj])
    onehot = (_t5_causal_bucket(jnp.maximum(dist, 0))[..., None]
              == jnp.arange(REL_BUCKETS, dtype=jnp.int32)).astype(F32)
    vals = (jnp.einsum('ntqb,bh->ntqh', onehot, rb, precision=lax.Precision.HIGHEST) - far) * LOG2E
    vals = jnp.where((dist >= 0)[..., None], vals, -jnp.inf)
    return jnp.transpose(vals, (3, 0, 1, 2))


def _diff_attention(qd, kd, vdt, btiles, subln_w, scalars, bsz, seq):
    t = qd.shape[0]
    tq = min(TQ_DF, seq)
    nq = seq // tq
    sbuf = pltpu.VMEM((2, tq, tq), F32)
    return pl.pallas_call(
        _df_kernel,
        grid_spec=pltpu.PrefetchScalarGridSpec(
            num_scalar_prefetch=1,
            grid=(bsz, DF_HEADS, nq),
            in_specs=[pl.BlockSpec((tq, 2 * DF_D), lambda b, h, i, sc: (b * nq + i, h)),
                      pl.BlockSpec((seq, 2 * DF_D), lambda b, h, i, sc: (b, h)),
                      pl.BlockSpec((None, 2 * DF_D, seq), lambda b, h, i, sc: (b, h, 0)),
                      pl.BlockSpec((None, 2, tq, tq), lambda b, h, i, sc: (h, 0, 0, 0)),
                      pl.BlockSpec((2 * DF_D, 1), lambda b, h, i, sc: (0, 0))],
            out_specs=pl.BlockSpec((tq, 2 * DF_D), lambda b, h, i, sc: (b * nq + i, h)),
            scratch_shapes=[sbuf, sbuf, pltpu.VMEM((2, 2 * DF_D, tq), F32)],
        ),
        out_shape=jax.ShapeDtypeStruct((t, DF_W), BF16),
        compiler_params=_cparams(("parallel", "parallel", "arbitrary")),
        name="diffattn",
    )(scalars, qd, kd, vdt, btiles, subln_w)


def _outproj_kernel(alpha, x_ref, a_ref, b_ref, w_ref, g_ref, bb_ref, o_ref):
    mix = _dot(a_ref[...], w_ref[0:DN_W, :]) + _dot(b_ref[...], w_ref[DN_W:, :])
    o_ref[...] = _layer_norm(alpha * x_ref[...] + mix, g_ref[...], bb_ref[...])


def _outproj_ln(x2, o_dn, o_df, w_out, g, b, alpha):
    t, d = x2.shape
    tm = min(2 * TM_LN, t)
    row = lambda i: (i, 0)
    const = lambda i: (0, 0)
    return pl.pallas_call(
        functools.partial(_outproj_kernel, alpha),
        grid=(t // tm,),
        in_specs=[pl.BlockSpec((tm, d), row), pl.BlockSpec((tm, DN_W), row), pl.BlockSpec((tm, DF_W), row),
                  pl.BlockSpec(w_out.shape, const, pipeline_mode=pl.Buffered(1)),
                  pl.BlockSpec((1, d), const), pl.BlockSpec((1, d), const)],
        out_specs=pl.BlockSpec((tm, d), row),
        out_shape=jax.ShapeDtypeStruct((t, d), F32),
        compiler_params=_cparams(("parallel",)),
        name="outproj_ln",
    )(x2, o_dn, o_df, w_out, g, b)


def _swiglu_acc(xparts, wg_ref, wu_ref, wd_ref, acc_ref):
    d_ff = wg_ref.shape[-1]
    for c0 in range(0, d_ff, FF_CHUNK):
        cs = slice(c0, c0 + FF_CHUNK)
        hg = sum(_dot(xp, wg_ref[k0:k0 + xp.shape[1], cs]) for xp, k0 in xparts)
        hu = sum(_dot(xp, wu_ref[k0:k0 + xp.shape[1], cs]) for xp, k0 in xparts)
        hh = (hg * jax.nn.sigmoid(hg) * hu).astype(BF16)
        contrib = _dot(hh, wd_ref[cs, :])
        if c0 == 0:
            acc_ref[...] = contrib
        else:
            acc_ref[...] += contrib


def _ffn_kernel(alpha, x_ref, wg_ref, wu_ref, wd_ref, g_ref, b_ref, o_ref, acc_ref):
    x = x_ref[...]
    _swiglu_acc([(x.astype(BF16), 0)], wg_ref, wu_ref, wd_ref, acc_ref)
    o_ref[...] = _layer_norm(alpha * x + acc_ref[...], g_ref[...], b_ref[...])


def _ffn_ln(x2, wg, wu, wd, g, b, alpha):
    t, d = x2.shape
    tm = min(TM_FFN, t)
    row = lambda i: (i, 0)
    const = lambda i: (0, 0)
    wspec = lambda a: pl.BlockSpec(a.shape, const, pipeline_mode=pl.Buffered(1))
    return pl.pallas_call(
        functools.partial(_ffn_kernel, alpha),
        grid=(t // tm,),
        in_specs=[pl.BlockSpec((tm, d), row), wspec(wg), wspec(wu), wspec(wd),
                  pl.BlockSpec((1, d), const), pl.BlockSpec((1, d), const)],
        out_specs=pl.BlockSpec((tm, d), row),
        out_shape=jax.ShapeDtypeStruct((t, d), F32),
        scratch_shapes=[pltpu.VMEM((tm, d), F32)],
        compiler_params=_cparams(("parallel",)),
        name="ffn_ln",
    )(x2, wg, wu, wd, g, b)


def _pack_halves(x):
    h = x.shape[1] // 2
    hi = lax.bitcast_convert_type(x[:, :h].astype(BF16).astype(F32), jnp.uint32)
    lo = lax.bitcast_convert_type(x[:, h:].astype(BF16).astype(F32), jnp.uint32)
    return hi | (lo >> 16)


def _unpack_halves(p):
    a = lax.bitcast_convert_type(p & jnp.uint32(0xFFFF0000), F32)
    b = lax.bitcast_convert_type(p << 16, F32)
    return a, b


def _router_kernel(x_ref, wr_ref, o_ref, xp_ref):
    x = x_ref[...]
    xp_ref[...] = _pack_halves(x)
    logits = _mm3(x, wr_ref[...])
    lane = lax.broadcasted_iota(jnp.int32, logits.shape, 1)
    lg = jnp.where(lane < N_EXPERTS, logits, -jnp.inf)
    m1 = jnp.max(lg, axis=-1, keepdims=True)
    i1 = jnp.min(jnp.where(lg == m1, lane, LANES), axis=-1, keepdims=True)
    lg2 = jnp.where(lane == i1, -jnp.inf, lg)
    m2 = jnp.max(lg2, axis=-1, keepdims=True)
    i2 = jnp.min(jnp.where(lg2 == m2, lane, LANES), axis=-1, keepdims=True)
    e = jnp.exp(m2 - m1)
    g1 = 1.0 / (1.0 + e)
    g2 = e / (1.0 + e)
    out = jnp.where(lane == 0, i1.astype(F32), 0.0)
    out = jnp.where(lane == 1, i2.astype(F32), out)
    out = jnp.where(lane == 2, g1, out)
    o_ref[...] = jnp.where(lane == 3, g2, out)


def _router(x2, wr):
    t, d = x2.shape
    tm = min(TM_LN, t)
    row = lambda i: (i, 0)
    return pl.pallas_call(
        _router_kernel,
        grid=(t // tm,),
        in_specs=[pl.BlockSpec((tm, d), row), pl.BlockSpec((d, LANES), lambda i: (0, 0))],
        out_specs=[pl.BlockSpec((tm, LANES), row), pl.BlockSpec((tm, d // 2), row)],
        out_shape=[jax.ShapeDtypeStruct((t, LANES), F32), jax.ShapeDtypeStruct((t, d // 2), jnp.uint32)],
        compiler_params=_cparams(("parallel",)),
        name="router",
    )(x2, wr)


def _sc_mesh():
    return plsc.VectorSubcoreMesh(core_axis_name="core", subcore_axis_name="subcore")


def _sc_index_rows(idx):
    win = idx.reshape(-1, SC_WIN)
    return jnp.concatenate([win, jnp.zeros((win.shape[0], SC_IDX_TILE - SC_WIN), idx.dtype)], axis=1)


def _sc_dispatch(xp, slot0, slot1, n_slot):
    t, c = xp.shape
    half = t // SC_WIN // 2
    idx_spec = pl.BlockSpec((1, SC_IDX_TILE), lambda cc, i: (cc * half + i, 0))

    @pl.kernel(out_type=jax.ShapeDtypeStruct((n_slot, c), xp.dtype), mesh=_sc_mesh(), scratch_types=[],
               name="moe_dispatch")
    def run(x_hbm, i0_hbm, i1_hbm, o_hbm):
        def body(x_vmem, i0_vmem, i1_vmem):
            pltpu.sync_copy(x_vmem, o_hbm.at[i0_vmem.at[0, pl.ds(0, SC_WIN)]])
            pltpu.sync_copy(x_vmem, o_hbm.at[i1_vmem.at[0, pl.ds(0, SC_WIN)]])

        pltpu.emit_pipeline(
            body,
            grid=(2, half),
            in_specs=[pl.BlockSpec((SC_WIN, c), lambda cc, i: (cc * half + i, 0)), idx_spec, idx_spec],
            out_specs=[],
            core_axis_name=("core", "subcore"),
            dimension_semantics=(pltpu.PARALLEL, pltpu.PARALLEL),
        )(x_hbm, i0_hbm, i1_hbm)

    return run(xp, _sc_index_rows(slot0), _sc_index_rows(slot1))


def _sc_gather(yp, idx):
    t = idx.shape[0]
    c = yp.shape[1]
    half = t // SC_WIN // 2

    @pl.kernel(out_type=jax.ShapeDtypeStruct((t, c), yp.dtype), mesh=_sc_mesh(), scratch_types=[],
               name="moe_gather")
    def run(y_hbm, i_hbm, o_hbm):
        def body(i_vmem, o_vmem):
            pltpu.sync_copy(y_hbm.at[i_vmem.at[0, pl.ds(0, SC_WIN)]], o_vmem)

        pltpu.emit_pipeline(
            body,
            grid=(2, half),
            in_specs=[pl.BlockSpec((1, SC_IDX_TILE), lambda cc, i: (cc * half + i, 0))],
            out_specs=[pl.BlockSpec((SC_WIN, c), lambda cc, i: (cc * half + i, 0))],
            core_axis_name=("core", "subcore"),
            dimension_semantics=(pltpu.PARALLEL, pltpu.PARALLEL),
        )(i_hbm, o_hbm)

    return run(yp, _sc_index_rows(idx))


def _expert_kernel(be_ref, nv_ref, x_ref, wg_ref, wu_ref, wd_ref, o_ref, acc_ref):
    i = pl.program_id(0)
    n_valid = nv_ref[i]

    @pl.when(n_valid > 0)
    def _():
        row = lax.broadcasted_iota(jnp.int32, x_ref.shape, 0)
        xa, xb = _unpack_halves(jnp.where(row < n_valid, x_ref[...], jnp.uint32(0)))
        h = xa.shape[1]
        _swiglu_acc([(xa.astype(BF16), 0), (xb.astype(BF16), h)], wg_ref, wu_ref, wd_ref, acc_ref)
        o_ref[...] = _pack_halves(acc_ref[...])

    @pl.when(n_valid <= 0)
    def _():
        o_ref[...] = jnp.zeros_like(o_ref)


def _experts(xbp, blk_e, n_valid, wg, wu, wd):
    n_slot, dh = xbp.shape
    d = 2 * dh
    n_blk = n_slot // MOE_BLK
    dff = wg.shape[-1]
    row = lambda i, be, nv: (i, 0)
    wmap = lambda i, be, nv: (be[i], 0, 0)
    return pl.pallas_call(
        _expert_kernel,
        grid_spec=pltpu.PrefetchScalarGridSpec(
            num_scalar_prefetch=2,
            grid=(n_blk,),
            in_specs=[pl.BlockSpec((MOE_BLK, dh), row),
                      pl.BlockSpec((None, d, dff), wmap, pipeline_mode=pl.Buffered(1)),
                      pl.BlockSpec((None, d, dff), wmap, pipeline_mode=pl.Buffered(1)),
                      pl.BlockSpec((None, dff, d), wmap, pipeline_mode=pl.Buffered(1))],
            out_specs=pl.BlockSpec((MOE_BLK, dh), row),
            scratch_shapes=[pltpu.VMEM((MOE_BLK, d), F32)],
        ),
        out_shape=jax.ShapeDtypeStruct((n_slot, dh), jnp.uint32),
        compiler_params=_cparams(("arbitrary",)),
        name="experts",
    )(blk_e, n_valid, xbp, wg, wu, wd)


def _combine_kernel(alpha, x_ref, y0_ref, y1_ref, r_ref, g_ref, b_ref, o_ref):
    r = r_ref[...]
    a0, b0 = _unpack_halves(y0_ref[...])
    a1, b1 = _unpack_halves(y1_ref[...])
    g0 = r[:, 2:3]
    g1 = r[:, 3:4]
    f = jnp.concatenate([g0 * a0 + g1 * a1, g0 * b0 + g1 * b1], axis=1)
    o_ref[...] = _layer_norm(alpha * x_ref[...] + f, g_ref[...], b_ref[...])


def _combine_ln(x2, y0, y1, route, g, b, alpha):
    t, d = x2.shape
    tm = min(TM_LN, t)
    row = lambda i: (i, 0)
    const = lambda i: (0, 0)
    return pl.pallas_call(
        functools.partial(_combine_kernel, alpha),
        grid=(t // tm,),
        in_specs=[pl.BlockSpec((tm, d), row), pl.BlockSpec((tm, d // 2), row), pl.BlockSpec((tm, d // 2), row),
                  pl.BlockSpec((tm, LANES), row), pl.BlockSpec((1, d), const), pl.BlockSpec((1, d), const)],
        out_specs=pl.BlockSpec((tm, d), row),
        out_shape=jax.ShapeDtypeStruct((t, d), F32),
        compiler_params=_cparams(("parallel",)),
        name="combine_ln",
    )(x2, y0, y1, route, g, b)


def _moe(x2, wr, wg, wu, wd, g, b, alpha):
    t, d = x2.shape
    route, xp = _router(x2, wr)
    top_idx = route[:, 0:2].astype(jnp.int32)
    n_asg = t * TOP_K
    flat_e = top_idx.reshape(n_asg)
    onehot = (flat_e[:, None] == jnp.arange(N_EXPERTS, dtype=jnp.int32)[None, :]).astype(jnp.int32)
    csum = jnp.cumsum(onehot, axis=0)
    rank = jnp.sum((csum - onehot) * onehot, axis=1)
    counts = csum[-1]
    padded = (counts + MOE_BLK - 1) // MOE_BLK * MOE_BLK
    pad_end = jnp.cumsum(padded)
    pad_start = pad_end - padded
    slot = (pad_start[flat_e] + rank).reshape(t, TOP_K)
    n_slot = -(-n_asg // MOE_BLK) * MOE_BLK + N_EXPERTS * MOE_BLK
    n_blk = n_slot // MOE_BLK
    blk_start = jnp.arange(n_blk, dtype=jnp.int32) * MOE_BLK
    blk_e = jnp.minimum(jnp.searchsorted(pad_end, blk_start, side='right'), N_EXPERTS - 1).astype(jnp.int32)
    n_valid = jnp.clip(pad_start[blk_e] + counts[blk_e] - blk_start, 0, MOE_BLK).astype(jnp.int32)
    slot0 = slot[:, 0]
    slot1 = slot[:, 1]
    xbp = _sc_dispatch(xp, slot0, slot1, n_slot)
    ybp = _experts(xbp, blk_e, n_valid, wg, wu, wd)
    return _combine_ln(x2, _sc_gather(ybp, slot0), _sc_gather(ybp, slot1), route, g, b, alpha)


def _t5_causal_bucket(dist):
    max_exact = REL_BUCKETS // 2
    d = jnp.maximum(dist, 1).astype(F32)
    large = max_exact + (jnp.log(d / max_exact) / math.log(REL_MAX_DIST / max_exact)
                         * (REL_BUCKETS - max_exact)).astype(jnp.int32)
    large = jnp.minimum(large, REL_BUCKETS - 1)
    return jnp.where(dist < max_exact, dist, large)


def _dn_constants(ts):
    lanes = np.arange(DN_W)
    ea = np.zeros((LANES, DN_W), np.float32)
    eb = np.zeros((LANES, DN_W), np.float32)
    ea[lanes // DN_D, lanes] = 1.0
    eb[DN_HEADS + lanes // DN_D, lanes] = 1.0
    seg = (lanes[:, None] // DN_D == lanes[None, :] // DN_D).astype(np.float32)
    r = np.arange(ts)
    same = r[:, None] // DN_CHUNK == r[None, :] // DN_CHUNK
    tri = (same & (r[:, None] >= r[None, :])).astype(np.float32)
    ones = same.astype(np.float32)
    return tuple(jnp.asarray(a, BF16) for a in (ea, eb, seg, tri, ones))


def kernel(x, w_in, w_out, conv_w, dn_a_log, dn_dt_bias, dn_norm_w, df_lambda, df_subln_w, rel_bias,
           ln1_g, ln1_b, ln2_g, ln2_b, ffn_w_gate, ffn_w_up, ffn_w_down, moe_router, moe_w_gate,
           moe_w_up, moe_w_down):
    bsz, seq, d = x.shape
    depth = w_in.shape[0]
    alpha = (2 * depth) ** 0.25
    t = bsz * seq
    tq = min(TQ_DF, seq)
    assert REL_MAX_DIST <= LANES and tq % (DF_DIAG_BANDS * LANES) == 0
    assert seq % tq == 0 and seq % min(TS_DN, seq) == 0
    btiles = _attn_bias_tiles(rel_bias, seq, tq)

    dn_consts = _dn_constants(min(TS_DN, seq))
    c_dn = 3 * DN_W
    x2 = x.reshape(t, d)
    for layer in range(depth):
        lambda_init = 0.8 - 0.6 * math.exp(-0.3 * layer)
        wl = w_in[layer]
        n1 = c_dn + DN_W + 2 * DN_HEADS
        w1 = jnp.concatenate([wl[:, :n1], jnp.zeros((d, LANES - 2 * DN_HEADS), F32)], axis=1).astype(BF16)
        w2 = wl[:, n1:n1 + 2 * DF_W].astype(BF16)
        wvt = wl[:, n1 + 2 * DF_W:].T.astype(BF16)
        qkv, z, ab, qd, kd, vdt = _inproj(x2, w1, w2, wvt, bsz, seq)

        convw = jnp.concatenate([conv_w[layer], jnp.zeros((HALO - DN_CONV, c_dn), F32)], axis=0)
        par = jnp.zeros((HALO, DN_W), F32)
        par = par.at[0, 0:DN_HEADS].set(dn_a_log[layer])
        par = par.at[1, 0:DN_HEADS].set(dn_dt_bias[layer])
        par = par.at[2].set(jnp.tile(dn_norm_w[layer], DN_HEADS))
        o_dn = _deltanet(qkv, ab, z, convw, par, dn_consts, bsz, seq)

        lf = df_lambda[layer].astype(F32)
        lam = jnp.exp(jnp.sum(lf[0] * lf[1])) - jnp.exp(jnp.sum(lf[2] * lf[3])) + lambda_init
        scalars = jnp.stack([lam, jnp.asarray(1.0 - lambda_init, F32)]).astype(F32)
        o_df = _diff_attention(qd, kd, vdt, btiles, df_subln_w[layer].reshape(2 * DF_D, 1), scalars, bsz, seq)

        i = layer // 2
        g1 = ln1_g[layer].reshape(1, d)
        b1 = ln1_b[layer].reshape(1, d)
        g2 = ln2_g[layer].reshape(1, d)
        b2 = ln2_b[layer].reshape(1, d)
        x2 = _outproj_ln(x2, o_dn, o_df, w_out[layer].astype(BF16), g1, b1, alpha)
        if layer % 2 == 0:
            x2 = _ffn_ln(x2, ffn_w_gate[i].astype(BF16), ffn_w_up[i].astype(BF16), ffn_w_down[i].astype(BF16),
                         g2, b2, alpha)
        else:
            wr = jnp.concatenate([moe_router[i], jnp.zeros((d, LANES - N_EXPERTS), F32)], axis=1)
            x2 = _moe(x2, wr, moe_w_gate[i].astype(BF16), moe_w_up[i].astype(BF16), moe_w_down[i].astype(BF16),
                      g2, b2, alpha)
    return x2.reshape(bsz, seq, d)
```
